```python
import math
import jax
import jax.numpy as jnp
from jax import lax
import numpy as np

D_MODEL = 1024
BATCH = 16
SEQ = 2048
DEPTH = 2

GRID_W = 64
CTX_LEN = 256
N_EVEN = (DEPTH + 1) // 2
N_ODD = DEPTH // 2
N_MOD = 6
D_FF = 4 * D_MODEL
HEAD_DIM = D_MODEL // 16
MLA_HEADS = 8
MLA_NOPE = HEAD_DIM
MLA_ROPE = HEAD_DIM // 2
MLA_V = HEAD_DIM
MLA_Q_LORA = 3 * D_MODEL // 8
MLA_KV_LORA = D_MODEL // 4
GQA_HEADS = 8
GQA_KV_HEADS = 2
GQA_DIM = HEAD_DIM
ATTN_PARTS = (MLA_Q_LORA, MLA_KV_LORA, MLA_ROPE,
              GQA_HEADS * GQA_DIM, GQA_KV_HEADS * GQA_DIM, GQA_KV_HEADS * GQA_DIM)
ATTN_IN = sum(ATTN_PARTS)
ATTN_SPLITS = tuple(sum(ATTN_PARTS[:i + 1]) for i in range(len(ATTN_PARTS) - 1))
MIX_WIDTH = MLA_HEADS * MLA_V + GQA_HEADS * GQA_DIM
ROPE_THETA = 10000.0
Q_BLOCK = 128
HYENA_ORDER = 2
HYENA_SHORT = 3
HYENA_BANDS = 8
HYENA_EMB = 1 + 2 * HYENA_BANDS
HYENA_FILTER_HIDDEN = 64
HYENA_FAST_DECAY = 0.3
HYENA_SLOW_DECAY = 1.5
HYENA_TARGET = 1e-2
EPS = 1e-6

kernel_name = "hybrid_mla_gqa_hyena_dit"

F32 = jnp.float32


def rmsnorm(x, g):
    xf = x.astype(F32)
    y = xf * lax.rsqrt(jnp.mean(xf * xf, axis=-1, keepdims=True) + EPS)
    return (y * g.astype(F32)).astype(x.dtype)


def modulate(x, g, shift, scale):
    return rmsnorm(x, g) * (1 + scale) + shift


def _rotate(x, ang):
    n = ang.shape[-1]
    cos = jnp.cos(ang)[None, :, None, :]
    sin = jnp.sin(ang)[None, :, None, :]
    a = x[..., :n].astype(F32)
    b = x[..., n:].astype(F32)
    return jnp.concatenate([a * cos - b * sin, a * sin + b * cos], axis=-1).astype(x.dtype)


def rope_2d(x, row, col):
    half = x.shape[-1] // 2
    nf = half // 2
    inv = ROPE_THETA ** (-jnp.arange(nf, dtype=F32) / nf)
    ang_r = row.astype(F32)[:, None] * inv[None]
    ang_c = col.astype(F32)[:, None] * inv[None]
    return jnp.concatenate([_rotate(x[..., :half], ang_r), _rotate(x[..., half:], ang_c)], axis=-1)


def block_attention(q, k, v):
    B, S, Hq, dk = q.shape
    Hk, dv = k.shape[2], v.shape[-1]
    G = Hq // Hk
    nb = S // Q_BLOCK
    scale = 1.0 / math.sqrt(dk)
    qb = q.reshape(B, nb, Q_BLOCK, Hk, G, dk).transpose(1, 0, 2, 3, 4, 5)

    def one(qblk):
        s = jnp.einsum('bqhgd,bkhd->bhgqk', qblk, k, preferred_element_type=F32) * scale
        p = jax.nn.softmax(s, axis=-1).astype(v.dtype)
        return jnp.einsum('bhgqk,bkhe->bqhge', p, v)

    o = lax.map(one, qb)
    return o.transpose(1, 0, 2, 3, 4, 5).reshape(B, S, Hq, dv)


def attn_heads(z, q_a_g, w_q_b, kv_a_g, w_kv_b, q_norm_g, k_norm_g, pos):
    B, L, _ = z.shape
    cq, ckv, kr, gq, gk, gv = jnp.split(z, ATTN_SPLITS, axis=-1)
    q = (rmsnorm(cq, q_a_g) @ w_q_b).reshape(B, L, MLA_HEADS, MLA_NOPE + MLA_ROPE)
    q_nope, q_rope = q[..., :MLA_NOPE], q[..., MLA_NOPE:]
    kv = (rmsnorm(ckv, kv_a_g) @ w_kv_b).reshape(B, L, MLA_HEADS, MLA_NOPE + MLA_V)
    k_nope, v_mla = kv[..., :MLA_NOPE], kv[..., MLA_NOPE:]
    k_rope = kr[:, :, None, :]
    gq = rmsnorm(gq.reshape(B, L, GQA_HEADS, GQA_DIM), q_norm_g)
    gk = rmsnorm(gk.reshape(B, L, GQA_KV_HEADS, GQA_DIM), k_norm_g)
    gv = gv.reshape(B, L, GQA_KV_HEADS, GQA_DIM)
    if pos is not None:
        row, col = pos
        q_rope = rope_2d(q_rope, row, col)
        k_rope = rope_2d(k_rope, row, col)
        gq = rope_2d(gq, row, col)
        gk = rope_2d(gk, row, col)
    q_mla = jnp.concatenate([q_nope, q_rope], axis=-1)
    k_mla = jnp.concatenate([k_nope, jnp.broadcast_to(k_rope, (B, L, MLA_HEADS, MLA_ROPE))], axis=-1)
    return q_mla, k_mla, v_mla, gq, gk, gv


def attention_mixer(hl, hc, w_in, q_a_g, w_q_b, kv_a_g, w_kv_b, q_norm_g, k_norm_g, w_out,
                    row, col, need_ctx):
    B, S, _ = hl.shape
    lat = attn_heads(hl @ w_in, q_a_g, w_q_b, kv_a_g, w_kv_b, q_norm_g, k_norm_g, (row, col))
    cx = attn_heads(hc @ w_in, q_a_g, w_q_b, kv_a_g, w_kv_b, q_norm_g, k_norm_g, None)
    k_mla = jnp.concatenate([cx[1], lat[1]], axis=1)
    v_mla = jnp.concatenate([cx[2], lat[2]], axis=1)
    k_gqa = jnp.concatenate([cx[4], lat[4]], axis=1)
    v_gqa = jnp.concatenate([cx[5], lat[5]], axis=1)
    o_lat = jnp.concatenate([
        block_attention(lat[0], k_mla, v_mla).reshape(B, S, MLA_HEADS * MLA_V),
        block_attention(lat[3], k_gqa, v_gqa).reshape(B, S, GQA_HEADS * GQA_DIM)], axis=-1) @ w_out
    o_ctx = None
    if need_ctx:
        Lc = hc.shape[1]
        o_ctx = jnp.concatenate([
            block_attention(cx[0], cx[1], cx[2]).reshape(B, Lc, MLA_HEADS * MLA_V),
            block_attention(cx[3], cx[4], cx[5]).reshape(B, Lc, GQA_HEADS * GQA_DIM)], axis=-1) @ w_out
    return o_lat, o_ctx


def short_conv(z, w, b):
    L = z.shape[1]
    p = HYENA_SHORT // 2
    zp = jnp.pad(z, ((0, 0), (p, p), (0, 0)))
    out = b
    for j in range(HYENA_SHORT):
        out = out + zp[:, j:j + L] * w[j]
    return out


def hyena_filters(L, f_w1, f_b1, f_w2, f_b2, f_w3, f_b3, f_freq, f_w4):
    t = jnp.arange(L, dtype=F32)
    t_norm = t / L
    w = 2.0 * math.pi * t / L
    bands = jnp.linspace(1e-4, HYENA_BANDS - 1, HYENA_BANDS, dtype=F32)
    fw = w[:, None] * bands[None]
    feats = jnp.concatenate([t_norm[:, None], jnp.cos(fw), -jnp.sin(fw)], axis=-1)
    fr = f_freq.astype(F32)
    h = jnp.sin(fr[0] * (feats @ f_w1.astype(F32) + f_b1.astype(F32)))
    h = jnp.sin(fr[1] * (h @ f_w2.astype(F32) + f_b2.astype(F32)))
    h = jnp.sin(fr[2] * (h @ f_w3.astype(F32) + f_b3.astype(F32)))
    h = (h @ f_w4.astype(F32)).reshape(L, 2, HYENA_ORDER, D_MODEL)
    max_decay = math.log(HYENA_TARGET) / HYENA_FAST_DECAY
    min_decay = math.log(HYENA_TARGET) / HYENA_SLOW_DECAY
    deltas = jnp.abs(jnp.linspace(min_decay, max_decay, D_MODEL, dtype=F32))
    window = jnp.exp(-t_norm[:, None] * deltas[None])
    h = h * window[:, None, None, :]
    h = h * lax.rsqrt(jnp.sum(h * h, axis=(0, 1), keepdims=True) + EPS)
    return h[:, 0], h[:, 1]


def bidir_long_conv(u, hf, hb, skip):
    L = u.shape[1]
    kfull = jnp.concatenate([hf, jnp.zeros_like(hf[:1]), hb[:0:-1]], axis=0)
    K = jnp.fft.rfft(kfull, n=2 * L, axis=0)
    uf = u.astype(F32)
    U = jnp.fft.rfft(uf, n=2 * L, axis=1)
    y = jnp.fft.irfft(U * K[None], n=2 * L, axis=1)[:, :L]
    return (y + uf * skip.astype(F32)).astype(u.dtype)


def hyena_mixer(h, w_in, conv_w, conv_b, f_w1, f_b1, f_w2, f_b2, f_w3, f_b3, f_freq, f_w4,
                skip, w_out):
    L = h.shape[1]
    z = short_conv(h @ w_in, conv_w, conv_b)
    x1, x2, v = jnp.split(z, 3, axis=-1)
    hf, hb = hyena_filters(L, f_w1, f_b1, f_w2, f_b2, f_w3, f_b3, f_freq, f_w4)
    y = v
    for n, gate in enumerate((x1, x2)):
        y = gate * bidir_long_conv(y, hf[:, n], hb[:, n], skip[n])
    return y @ w_out


def sq_relu_mlp(h, w1, w2):
    return jnp.square(jax.nn.relu(h @ w1)) @ w2


def setup_inputs(seed: int = 0) -> dict:
    key = jax.random.key(seed)
    ks = list(jax.random.split(key, 40))

    def nrm(shape, scale):
        return jax.random.normal(ks.pop(), shape, F32) * scale

    def gain(shape):
        return 1.0 + nrm(shape, 0.02)

    D = D_MODEL
    return {
        'x': nrm((BATCH, SEQ, D), 1.0),
        'c': nrm((BATCH, D), 1.0),
        'ctx': nrm((BATCH, CTX_LEN, D), 1.0),
        'c_ctx': nrm((D,), 1.0),
        'w_mod': nrm((DEPTH, D, N_MOD * D), 0.5 * D ** -0.5),
        'b_mod': nrm((DEPTH, N_MOD * D), 0.02),
        'norm1_g': gain((DEPTH, D)),
        'norm2_g': gain((DEPTH, D)),
        'mlp_w1': nrm((DEPTH, D, D_FF), D ** -0.5),
        'mlp_w2': nrm((DEPTH, D_FF, D), D_FF ** -0.5),
        'a_w_in': nrm((N_EVEN, D, ATTN_IN), D ** -0.5),
        'a_q_a_g': gain((N_EVEN, MLA_Q_LORA)),
        'a_w_q_b': nrm((N_EVEN, MLA_Q_LORA, MLA_HEADS * (MLA_NOPE + MLA_ROPE)), MLA_Q_LORA ** -0.5),
        'a_kv_a_g': gain((N_EVEN, MLA_KV_LORA)),
        'a_w_kv_b': nrm((N_EVEN, MLA_KV_LORA, MLA_HEADS * (MLA_NOPE + MLA_V)), MLA_KV_LORA ** -0.5),
        'a_q_norm_g': gain((N_EVEN, GQA_DIM)),
        'a_k_norm_g': gain((N_EVEN, GQA_DIM)),
        'a_w_out': nrm((N_EVEN, MIX_WIDTH, D), MIX_WIDTH ** -0.5),
        'h_w_in': nrm((N_ODD, D, 3 * D), D ** -0.5),
        'h_conv_w': nrm((N_ODD, HYENA_SHORT, 3 * D), HYENA_SHORT ** -0.5),
        'h_conv_b': nrm((N_ODD, 3 * D), 0.02),
        'h_f_w1': nrm((N_ODD, HYENA_EMB, HYENA_FILTER_HIDDEN), HYENA_EMB ** -0.5),
        'h_f_b1': nrm((N_ODD, HYENA_FILTER_HIDDEN), 0.1),
        'h_f_w2': nrm((N_ODD, HYENA_FILTER_HIDDEN, HYENA_FILTER_HIDDEN), HYENA_FILTER_HIDDEN ** -0.5),
        'h_f_b2': nrm((N_ODD, HYENA_FILTER_HIDDEN), 0.1),
        'h_f_w3': nrm((N_ODD, HYENA_FILTER_HIDDEN, HYENA_FILTER_HIDDEN), HYENA_FILTER_HIDDEN ** -0.5),
        'h_f_b3': nrm((N_ODD, HYENA_FILTER_HIDDEN), 0.1),
        'h_f_freq': gain((N_ODD, 3, HYENA_FILTER_HIDDEN)),
        'h_f_w4': nrm((N_ODD, HYENA_FILTER_HIDDEN, 2 * HYENA_ORDER * D), HYENA_FILTER_HIDDEN ** -0.5),
        'h_skip': nrm((N_ODD, HYENA_ORDER, D), 0.3),
        'h_w_out': nrm((N_ODD, D, D), D ** -0.5),
        'final_g': gain((D,)),
    }


def reference(x, c, ctx, c_ctx, w_mod, b_mod, norm1_g, norm2_g, mlp_w1, mlp_w2,
              a_w_in, a_q_a_g, a_w_q_b, a_kv_a_g, a_w_kv_b, a_q_norm_g, a_k_norm_g, a_w_out,
              h_w_in, h_conv_w, h_conv_b, h_f_w1, h_f_b1, h_f_w2, h_f_b2, h_f_w3, h_f_b3,
              h_f_freq, h_f_w4, h_skip, h_w_out, final_g):
    S = x.shape[1]
    rows = S // GRID_W
    rr, cc = jnp.meshgrid(jnp.arange(rows, dtype=jnp.int32), jnp.arange(GRID_W, dtype=jnp.int32),
                          indexing='ij')
    row, col = rr.reshape(-1), cc.reshape(-1)
    silu_c = jax.nn.silu(c)
    silu_cc = jax.nn.silu(c_ctx)
    for i in range(DEPTH):
        ctx_live = any(j % 2 == 0 for j in range(i + 1, DEPTH))
        j = i // 2
        m_l = (silu_c @ w_mod[i] + b_mod[i])[:, None, :]
        m_c = (silu_cc @ w_mod[i] + b_mod[i])[None, None, :]
        sh1, sc1, g1, sh2, sc2, g2 = jnp.split(m_l, N_MOD, axis=-1)
        csh1, csc1, cg1, csh2, csc2, cg2 = jnp.split(m_c, N_MOD, axis=-1)
        hl = modulate(x, norm1_g[i], sh1, sc1)
        if i % 2 == 0:
            hc = modulate(ctx, norm1_g[i], csh1, csc1)
            ol, oc = attention_mixer(hl, hc, a_w_in[j], a_q_a_g[j], a_w_q_b[j], a_kv_a_g[j],
                                     a_w_kv_b[j], a_q_norm_g[j], a_k_norm_g[j], a_w_out[j],
                                     row, col, ctx_live)
        else:
            hy = (h_w_in[j], h_conv_w[j], h_conv_b[j], h_f_w1[j], h_f_b1[j], h_f_w2[j], h_f_b2[j],
                  h_f_w3[j], h_f_b3[j], h_f_freq[j], h_f_w4[j], h_skip[j], h_w_out[j])
            ol = hyena_mixer(hl, *hy)
            oc = None
            if ctx_live:
                oc = hyena_mixer(modulate(ctx, norm1_g[i], csh1, csc1), *hy)
        x = x + g1 * ol
        x = x + g2 * sq_relu_mlp(modulate(x, norm2_g[i], sh2, sc2), mlp_w1[i], mlp_w2[i])
        if ctx_live:
            ctx = ctx + cg1 * oc
            ctx = ctx + cg2 * sq_relu_mlp(modulate(ctx, norm2_g[i], csh2, csc2), mlp_w1[i], mlp_w2[i])
    return rmsnorm(x, final_g)
```

```python
import functools
import math

import jax
import jax.numpy as jnp
from jax import lax
from jax.experimental import pallas as pl
from jax.experimental.pallas import tpu as pltpu

F32 = jnp.float32
BF16 = jnp.bfloat16

D = 1024
B = 16
S = 2048
CTX = 256
T = CTX + S
GRID_W = 64
D_FF = 4 * D
N_MOD = 6
HEAD = 64
MLA_HEADS = 8
MLA_ROPE = 32
Q_LORA = 384
KV_LORA = 256
GQA_HEADS = 8
GQA_KV = 2
N_HEADS = MLA_HEADS + GQA_HEADS
N_KV = MLA_HEADS + GQA_KV
LANE = 128
ROPE_THETA = 10000.0
EPS = 1e-6
HY_BANDS = 8
HY_EMB = 1 + 2 * HY_BANDS
HY_EMB_PAD = 32
HY_HID = 64
NFFT = 2 * S

VMEM_LIMIT = 60 * 1024 * 1024

MOD_ROWS = 24
TM = 512
TP = 256
TQ = 512
TC = 256


def _cparams(sem):
    return pltpu.CompilerParams(dimension_semantics=sem, vmem_limit_bytes=VMEM_LIMIT)


def _rms_mod(x, g, scale, shift):
    ms = jnp.mean(x * x, axis=-1, keepdims=True)
    return x * lax.rsqrt(ms + EPS) * (g * (1.0 + scale)) + shift


def _mods_kernel(c_ref, w_ref, b_ref, o_ref):
    c = c_ref[...]
    s = c * (1.0 / (1.0 + jnp.exp(-c)))
    o_ref[...] = jnp.dot(s.astype(BF16), w_ref[...].astype(BF16),
                         preferred_element_type=F32) + b_ref[...]


def _mods(c_rows, w_mod, b_mod):
    depth = w_mod.shape[0]
    tn = 1024
    return pl.pallas_call(
        _mods_kernel,
        grid=(depth, N_MOD * D // tn),
        in_specs=[
            pl.BlockSpec((MOD_ROWS, D), lambda i, j: (0, 0)),
            pl.BlockSpec((None, D, tn), lambda i, j: (i, 0, j)),
            pl.BlockSpec((None, 1, tn), lambda i, j: (i, 0, j)),
        ],
        out_specs=pl.BlockSpec((None, MOD_ROWS, tn), lambda i, j: (i, 0, j)),
        out_shape=jax.ShapeDtypeStruct((depth, MOD_ROWS, N_MOD * D), F32),
        compiler_params=_cparams(("arbitrary", "arbitrary")),
        name="mods",
    )(c_rows, w_mod, b_mod.reshape(depth, 1, N_MOD * D))


def _modmm_kernel(x_ref, g_ref, sc_ref, sh_ref, w_ref, o_ref, *, n_chunk):
    h = _rms_mod(x_ref[...], g_ref[...], sc_ref[...], sh_ref[...]).astype(BF16)
    n = w_ref.shape[1]
    for j in range(n // n_chunk):
        sl = slice(j * n_chunk, (j + 1) * n_chunk)
        o_ref[:, sl] = jnp.dot(h, w_ref[:, sl], preferred_element_type=F32).astype(o_ref.dtype)


def _modmm(x, g, scale, shift, w):
    n = w.shape[1]
    return pl.pallas_call(
        functools.partial(_modmm_kernel, n_chunk=1024),
        grid=(B, S // TM),
        in_specs=[
            pl.BlockSpec((None, TM, D), lambda b, t: (b, t, 0)),
            pl.BlockSpec((1, D), lambda b, t: (0, 0)),
            pl.BlockSpec((None, 1, D), lambda b, t: (b, 0, 0)),
            pl.BlockSpec((None, 1, D), lambda b, t: (b, 0, 0)),
            pl.BlockSpec((D, n), lambda b, t: (0, 0)),
        ],
        out_specs=pl.BlockSpec((None, TM, n), lambda b, t: (b, t, 0)),
        out_shape=jax.ShapeDtypeStruct((B, S, n), BF16),
        compiler_params=_cparams(("arbitrary", "arbitrary")),
        name="modmm",
    )(x, g, scale, shift, w)


def _rope(x, cos, sin_lo, sin_hi, n):
    return x * cos + pltpu.roll(x, LANE - n, 1) * sin_lo + pltpu.roll(x, n, 1) * sin_hi


def _attn_prep_kernel(x_ref, ctx_ref, g_ref, sc_ref, sh_ref, csc_ref, csh_ref, tab_ref,
                      win_ref, wqb_ref, wk_ref, wv_ref, qag_ref, kvag_ref, qng_ref, kng_ref,
                      q_ref, k_ref, v_ref):
    t = pl.program_id(1)
    is_ctx = t == 0
    src = jnp.where(is_ctx, ctx_ref[...], x_ref[...])
    scale = jnp.where(is_ctx, csc_ref[...], sc_ref[...])
    shift = jnp.where(is_ctx, csh_ref[...], sh_ref[...])
    h = _rms_mod(src, g_ref[...], scale, shift).astype(BF16)
    z = jnp.dot(h, win_ref[...], preferred_element_type=F32)

    cos_m, slo_m, shi_m = tab_ref[0], tab_ref[1], tab_ref[2]
    cos_g, slo_g, shi_g = tab_ref[3], tab_ref[4], tab_ref[5]
    ones_col = (lax.broadcasted_iota(jnp.int32, (1, LANE), 1) == HEAD).astype(F32)

    o_kr = Q_LORA + KV_LORA
    o_gq = o_kr + LANE
    o_gk = o_gq + GQA_HEADS * LANE
    o_gv = o_gk + GQA_KV * LANE

    def rms(v, g):
        return v * lax.rsqrt(jnp.mean(v * v, axis=-1, keepdims=True) + EPS) * g

    def head_norm(v, g):
        ss = jnp.sum(v * v, axis=-1, keepdims=True)
        return v * lax.rsqrt(ss * (1.0 / HEAD) + EPS) * g

    ckv = rms(z[:, Q_LORA:o_kr], kvag_ref[...]).astype(BF16)
    kn = jnp.dot(ckv, wk_ref[...], preferred_element_type=F32)
    vm = jnp.dot(ckv, wv_ref[...], preferred_element_type=F32)
    kr = _rope(z[:, o_kr:o_gq], cos_m, slo_m, shi_m, MLA_ROPE // 4)
    for hh in range(MLA_HEADS):
        sl = slice(hh * LANE, (hh + 1) * LANE)
        k_ref[:, sl] = (kn[:, sl] + kr).astype(BF16)
        v_ref[:, sl] = (vm[:, sl] + ones_col).astype(BF16)
    for j in range(GQA_KV):
        sl = slice((MLA_HEADS + j) * LANE, (MLA_HEADS + j + 1) * LANE)
        gk = head_norm(z[:, o_gk + j * LANE:o_gk + (j + 1) * LANE], kng_ref[...])
        k_ref[:, sl] = _rope(gk, cos_g, slo_g, shi_g, HEAD // 4).astype(BF16)
        gv = z[:, o_gv + j * LANE:o_gv + (j + 1) * LANE]
        v_ref[:, sl] = (gv + ones_col).astype(BF16)

    @pl.when(t > 0)
    def _():
        cq = rms(z[:, :Q_LORA], qag_ref[...]).astype(BF16)
        q = jnp.dot(cq, wqb_ref[...], preferred_element_type=F32)
        s_mla = 1.0 / math.sqrt(HEAD + MLA_ROPE)
        s_gqa = 1.0 / math.sqrt(HEAD)
        for hh in range(MLA_HEADS):
            sl = slice(hh * LANE, (hh + 1) * LANE)
            q_ref[:, sl] = (_rope(q[:, sl], cos_m, slo_m, shi_m, MLA_ROPE // 4) * s_mla).astype(BF16)
        for hh in range(GQA_HEADS):
            gq = head_norm(z[:, o_gq + hh * LANE:o_gq + (hh + 1) * LANE], qng_ref[...])
            sl = slice((MLA_HEADS + hh) * LANE, (MLA_HEADS + hh + 1) * LANE)
            q_ref[:, sl] = (_rope(gq, cos_g, slo_g, shi_g, HEAD // 4) * s_gqa).astype(BF16)


def _attn_prep(x, ctx, g, sc, sh, csc, csh, tabs, win, wqb, wk, wv, qag, kvag, qng, kng):
    nw = win.shape[1]
    lat = lambda b, t: (b, jnp.maximum(t - 1, 0), 0)
    full2 = lambda b, t: (0, 0)
    return pl.pallas_call(
        _attn_prep_kernel,
        grid=(B, T // TP),
        in_specs=[
            pl.BlockSpec((None, TP, D), lat),
            pl.BlockSpec((None, CTX, D), lambda b, t: (b, 0, 0)),
            pl.BlockSpec((1, D), full2),
            pl.BlockSpec((None, 1, D), lambda b, t: (b, 0, 0)),
            pl.BlockSpec((None, 1, D), lambda b, t: (b, 0, 0)),
            pl.BlockSpec((1, D), full2),
            pl.BlockSpec((1, D), full2),
            pl.BlockSpec((6, TP, LANE), lambda b, t: (0, t, 0)),
            pl.BlockSpec((D, nw), full2),
            pl.BlockSpec((Q_LORA, MLA_HEADS * LANE), full2),
            pl.BlockSpec((KV_LORA, MLA_HEADS * LANE), full2),
            pl.BlockSpec((KV_LORA, MLA_HEADS * LANE), full2),
            pl.BlockSpec((1, Q_LORA), full2),
            pl.BlockSpec((1, KV_LORA), full2),
            pl.BlockSpec((1, LANE), full2),
            pl.BlockSpec((1, LANE), full2),
        ],
        out_specs=[
            pl.BlockSpec((None, TP, N_HEADS * LANE), lat),
            pl.BlockSpec((None, TP, N_KV * LANE), lambda b, t: (b, t, 0)),
            pl.BlockSpec((None, TP, N_KV * LANE), lambda b, t: (b, t, 0)),
        ],
        out_shape=[
            jax.ShapeDtypeStruct((B, S, N_HEADS * LANE), BF16),
            jax.ShapeDtypeStruct((B, T, N_KV * LANE), BF16),
            jax.ShapeDtypeStruct((B, T, N_KV * LANE), BF16),
        ],
        compiler_params=_cparams(("arbitrary", "arbitrary")),
        name="attn_prep",
    )(x, ctx, g, sc, sh, csc, csh, tabs, win, wqb, wk, wv, qag, kvag, qng, kng)


def _attn_kernel(q_ref, k_ref, v_ref, o_ref):
    s = lax.dot_general(q_ref[...], k_ref[...], (((1,), (1,)), ((), ())),
                        preferred_element_type=F32)
    m = jnp.max(s, axis=-1, keepdims=True)
    p = jnp.exp(s - m).astype(BF16)
    o = jnp.dot(p, v_ref[...], preferred_element_type=F32)
    o_ref[...] = (o / o[:, HEAD:HEAD + 1]).astype(BF16)


def _kv_head(h):
    return jnp.where(h < MLA_HEADS, h, MLA_HEADS + (h - MLA_HEADS) // (GQA_HEADS // GQA_KV))


def _attention(q, k, v):
    return pl.pallas_call(
        _attn_kernel,
        grid=(B, N_HEADS, S // TQ),
        in_specs=[
            pl.BlockSpec((None, TQ, LANE), lambda b, h, i: (b, i, h)),
            pl.BlockSpec((None, T, LANE), lambda b, h, i: (b, 0, _kv_head(h))),
            pl.BlockSpec((None, T, LANE), lambda b, h, i: (b, 0, _kv_head(h))),
        ],
        out_specs=pl.BlockSpec((None, TQ, LANE), lambda b, h, i: (b, i, h)),
        out_shape=jax.ShapeDtypeStruct((B, S, N_HEADS * LANE), BF16),
        compiler_params=_cparams(("arbitrary", "arbitrary", "arbitrary")),
        name="attention",
    )(q, k, v)


def _linres_kernel(a_ref, w_ref, res_ref, gate_ref, o_ref):
    y = jnp.dot(a_ref[...], w_ref[...], preferred_element_type=F32)
    o_ref[...] = res_ref[...] + gate_ref[...] * y


def _linres(a, w, res, gate):
    k = a.shape[-1]
    return pl.pallas_call(
        _linres_kernel,
        grid=(B, S // TM),
        in_specs=[
            pl.BlockSpec((None, TM, k), lambda b, t: (b, t, 0)),
            pl.BlockSpec((k, D), lambda b, t: (0, 0)),
            pl.BlockSpec((None, TM, D), lambda b, t: (b, t, 0)),
            pl.BlockSpec((None, 1, D), lambda b, t: (b, 0, 0)),
        ],
        out_specs=pl.BlockSpec((None, TM, D), lambda b, t: (b, t, 0)),
        out_shape=jax.ShapeDtypeStruct((B, S, D), F32),
        compiler_params=_cparams(("arbitrary", "arbitrary")),
        name="linres",
    )(a, w, res, gate)


def _mlp_kernel(x_ref, g_ref, sc_ref, sh_ref, gate_ref, w1_ref, w2_ref, fg_ref, o_ref, *,
                final, f_chunk):
    x = x_ref[...]
    h = _rms_mod(x, g_ref[...], sc_ref[...], sh_ref[...]).astype(BF16)
    acc = jnp.zeros(x.shape, F32)
    for j in range(D_FF // f_chunk):
        sl = slice(j * f_chunk, (j + 1) * f_chunk)
        a = jnp.maximum(jnp.dot(h, w1_ref[:, sl], preferred_element_type=F32), 0.0)
        acc = acc + jnp.dot((a * a).astype(BF16), w2_ref[sl, :], preferred_element_type=F32)
    y = x + gate_ref[...] * acc
    if final:
        y = y * lax.rsqrt(jnp.mean(y * y, axis=-1, keepdims=True) + EPS) * fg_ref[...]
    o_ref[...] = y


def _mlp(x, g, scale, shift, gate, w1, w2, final_g, final):
    const = lambda b, t: (0, 0)
    return pl.pallas_call(
        functools.partial(_mlp_kernel, final=final, f_chunk=1024),
        grid=(B, S // TM),
        in_specs=[
            pl.BlockSpec((None, TM, D), lambda b, t: (b, t, 0)),
            pl.BlockSpec((1, D), const),
            pl.BlockSpec((None, 1, D), lambda b, t: (b, 0, 0)),
            pl.BlockSpec((None, 1, D), lambda b, t: (b, 0, 0)),
            pl.BlockSpec((None, 1, D), lambda b, t: (b, 0, 0)),
            pl.BlockSpec((D, D_FF), const, pipeline_mode=pl.Buffered(1)),
            pl.BlockSpec((D_FF, D), const, pipeline_mode=pl.Buffered(1)),
            pl.BlockSpec((1, D), const),
        ],
        out_specs=pl.BlockSpec((None, TM, D), lambda b, t: (b, t, 0)),
        out_shape=jax.ShapeDtypeStruct((B, S, D), F32),
        compiler_params=_cparams(("arbitrary", "arbitrary")),
        name="mlp",
    )(x, g, scale, shift, gate, w1, w2, final_g)


def _hyena_filter_kernel(feat_ref, w1_ref, b1_ref, w2_ref, b2_ref, w3_ref, b3_ref, fr_ref,
                         w4f0_ref, w4f1_ref, w4b0_ref, w4b1_ref, dl_ref, co_ref, so_ref,
                         kr_ref, ki_ref):
    hp = lax.Precision.HIGHEST
    hid = jnp.sin(fr_ref[0:1, :] * (jnp.dot(feat_ref[...], w1_ref[...], precision=hp,
                                            preferred_element_type=F32) + b1_ref[...]))
    hid = jnp.sin(fr_ref[1:2, :] * (jnp.dot(hid, w2_ref[...], precision=hp,
                                            preferred_element_type=F32) + b2_ref[...]))
    hid = jnp.sin(fr_ref[2:3, :] * (jnp.dot(hid, w3_ref[...], precision=hp,
                                            preferred_element_type=F32) + b3_ref[...]))
    row = lax.broadcasted_iota(jnp.int32, (S, TC), 0)
    t_norm = row.astype(F32) / S
    window = jnp.exp(-t_norm * dl_ref[...])
    for order, (wf_ref, wb_ref) in enumerate(((w4f0_ref, w4b0_ref), (w4f1_ref, w4b1_ref))):
        hf = jnp.dot(hid, wf_ref[...], precision=hp, preferred_element_type=F32) * window
        hb = jnp.dot(hid, wb_ref[...], precision=hp, preferred_element_type=F32) * window
        ss = jnp.sum(hf * hf + hb * hb, axis=0, keepdims=True)
        nrm = lax.rsqrt(ss + EPS)
        hf = hf * nrm
        hb = jnp.where(row == 0, 0.0, hb * nrm)
        kr_ref[order] = (2.0 / NFFT) * jnp.dot(co_ref[...], (hf + hb).astype(BF16),
                                               preferred_element_type=F32)
        ki_ref[order] = (-2.0 / NFFT) * jnp.dot(so_ref[...], (hf - hb).astype(BF16),
                                                preferred_element_type=F32)


def _hyena_filters(feats, w1, b1, w2, b2, w3, b3, freq, w4, deltas, co, so):
    nc = D // TC
    const = lambda c: (0, 0)
    w4spec = lambda k: pl.BlockSpec((HY_HID, TC), lambda c, k=k: (0, k * nc + c))
    return pl.pallas_call(
        _hyena_filter_kernel,
        grid=(nc,),
        in_specs=[
            pl.BlockSpec((S, HY_EMB_PAD), const),
            pl.BlockSpec((HY_EMB_PAD, HY_HID), const),
            pl.BlockSpec((1, HY_HID), const),
            pl.BlockSpec((HY_HID, HY_HID), const),
            pl.BlockSpec((1, HY_HID), const),
            pl.BlockSpec((HY_HID, HY_HID), const),
            pl.BlockSpec((1, HY_HID), const),
            pl.BlockSpec((3, HY_HID), const),
            w4spec(0), w4spec(1), w4spec(2), w4spec(3),
            pl.BlockSpec((1, TC), lambda c: (0, c)),
            pl.BlockSpec((S, S), const, pipeline_mode=pl.Buffered(1)),
            pl.BlockSpec((S, S), const, pipeline_mode=pl.Buffered(1)),
        ],
        out_specs=[
            pl.BlockSpec((2, S, TC), lambda c: (0, 0, c)),
            pl.BlockSpec((2, S, TC), lambda c: (0, 0, c)),
        ],
        out_shape=[
            jax.ShapeDtypeStruct((2, S, D), F32),
            jax.ShapeDtypeStruct((2, S, D), F32),
        ],
        compiler_params=_cparams(("arbitrary",)),
        name="hyena_filters",
    )(feats, w1, b1, w2, b2, w3, b3, freq, w4, w4, w4, w4, deltas, co, so)


def _hyena_conv_kernel(x1_ref, x2_ref, v_ref, cw1_ref, cw2_ref, cwv_ref, cb1_ref, cb2_ref,
                       cbv_ref, kr_ref, ki_ref, skip_ref, cs_ref, ss_ref, y_ref):
    row = lax.broadcasted_iota(jnp.int32, (S, TC), 0)

    def short_conv(z_ref, w_ref, b_ref):
        z = z_ref[...].astype(F32)
        zm = jnp.where(row == 0, 0.0, pltpu.roll(z, 1, 0))
        zp = jnp.where(row == S - 1, 0.0, pltpu.roll(z, S - 1, 0))
        return b_ref[...] + zm * w_ref[0:1, :] + z * w_ref[1:2, :] + zp * w_ref[2:3, :]

    y = short_conv(v_ref, cwv_ref, cbv_ref)
    gates = ((x1_ref, cw1_ref, cb1_ref), (x2_ref, cw2_ref, cb2_ref))
    for order, (z_ref, w_ref, b_ref) in enumerate(gates):
        u = y.astype(BF16)
        a = jnp.dot(cs_ref[...], u, preferred_element_type=F32)
        b = jnp.dot(ss_ref[...], u, preferred_element_type=F32)
        kr = kr_ref[order]
        ki = ki_ref[order]
        yre = (a * kr + b * ki).astype(BF16)
        yim = (b * kr - a * ki).astype(BF16)
        conv = (jnp.dot(cs_ref[...], yre, preferred_element_type=F32)
                + jnp.dot(ss_ref[...], yim, preferred_element_type=F32))
        y = short_conv(z_ref, w_ref, b_ref) * (conv + y * skip_ref[order:order + 1, :])
    y_ref[...] = y.astype(y_ref.dtype)


def _hyena_conv(z, conv_w, conv_b, kr, ki, skip, cs, ss):
    nc = D // TC
    const = lambda c, b: (0, 0)
    zspec = lambda k: pl.BlockSpec((None, S, TC), lambda c, b, k=k: (b, 0, k * nc + c))
    wspec = lambda k: pl.BlockSpec((3, TC), lambda c, b, k=k: (0, k * nc + c))
    bspec = lambda k: pl.BlockSpec((1, TC), lambda c, b, k=k: (0, k * nc + c))
    return pl.pallas_call(
        _hyena_conv_kernel,
        grid=(nc, B),
        in_specs=[
            zspec(0), zspec(1), zspec(2),
            wspec(0), wspec(1), wspec(2),
            bspec(0), bspec(1), bspec(2),
            pl.BlockSpec((2, S, TC), lambda c, b: (0, 0, c), pipeline_mode=pl.Buffered(1)),
            pl.BlockSpec((2, S, TC), lambda c, b: (0, 0, c), pipeline_mode=pl.Buffered(1)),
            pl.BlockSpec((2, TC), lambda c, b: (0, c)),
            pl.BlockSpec((S, S), const, pipeline_mode=pl.Buffered(1)),
            pl.BlockSpec((S, S), const, pipeline_mode=pl.Buffered(1)),
        ],
        out_specs=pl.BlockSpec((None, S, TC), lambda c, b: (b, 0, c)),
        out_shape=jax.ShapeDtypeStruct((B, S, D), BF16),
        compiler_params=_cparams(("arbitrary", "arbitrary")),
        name="hyena_conv",
    )(z, z, z, conv_w, conv_w, conv_w, conv_b, conv_b, conv_b, kr, ki, skip, cs, ss)


def _pad_heads(w, heads, d, front=0):
    k = w.shape[0]
    w = w.reshape(k, heads, d)
    w = jnp.pad(w, ((0, 0), (0, 0), (front, LANE - d - front)))
    return w.reshape(k, heads * LANE)


def _rope_tables():
    pos = jnp.arange(S, dtype=jnp.int32)
    rowf = (pos // GRID_W).astype(F32)
    colf = (pos % GRID_W).astype(F32)

    def pattern(base, half):
        n = half // 2
        inv = ROPE_THETA ** (-jnp.arange(n, dtype=F32) / n)
        cos_cols, lo_cols, hi_cols = [], [], []
        for p in (rowf, colf):
            ang = p[:, None] * inv[None]
            c, s = jnp.cos(ang), jnp.sin(ang)
            zero = jnp.zeros_like(s)
            cos_cols += [c, c]
            lo_cols += [-s, zero]
            hi_cols += [zero, s]
        width = 2 * half

        def place(cols, fill):
            body = jnp.concatenate(cols, axis=1)
            return jnp.concatenate([jnp.full((S, base), fill, F32), body,
                                    jnp.full((S, LANE - base - width), fill, F32)], axis=1)
        return place(cos_cols, 1.0), place(lo_cols, 0.0), place(hi_cols, 0.0)

    tabs = jnp.stack(pattern(HEAD, MLA_ROPE // 2) + pattern(0, HEAD // 2))
    ident = jnp.stack([jnp.ones((CTX, LANE), F32), jnp.zeros((CTX, LANE), F32),
                       jnp.zeros((CTX, LANE), F32)] * 2)
    return jnp.concatenate([ident, tabs], axis=1)


def _dft_tables():
    f2 = 2 * jnp.arange(S, dtype=jnp.int32) + 1
    n1 = jnp.arange(S, dtype=jnp.int32)
    k_sym = (f2[:, None] * f2[None, :]) % (4 * NFFT)
    k_odd = (f2[:, None] * (2 * n1[None, :])) % (4 * NFFT)
    w = 2.0 * math.pi / (4 * NFFT)
    ang_sym = k_sym.astype(F32) * w
    ang_odd = k_odd.astype(F32) * w
    return (jnp.cos(ang_sym).astype(BF16), jnp.sin(ang_sym).astype(BF16),
            jnp.cos(ang_odd).astype(BF16), jnp.sin(ang_odd).astype(BF16))


def _hyena_features():
    t = jnp.arange(S, dtype=F32)
    t_norm = t / S
    w = 2.0 * math.pi * t / S
    bands = jnp.linspace(1e-4, HY_BANDS - 1, HY_BANDS, dtype=F32)
    fw = w[:, None] * bands[None]
    feats = jnp.concatenate([t_norm[:, None], jnp.cos(fw), -jnp.sin(fw)], axis=-1)
    feats = jnp.pad(feats, ((0, 0), (0, HY_EMB_PAD - HY_EMB)))
    max_decay = math.log(1e-2) / 0.3
    min_decay = math.log(1e-2) / 1.5
    deltas = jnp.abs(jnp.linspace(min_decay, max_decay, D, dtype=F32))
    return feats, deltas.reshape(1, D)


def kernel(x, c, ctx, c_ctx, w_mod, b_mod, norm1_g, norm2_g, mlp_w1, mlp_w2, a_w_in, a_q_a_g, a_w_q_b, a_kv_a_g, a_w_kv_b, a_q_norm_g, a_k_norm_g, a_w_out, h_w_in, h_conv_w, h_conv_b, h_f_w1, h_f_b1, h_f_w2, h_f_b2, h_f_w3, h_f_b3, h_f_freq, h_f_w4, h_skip, h_w_out, final_g):
    assert x.shape == (B, S, D) and ctx.shape == (B, CTX, D) and w_mod.shape[0] == 2

    c_rows = jnp.concatenate([c, c_ctx[None], jnp.zeros((MOD_ROWS - B - 1, D), F32)], axis=0)
    mods = _mods(c_rows, w_mod, b_mod)

    def lat_mod(i, k):
        return mods[i, :B, k * D:(k + 1) * D].reshape(B, 1, D)

    def ctx_mod(i, k):
        return mods[i, B:B + 1, k * D:(k + 1) * D]

    w_in = a_w_in[0]
    o_kr = Q_LORA + KV_LORA
    o_gq = o_kr + MLA_ROPE
    o_gk = o_gq + GQA_HEADS * HEAD
    o_gv = o_gk + GQA_KV * HEAD
    win = jnp.concatenate([
        w_in[:, :o_kr],
        _pad_heads(w_in[:, o_kr:o_gq], 1, MLA_ROPE, front=HEAD),
        _pad_heads(w_in[:, o_gq:o_gk], GQA_HEADS, HEAD),
        _pad_heads(w_in[:, o_gk:o_gv], GQA_KV, HEAD),
        _pad_heads(w_in[:, o_gv:], GQA_KV, HEAD),
    ], axis=1).astype(BF16)
    wqb = _pad_heads(a_w_q_b[0], MLA_HEADS, HEAD + MLA_ROPE).astype(BF16)
    wkv = a_w_kv_b[0].reshape(KV_LORA, MLA_HEADS, 2 * HEAD)
    wk = _pad_heads(wkv[:, :, :HEAD].reshape(KV_LORA, MLA_HEADS * HEAD), MLA_HEADS, HEAD).astype(BF16)
    wv = _pad_heads(wkv[:, :, HEAD:].reshape(KV_LORA, MLA_HEADS * HEAD), MLA_HEADS, HEAD).astype(BF16)
    qng = jnp.pad(a_q_norm_g[0], (0, LANE - HEAD)).reshape(1, LANE)
    kng = jnp.pad(a_k_norm_g[0], (0, LANE - HEAD)).reshape(1, LANE)
    wout = _pad_heads(a_w_out[0].T, N_HEADS, HEAD).T.astype(BF16)

    q, k, v = _attn_prep(x, ctx, norm1_g[0:1], lat_mod(0, 1), lat_mod(0, 0),
                         ctx_mod(0, 1), ctx_mod(0, 0), _rope_tables(), win, wqb, wk, wv,
                         a_q_a_g[0:1], a_kv_a_g[0:1], qng, kng)
    o = _attention(q, k, v)
    x = _linres(o, wout, x, lat_mod(0, 2))
    x = _mlp(x, norm2_g[0:1], lat_mod(0, 4), lat_mod(0, 3), lat_mod(0, 5),
             mlp_w1[0].astype(BF16), mlp_w2[0].astype(BF16), final_g.reshape(1, D), final=False)

    cs, ss, co, so = _dft_tables()
    feats, deltas = _hyena_features()
    kr, ki = _hyena_filters(
        feats, jnp.pad(h_f_w1[0], ((0, HY_EMB_PAD - HY_EMB), (0, 0))), h_f_b1[0:1],
        h_f_w2[0], h_f_b2[0:1], h_f_w3[0], h_f_b3[0:1], h_f_freq[0], h_f_w4[0], deltas, co, so)
    z = _modmm(x, norm1_g[1:2], lat_mod(1, 1), lat_mod(1, 0), h_w_in[0].astype(BF16))
    y = _hyena_conv(z, h_conv_w[0], h_conv_b[0:1], kr, ki, h_skip[0], cs, ss)
    x = _linres(y, h_w_out[0].astype(BF16), x, lat_mod(1, 2))
    x = _mlp(x, norm2_g[1:2], lat_mod(1, 4), lat_mod(1, 3), lat_mod(1, 5),
             mlp_w1[1].astype(BF16), mlp_w2[1].astype(BF16), final_g.reshape(1, D), final=True)
    return x
```

```python
import functools
import math

import jax
import jax.numpy as jnp
from jax import lax
from jax.experimental import pallas as pl
from jax.experimental.pallas import tpu as pltpu

F32 = jnp.float32
BF16 = jnp.bfloat16

D = 1024
B = 16
S = 2048
CTX = 256
T = CTX + S
GRID_W = 64
D_FF = 4 * D
N_MOD = 6
HEAD = 64
MLA_HEADS = 8
MLA_ROPE = 32
Q_LORA = 384
KV_LORA = 256
GQA_HEADS = 8
GQA_KV = 2
N_HEADS = MLA_HEADS + GQA_HEADS
LANE = 128
LOG2E = 1.4426950408889634
ROPE_THETA = 10000.0
EPS = 1e-6
HY_BANDS = 8
HY_EMB = 1 + 2 * HY_BANDS
HY_EMB_PAD = 32
HY_HID = 64
NFFT = 2 * S

VMEM_LIMIT = 60 * 1024 * 1024

MOD_ROWS = 24
TM = 512
TP = 256
TQ = 512
HEADS_PER_STEP = 8
TC = 256
DFT_ROWS = 512


def _cparams(sem):
    return pltpu.CompilerParams(dimension_semantics=sem, vmem_limit_bytes=VMEM_LIMIT)


def _rms_mod(x, g, scale, shift):
    ms = jnp.mean(x * x, axis=-1, keepdims=True)
    return x * lax.rsqrt(ms + EPS) * (g * (1.0 + scale)) + shift


def _mods_kernel(c_ref, w_ref, b_ref, o_ref):
    c = c_ref[...]
    s = c * (1.0 / (1.0 + jnp.exp(-c)))
    o_ref[...] = jnp.dot(s.astype(BF16), w_ref[...].astype(BF16),
                         preferred_element_type=F32) + b_ref[...]


def _mods(c_rows, w_mod, b_mod):
    depth = w_mod.shape[0]
    tn = 1024
    return pl.pallas_call(
        _mods_kernel,
        grid=(depth, N_MOD * D // tn),
        in_specs=[
            pl.BlockSpec((MOD_ROWS, D), lambda i, j: (0, 0)),
            pl.BlockSpec((None, D, tn), lambda i, j: (i, 0, j)),
            pl.BlockSpec((None, 1, tn), lambda i, j: (i, 0, j)),
        ],
        out_specs=pl.BlockSpec((None, MOD_ROWS, tn), lambda i, j: (i, 0, j)),
        out_shape=jax.ShapeDtypeStruct((depth, MOD_ROWS, N_MOD * D), F32),
        compiler_params=_cparams(("arbitrary", "arbitrary")),
        name="mods",
    )(c_rows, w_mod, b_mod.reshape(depth, 1, N_MOD * D))


def _modmm_kernel(x_ref, g_ref, sc_ref, sh_ref, w_ref, o_ref, *, n_chunk):
    h = _rms_mod(x_ref[...], g_ref[...], sc_ref[...], sh_ref[...]).astype(BF16)
    n = w_ref.shape[1]
    for j in range(n // n_chunk):
        sl = slice(j * n_chunk, (j + 1) * n_chunk)
        o_ref[:, sl] = jnp.dot(h, w_ref[:, sl], preferred_element_type=F32).astype(o_ref.dtype)


def _modmm(x, g, scale, shift, w):
    n = w.shape[1]
    return pl.pallas_call(
        functools.partial(_modmm_kernel, n_chunk=1024),
        grid=(B, S // TM),
        in_specs=[
            pl.BlockSpec((None, TM, D), lambda b, t: (b, t, 0)),
            pl.BlockSpec((1, D), lambda b, t: (0, 0)),
            pl.BlockSpec((None, 1, D), lambda b, t: (b, 0, 0)),
            pl.BlockSpec((None, 1, D), lambda b, t: (b, 0, 0)),
            pl.BlockSpec((D, n), lambda b, t: (0, 0)),
        ],
        out_specs=pl.BlockSpec((None, TM, n), lambda b, t: (b, t, 0)),
        out_shape=jax.ShapeDtypeStruct((B, S, n), BF16),
        compiler_params=_cparams(("arbitrary", "arbitrary")),
        name="modmm",
    )(x, g, scale, shift, w)


def _rope(x, cos, sin_lo, sin_hi, n):
    return x * cos + pltpu.roll(x, LANE - n, 1) * sin_lo + pltpu.roll(x, n, 1) * sin_hi


def _attn_prep_kernel(x_ref, ctx_ref, g_ref, sc_ref, sh_ref, csc_ref, csh_ref, tab_ref,
                      win_ref, wqb_ref, wk_ref, wv_ref, qag_ref, kvag_ref, qng_ref, kng_ref,
                      q_ref, k_ref, vt_ref):
    t = pl.program_id(1)
    is_ctx = t == 0
    src = jnp.where(is_ctx, ctx_ref[...], x_ref[...])
    scale = jnp.where(is_ctx, csc_ref[...], sc_ref[...])
    shift = jnp.where(is_ctx, csh_ref[...], sh_ref[...])
    h = _rms_mod(src, g_ref[...], scale, shift).astype(BF16)
    z = jnp.dot(h, win_ref[...], preferred_element_type=F32)

    cos_m, slo_m, shi_m = tab_ref[0], tab_ref[1], tab_ref[2]
    cos_g, slo_g, shi_g = tab_ref[3], tab_ref[4], tab_ref[5]
    ones_col = (lax.broadcasted_iota(jnp.int32, (1, LANE), 1) == HEAD).astype(F32)

    o_kr = Q_LORA + KV_LORA
    o_gq = o_kr + LANE
    o_gk = o_gq + GQA_HEADS * LANE
    o_gv = o_gk + GQA_KV * LANE

    def rms(v, g):
        return v * lax.rsqrt(jnp.mean(v * v, axis=-1, keepdims=True) + EPS) * g

    def head_norm(v, g):
        ss = jnp.sum(v * v, axis=-1, keepdims=True)
        return v * lax.rsqrt(ss * (1.0 / HEAD) + EPS) * g

    ckv = rms(z[:, Q_LORA:o_kr], kvag_ref[...]).astype(BF16)
    kn = jnp.dot(ckv, wk_ref[...], preferred_element_type=F32)
    vm = jnp.dot(ckv, wv_ref[...], preferred_element_type=F32)
    kr = _rope(z[:, o_kr:o_gq], cos_m, slo_m, shi_m, MLA_ROPE // 4)
    for hh in range(MLA_HEADS):
        sl = slice(hh * LANE, (hh + 1) * LANE)
        k_ref[:, sl] = (kn[:, sl] + kr).astype(BF16)
        vt_ref[sl, :] = (vm[:, sl] + ones_col).T.astype(BF16)
    rep = GQA_HEADS // GQA_KV
    for j in range(GQA_KV):
        gk = head_norm(z[:, o_gk + j * LANE:o_gk + (j + 1) * LANE], kng_ref[...])
        gk = _rope(gk, cos_g, slo_g, shi_g, HEAD // 4).astype(BF16)
        gvt = (z[:, o_gv + j * LANE:o_gv + (j + 1) * LANE] + ones_col).T.astype(BF16)
        for r in range(rep):
            hh = MLA_HEADS + j * rep + r
            sl = slice(hh * LANE, (hh + 1) * LANE)
            k_ref[:, sl] = gk
            vt_ref[sl, :] = gvt

    @pl.when(t > 0)
    def _():
        cq = rms(z[:, :Q_LORA], qag_ref[...]).astype(BF16)
        q = jnp.dot(cq, wqb_ref[...], preferred_element_type=F32)
        s_mla = LOG2E / math.sqrt(HEAD + MLA_ROPE)
        s_gqa = LOG2E / math.sqrt(HEAD)
        for hh in range(MLA_HEADS):
            sl = slice(hh * LANE, (hh + 1) * LANE)
            q_ref[:, sl] = (_rope(q[:, sl], cos_m, slo_m, shi_m, MLA_ROPE // 4) * s_mla).astype(BF16)
        for hh in range(GQA_HEADS):
            gq = head_norm(z[:, o_gq + hh * LANE:o_gq + (hh + 1) * LANE], qng_ref[...])
            sl = slice((MLA_HEADS + hh) * LANE, (MLA_HEADS + hh + 1) * LANE)
            q_ref[:, sl] = (_rope(gq, cos_g, slo_g, shi_g, HEAD // 4) * s_gqa).astype(BF16)


def _attn_prep(x, ctx, g, sc, sh, csc, csh, tabs, win, wqb, wk, wv, qag, kvag, qng, kng):
    nw = win.shape[1]
    lat = lambda b, t: (b, jnp.maximum(t - 1, 0), 0)
    full2 = lambda b, t: (0, 0)
    return pl.pallas_call(
        _attn_prep_kernel,
        grid=(B, T // TP),
        in_specs=[
            pl.BlockSpec((None, TP, D), lat),
            pl.BlockSpec((None, CTX, D), lambda b, t: (b, 0, 0)),
            pl.BlockSpec((1, D), full2),
            pl.BlockSpec((None, 1, D), lambda b, t: (b, 0, 0)),
            pl.BlockSpec((None, 1, D), lambda b, t: (b, 0, 0)),
            pl.BlockSpec((1, D), full2),
            pl.BlockSpec((1, D), full2),
            pl.BlockSpec((6, TP, LANE), lambda b, t: (0, t, 0)),
            pl.BlockSpec((D, nw), full2),
            pl.BlockSpec((Q_LORA, MLA_HEADS * LANE), full2),
            pl.BlockSpec((KV_LORA, MLA_HEADS * LANE), full2),
            pl.BlockSpec((KV_LORA, MLA_HEADS * LANE), full2),
            pl.BlockSpec((1, Q_LORA), full2),
            pl.BlockSpec((1, KV_LORA), full2),
            pl.BlockSpec((1, LANE), full2),
            pl.BlockSpec((1, LANE), full2),
        ],
        out_specs=[
            pl.BlockSpec((None, TP, N_HEADS * LANE), lat),
            pl.BlockSpec((None, TP, N_HEADS * LANE), lambda b, t: (b, t, 0)),
            pl.BlockSpec((None, N_HEADS * LANE, TP), lambda b, t: (b, 0, t)),
        ],
        out_shape=[
            jax.ShapeDtypeStruct((B, S, N_HEADS * LANE), BF16),
            jax.ShapeDtypeStruct((B, T, N_HEADS * LANE), BF16),
            jax.ShapeDtypeStruct((B, N_HEADS * LANE, T), BF16),
        ],
        compiler_params=_cparams(("arbitrary", "arbitrary")),
        name="attn_prep",
    )(x, ctx, g, sc, sh, csc, csh, tabs, win, wqb, wk, wv, qag, kvag, qng, kng)


def _attn_kernel(q_ref, k_ref, vt_ref, o_ref, s_buf, p_buf, m_buf):
    def scores(h):
        sl = slice(h * LANE, (h + 1) * LANE)
        st = lax.dot_general(k_ref[:, sl], q_ref[:, sl], (((1,), (1,)), ((), ())),
                             preferred_element_type=F32)
        s_buf[h % 2] = st
        m_buf[h % 2] = jnp.max(st, axis=0, keepdims=True)

    scores(0)
    for h in range(HEADS_PER_STEP):
        sl = slice(h * LANE, (h + 1) * LANE)
        if h + 1 < HEADS_PER_STEP:
            scores(h + 1)
        p_buf[h % 2] = jnp.exp2(s_buf[h % 2] - m_buf[h % 2]).astype(BF16)
        ot = jnp.dot(vt_ref[sl, :], p_buf[h % 2], preferred_element_type=F32)
        ot = ot / ot[HEAD:HEAD + 1, :]
        o_ref[:, sl] = ot.T.astype(BF16)


def _attention(q, k, vt):
    nl = HEADS_PER_STEP * LANE
    return pl.pallas_call(
        _attn_kernel,
        grid=(B, N_HEADS // HEADS_PER_STEP, S // TQ),
        in_specs=[
            pl.BlockSpec((None, TQ, nl), lambda b, g, i: (b, i, g)),
            pl.BlockSpec((None, T, nl), lambda b, g, i: (b, 0, g)),
            pl.BlockSpec((None, nl, T), lambda b, g, i: (b, g, 0)),
        ],
        out_specs=pl.BlockSpec((None, TQ, nl), lambda b, g, i: (b, i, g)),
        out_shape=jax.ShapeDtypeStruct((B, S, N_HEADS * LANE), BF16),
        scratch_shapes=[pltpu.VMEM((2, T, TQ), F32), pltpu.VMEM((2, T, TQ), BF16),
                        pltpu.VMEM((2, 1, TQ), F32)],
        compiler_params=_cparams(("arbitrary", "arbitrary", "arbitrary")),
        name="attention",
    )(q, k, vt)


def _linres_kernel(a_ref, w_ref, res_ref, gate_ref, o_ref):
    y = jnp.dot(a_ref[...], w_ref[...], preferred_element_type=F32)
    o_ref[...] = res_ref[...] + gate_ref[...] * y


def _linres(a, w, res, gate):
    k = a.shape[-1]
    return pl.pallas_call(
        _linres_kernel,
        grid=(B, S // TM),
        in_specs=[
            pl.BlockSpec((None, TM, k), lambda b, t: (b, t, 0)),
            pl.BlockSpec((k, D), lambda b, t: (0, 0)),
            pl.BlockSpec((None, TM, D), lambda b, t: (b, t, 0)),
            pl.BlockSpec((None, 1, D), lambda b, t: (b, 0, 0)),
        ],
        out_specs=pl.BlockSpec((None, TM, D), lambda b, t: (b, t, 0)),
        out_shape=jax.ShapeDtypeStruct((B, S, D), F32),
        compiler_params=_cparams(("arbitrary", "arbitrary")),
        name="linres",
    )(a, w, res, gate)


def _mlp_kernel(x_ref, g_ref, sc_ref, sh_ref, gate_ref, w1_ref, w2_ref, fg_ref, o_ref, *,
                final, f_chunk):
    x = x_ref[...]
    h = _rms_mod(x, g_ref[...], sc_ref[...], sh_ref[...]).astype(BF16)
    acc = jnp.zeros(x.shape, F32)
    for j in range(D_FF // f_chunk):
        sl = slice(j * f_chunk, (j + 1) * f_chunk)
        a = jnp.maximum(jnp.dot(h, w1_ref[:, sl], preferred_element_type=F32), 0.0)
        acc = acc + jnp.dot((a * a).astype(BF16), w2_ref[sl, :], preferred_element_type=F32)
    y = x + gate_ref[...] * acc
    if final:
        y = y * lax.rsqrt(jnp.mean(y * y, axis=-1, keepdims=True) + EPS) * fg_ref[...]
    o_ref[...] = y


def _mlp(x, g, scale, shift, gate, w1, w2, final_g, final):
    const = lambda b, t: (0, 0)
    return pl.pallas_call(
        functools.partial(_mlp_kernel, final=final, f_chunk=1024),
        grid=(B, S // TM),
        in_specs=[
            pl.BlockSpec((None, TM, D), lambda b, t: (b, t, 0)),
            pl.BlockSpec((1, D), const),
            pl.BlockSpec((None, 1, D), lambda b, t: (b, 0, 0)),
            pl.BlockSpec((None, 1, D), lambda b, t: (b, 0, 0)),
            pl.BlockSpec((None, 1, D), lambda b, t: (b, 0, 0)),
            pl.BlockSpec((D, D_FF), const, pipeline_mode=pl.Buffered(1)),
            pl.BlockSpec((D_FF, D), const, pipeline_mode=pl.Buffered(1)),
            pl.BlockSpec((1, D), const),
        ],
        out_specs=pl.BlockSpec((None, TM, D), lambda b, t: (b, t, 0)),
        out_shape=jax.ShapeDtypeStruct((B, S, D), F32),
        compiler_params=_cparams(("arbitrary", "arbitrary")),
        name="mlp",
    )(x, g, scale, shift, gate, w1, w2, final_g)


def _hyena_filter_kernel(feat_ref, w1_ref, b1_ref, w2_ref, b2_ref, w3_ref, b3_ref, fr_ref,
                         w4f0_ref, w4f1_ref, w4b0_ref, w4b1_ref, dl_ref, co_ref, so_ref,
                         kr_ref, ki_ref):
    hp = lax.Precision.HIGHEST
    hid = jnp.sin(fr_ref[0:1, :] * (jnp.dot(feat_ref[...], w1_ref[...], precision=hp,
                                            preferred_element_type=F32) + b1_ref[...]))
    hid = jnp.sin(fr_ref[1:2, :] * (jnp.dot(hid, w2_ref[...], precision=hp,
                                            preferred_element_type=F32) + b2_ref[...]))
    hid = jnp.sin(fr_ref[2:3, :] * (jnp.dot(hid, w3_ref[...], precision=hp,
                                            preferred_element_type=F32) + b3_ref[...]))
    row = lax.broadcasted_iota(jnp.int32, (S, TC), 0)
    t_norm = row.astype(F32) / S
    window = jnp.exp(-t_norm * dl_ref[...])
    for order, (wf_ref, wb_ref) in enumerate(((w4f0_ref, w4b0_ref), (w4f1_ref, w4b1_ref))):
        hf = jnp.dot(hid, wf_ref[...], precision=hp, preferred_element_type=F32) * window
        hb = jnp.dot(hid, wb_ref[...], precision=hp, preferred_element_type=F32) * window
        ss = jnp.sum(hf * hf + hb * hb, axis=0, keepdims=True)
        nrm = lax.rsqrt(ss + EPS)
        hf = hf * nrm
        hb = jnp.where(row == 0, 0.0, hb * nrm)
        hsum = (hf + hb).astype(BF16)
        hdif = (hf - hb).astype(BF16)
        for r in range(S // DFT_ROWS):
            rs = slice(r * DFT_ROWS, (r + 1) * DFT_ROWS)
            kr_ref[order, rs, :] = (2.0 / NFFT) * jnp.dot(co_ref[rs, :], hsum,
                                                          preferred_element_type=F32)
            ki_ref[order, rs, :] = (-2.0 / NFFT) * jnp.dot(so_ref[rs, :], hdif,
                                                           preferred_element_type=F32)


def _hyena_filters(feats, w1, b1, w2, b2, w3, b3, freq, w4, deltas, co, so):
    nc = D // TC
    const = lambda c: (0, 0)
    w4spec = lambda k: pl.BlockSpec((HY_HID, TC), lambda c, k=k: (0, k * nc + c))
    return pl.pallas_call(
        _hyena_filter_kernel,
        grid=(nc,),
        in_specs=[
            pl.BlockSpec((S, HY_EMB_PAD), const),
            pl.BlockSpec((HY_EMB_PAD, HY_HID), const),
            pl.BlockSpec((1, HY_HID), const),
            pl.BlockSpec((HY_HID, HY_HID), const),
            pl.BlockSpec((1, HY_HID), const),
            pl.BlockSpec((HY_HID, HY_HID), const),
            pl.BlockSpec((1, HY_HID), const),
            pl.BlockSpec((3, HY_HID), const),
            w4spec(0), w4spec(1), w4spec(2), w4spec(3),
            pl.BlockSpec((1, TC), lambda c: (0, c)),
            pl.BlockSpec((S, S), const, pipeline_mode=pl.Buffered(1)),
            pl.BlockSpec((S, S), const, pipeline_mode=pl.Buffered(1)),
        ],
        out_specs=[
            pl.BlockSpec((2, S, TC), lambda c: (0, 0, c)),
            pl.BlockSpec((2, S, TC), lambda c: (0, 0, c)),
        ],
        out_shape=[
            jax.ShapeDtypeStruct((2, S, D), F32),
            jax.ShapeDtypeStruct((2, S, D), F32),
        ],
        compiler_params=_cparams(("arbitrary",)),
        name="hyena_filters",
    )(feats, w1, b1, w2, b2, w3, b3, freq, w4, w4, w4, w4, deltas, co, so)


def _hyena_conv_kernel(x1_ref, x2_ref, v_ref, cw1_ref, cw2_ref, cwv_ref, cb1_ref, cb2_ref,
                       cbv_ref, kr_ref, ki_ref, skip_ref, cs_ref, ss_ref, y_ref,
                       y_buf, g_buf, u_buf, yre_buf, yim_buf):
    row = lax.broadcasted_iota(jnp.int32, (S, TC), 0)

    def short_conv(z_ref, w_ref, b_ref):
        z = z_ref[...].astype(F32)
        zm = jnp.where(row == 0, 0.0, pltpu.roll(z, 1, 0))
        zp = jnp.where(row == S - 1, 0.0, pltpu.roll(z, S - 1, 0))
        return b_ref[...] + zm * w_ref[0:1, :] + z * w_ref[1:2, :] + zp * w_ref[2:3, :]

    chunks = [slice(r * DFT_ROWS, (r + 1) * DFT_ROWS) for r in range(S // DFT_ROWS)]
    y_buf[...] = short_conv(v_ref, cwv_ref, cbv_ref)
    gates = ((x1_ref, cw1_ref, cb1_ref), (x2_ref, cw2_ref, cb2_ref))
    for order, (z_ref, w_ref, b_ref) in enumerate(gates):
        u_buf[...] = y_buf[...].astype(BF16)
        for rs in chunks:
            a = jnp.dot(cs_ref[rs, :], u_buf[...], preferred_element_type=F32)
            b = jnp.dot(ss_ref[rs, :], u_buf[...], preferred_element_type=F32)
            kr = kr_ref[order, rs, :]
            ki = ki_ref[order, rs, :]
            yre_buf[rs, :] = (a * kr + b * ki).astype(BF16)
            yim_buf[rs, :] = (b * kr - a * ki).astype(BF16)
        g_buf[...] = short_conv(z_ref, w_ref, b_ref)
        skip = skip_ref[order:order + 1, :]
        for rs in chunks:
            conv = (jnp.dot(cs_ref[rs, :], yre_buf[...], preferred_element_type=F32)
                    + jnp.dot(ss_ref[rs, :], yim_buf[...], preferred_element_type=F32))
            y_buf[rs, :] = g_buf[rs, :] * (conv + y_buf[rs, :] * skip)
    y_ref[...] = y_buf[...].astype(y_ref.dtype)


def _hyena_conv(z, conv_w, conv_b, kr, ki, skip, cs, ss):
    nc = D // TC
    const = lambda c, b: (0, 0)
    zspec = lambda k: pl.BlockSpec((None, S, TC), lambda c, b, k=k: (b, 0, k * nc + c))
    wspec = lambda k: pl.BlockSpec((3, TC), lambda c, b, k=k: (0, k * nc + c))
    bspec = lambda k: pl.BlockSpec((1, TC), lambda c, b, k=k: (0, k * nc + c))
    return pl.pallas_call(
        _hyena_conv_kernel,
        grid=(nc, B),
        in_specs=[
            zspec(0), zspec(1), zspec(2),
            wspec(0), wspec(1), wspec(2),
            bspec(0), bspec(1), bspec(2),
            pl.BlockSpec((2, S, TC), lambda c, b: (0, 0, c), pipeline_mode=pl.Buffered(1)),
            pl.BlockSpec((2, S, TC), lambda c, b: (0, 0, c), pipeline_mode=pl.Buffered(1)),
            pl.BlockSpec((2, TC), lambda c, b: (0, c)),
            pl.BlockSpec((S, S), const, pipeline_mode=pl.Buffered(1)),
            pl.BlockSpec((S, S), const, pipeline_mode=pl.Buffered(1)),
        ],
        out_specs=pl.BlockSpec((None, S, TC), lambda c, b: (b, 0, c)),
        out_shape=jax.ShapeDtypeStruct((B, S, D), BF16),
        scratch_shapes=[pltpu.VMEM((S, TC), F32), pltpu.VMEM((S, TC), F32),
                        pltpu.VMEM((S, TC), BF16), pltpu.VMEM((S, TC), BF16),
                        pltpu.VMEM((S, TC), BF16)],
        compiler_params=_cparams(("arbitrary", "arbitrary")),
        name="hyena_conv",
    )(z, z, z, conv_w, conv_w, conv_w, conv_b, conv_b, conv_b, kr, ki, skip, cs, ss)


def _pad_heads(w, heads, d, front=0):
    k = w.shape[0]
    w = w.reshape(k, heads, d)
    w = jnp.pad(w, ((0, 0), (0, 0), (front, LANE - d - front)))
    return w.reshape(k, heads * LANE)


def _rope_tables():
    pos = jnp.arange(S, dtype=jnp.int32)
    rowf = (pos // GRID_W).astype(F32)
    colf = (pos % GRID_W).astype(F32)

    def pattern(base, half):
        n = half // 2
        inv = ROPE_THETA ** (-jnp.arange(n, dtype=F32) / n)
        cos_cols, lo_cols, hi_cols = [], [], []
        for p in (rowf, colf):
            ang = p[:, None] * inv[None]
            c, s = jnp.cos(ang), jnp.sin(ang)
            zero = jnp.zeros_like(s)
            cos_cols += [c, c]
            lo_cols += [-s, zero]
            hi_cols += [zero, s]
        width = 2 * half

        def place(cols, fill):
            body = jnp.concatenate(cols, axis=1)
            return jnp.concatenate([jnp.full((S, base), fill, F32), body,
                                    jnp.full((S, LANE - base - width), fill, F32)], axis=1)
        return place(cos_cols, 1.0), place(lo_cols, 0.0), place(hi_cols, 0.0)

    tabs = jnp.stack(pattern(HEAD, MLA_ROPE // 2) + pattern(0, HEAD // 2))
    ident = jnp.stack([jnp.ones((CTX, LANE), F32), jnp.zeros((CTX, LANE), F32),
                       jnp.zeros((CTX, LANE), F32)] * 2)
    return jnp.concatenate([ident, tabs], axis=1)


def _dft_tables():
    f2 = 2 * jnp.arange(S, dtype=jnp.int32) + 1
    n1 = jnp.arange(S, dtype=jnp.int32)
    k_sym = (f2[:, None] * f2[None, :]) % (4 * NFFT)
    k_odd = (f2[:, None] * (2 * n1[None, :])) % (4 * NFFT)
    w = 2.0 * math.pi / (4 * NFFT)
    ang_sym = k_sym.astype(F32) * w
    ang_odd = k_odd.astype(F32) * w
    return (jnp.cos(ang_sym).astype(BF16), jnp.sin(ang_sym).astype(BF16),
            jnp.cos(ang_odd).astype(BF16), jnp.sin(ang_odd).astype(BF16))


def _hyena_features():
    t = jnp.arange(S, dtype=F32)
    t_norm = t / S
    w = 2.0 * math.pi * t / S
    bands = jnp.linspace(1e-4, HY_BANDS - 1, HY_BANDS, dtype=F32)
    fw = w[:, None] * bands[None]
    feats = jnp.concatenate([t_norm[:, None], jnp.cos(fw), -jnp.sin(fw)], axis=-1)
    feats = jnp.pad(feats, ((0, 0), (0, HY_EMB_PAD - HY_EMB)))
    max_decay = math.log(1e-2) / 0.3
    min_decay = math.log(1e-2) / 1.5
    deltas = jnp.abs(jnp.linspace(min_decay, max_decay, D, dtype=F32))
    return feats, deltas.reshape(1, D)


def kernel(x, c, ctx, c_ctx, w_mod, b_mod, norm1_g, norm2_g, mlp_w1, mlp_w2, a_w_in, a_q_a_g, a_w_q_b, a_kv_a_g, a_w_kv_b, a_q_norm_g, a_k_norm_g, a_w_out, h_w_in, h_conv_w, h_conv_b, h_f_w1, h_f_b1, h_f_w2, h_f_b2, h_f_w3, h_f_b3, h_f_freq, h_f_w4, h_skip, h_w_out, final_g):
    assert x.shape == (B, S, D) and ctx.shape == (B, CTX, D) and w_mod.shape[0] == 2

    c_rows = jnp.concatenate([c, c_ctx[None], jnp.zeros((MOD_ROWS - B - 1, D), F32)], axis=0)
    mods = _mods(c_rows, w_mod, b_mod)

    def lat_mod(i, k):
        return mods[i, :B, k * D:(k + 1) * D].reshape(B, 1, D)

    def ctx_mod(i, k):
        return mods[i, B:B + 1, k * D:(k + 1) * D]

    w_in = a_w_in[0]
    o_kr = Q_LORA + KV_LORA
    o_gq = o_kr + MLA_ROPE
    o_gk = o_gq + GQA_HEADS * HEAD
    o_gv = o_gk + GQA_KV * HEAD
    win = jnp.concatenate([
        w_in[:, :o_kr],
        _pad_heads(w_in[:, o_kr:o_gq], 1, MLA_ROPE, front=HEAD),
        _pad_heads(w_in[:, o_gq:o_gk], GQA_HEADS, HEAD),
        _pad_heads(w_in[:, o_gk:o_gv], GQA_KV, HEAD),
        _pad_heads(w_in[:, o_gv:], GQA_KV, HEAD),
    ], axis=1).astype(BF16)
    wqb = _pad_heads(a_w_q_b[0], MLA_HEADS, HEAD + MLA_ROPE).astype(BF16)
    wkv = a_w_kv_b[0].reshape(KV_LORA, MLA_HEADS, 2 * HEAD)
    wk = _pad_heads(wkv[:, :, :HEAD].reshape(KV_LORA, MLA_HEADS * HEAD), MLA_HEADS, HEAD).astype(BF16)
    wv = _pad_heads(wkv[:, :, HEAD:].reshape(KV_LORA, MLA_HEADS * HEAD), MLA_HEADS, HEAD).astype(BF16)
    qng = jnp.pad(a_q_norm_g[0], (0, LANE - HEAD)).reshape(1, LANE)
    kng = jnp.pad(a_k_norm_g[0], (0, LANE - HEAD)).reshape(1, LANE)
    wout = _pad_heads(a_w_out[0].T, N_HEADS, HEAD).T.astype(BF16)

    q, k, vt = _attn_prep(x, ctx, norm1_g[0:1], lat_mod(0, 1), lat_mod(0, 0),
                          ctx_mod(0, 1), ctx_mod(0, 0), _rope_tables(), win, wqb, wk, wv,
                          a_q_a_g[0:1], a_kv_a_g[0:1], qng, kng)
    o = _attention(q, k, vt)
    x = _linres(o, wout, x, lat_mod(0, 2))
    x = _mlp(x, norm2_g[0:1], lat_mod(0, 4), lat_mod(0, 3), lat_mod(0, 5),
             mlp_w1[0].astype(BF16), mlp_w2[0].astype(BF16), final_g.reshape(1, D), final=False)

    cs, ss, co, so = _dft_tables()
    feats, deltas = _hyena_features()
    kr, ki = _hyena_filters(
        feats, jnp.pad(h_f_w1[0], ((0, HY_EMB_PAD - HY_EMB), (0, 0))), h_f_b1[0:1],
        h_f_w2[0], h_f_b2[0:1], h_f_w3[0], h_f_b3[0:1], h_f_freq[0], h_f_w4[0], deltas, co, so)
    z = _modmm(x, norm1_g[1:2], lat_mod(1, 1), lat_mod(1, 0), h_w_in[0].astype(BF16))
    y = _hyena_conv(z, h_conv_w[0], h_conv_b[0:1], kr, ki, h_skip[0], cs, ss)
    x = _linres(y, h_w_out[0].astype(BF16), x, lat_mod(1, 2))
    x = _mlp(x, norm2_g[1:2], lat_mod(1, 4), lat_mod(1, 3), lat_mod(1, 5),
             mlp_w1[1].astype(BF16), mlp_w2[1].astype(BF16), final_g.reshape(1, D), final=True)
    return x
```

```python
import functools
import math

import jax
import jax.numpy as jnp
from jax import lax
from jax.experimental import pallas as pl
from jax.experimental.pallas import tpu as pltpu

F32 = jnp.float32
BF16 = jnp.bfloat16

D = 1024
B = 16
S = 2048
CTX = 256
T = CTX + S
GRID_W = 64
D_FF = 4 * D
N_MOD = 6
HEAD = 64
MLA_HEADS = 8
MLA_ROPE = 32
Q_LORA = 384
KV_LORA = 256
GQA_HEADS = 8
GQA_KV = 2
N_HEADS = MLA_HEADS + GQA_HEADS
LANE = 128
LOG2E = 1.4426950408889634
ROPE_THETA = 10000.0
EPS = 1e-6
HY_BANDS = 8
HY_EMB = 1 + 2 * HY_BANDS
HY_EMB_PAD = 32
HY_HID = 64
NFFT = 2 * S

VMEM_LIMIT = 60 * 1024 * 1024

MOD_ROWS = 24
TM = 512
TP = 256
TQ = 512
HEADS_PER_STEP = 8
TC = 256
DFT_ROWS = 512
TWB = 128


def _cparams(sem):
    return pltpu.CompilerParams(dimension_semantics=sem, vmem_limit_bytes=VMEM_LIMIT)


def _rms_mod(x, g, scale, shift):
    ms = jnp.mean(x * x, axis=-1, keepdims=True)
    return x * lax.rsqrt(ms + EPS) * (g * (1.0 + scale)) + shift


def _mods_kernel(c_ref, w_ref, b_ref, o_ref):
    c = c_ref[...]
    s = c * (1.0 / (1.0 + jnp.exp(-c)))
    o_ref[...] = jnp.dot(s.astype(BF16), w_ref[...].astype(BF16),
                         preferred_element_type=F32) + b_ref[...]


def _mods(c_rows, w_mod, b_mod):
    depth = w_mod.shape[0]
    tn = 1024
    return pl.pallas_call(
        _mods_kernel,
        grid=(depth, N_MOD * D // tn),
        in_specs=[
            pl.BlockSpec((MOD_ROWS, D), lambda i, j: (0, 0)),
            pl.BlockSpec((None, D, tn), lambda i, j: (i, 0, j)),
            pl.BlockSpec((None, 1, tn), lambda i, j: (i, 0, j)),
        ],
        out_specs=pl.BlockSpec((None, MOD_ROWS, tn), lambda i, j: (i, 0, j)),
        out_shape=jax.ShapeDtypeStruct((depth, MOD_ROWS, N_MOD * D), F32),
        compiler_params=_cparams(("arbitrary", "arbitrary")),
        name="mods",
    )(c_rows, w_mod, b_mod.reshape(depth, 1, N_MOD * D))


def _modmm_kernel(x_ref, g_ref, sc_ref, sh_ref, w_ref, o_ref, *, n_chunk):
    g, sc, sh = g_ref[...], sc_ref[...], sh_ref[...]
    h = jnp.concatenate([_rms_mod(x_ref[:, :D], g, sc, sh).astype(BF16),
                         _rms_mod(x_ref[:, D:], g, sc, sh).astype(BF16)], axis=0)
    n = w_ref.shape[1]
    half = x_ref.shape[0]
    for j in range(n // n_chunk):
        sl = slice(j * n_chunk, (j + 1) * n_chunk)
        y = jnp.dot(h, w_ref[:, sl], preferred_element_type=F32).astype(o_ref.dtype)
        o_ref[0, :, sl] = y[:half]
        o_ref[1, :, sl] = y[half:]


def _modmm(x_pairs, g, scale, shift, w):
    n = w.shape[1]
    half = TM // 2
    return pl.pallas_call(
        functools.partial(_modmm_kernel, n_chunk=1024),
        grid=(B, S // TM),
        in_specs=[
            pl.BlockSpec((None, half, 2 * D), lambda b, t: (b, t, 0)),
            pl.BlockSpec((1, D), lambda b, t: (0, 0)),
            pl.BlockSpec((None, 1, D), lambda b, t: (b, 0, 0)),
            pl.BlockSpec((None, 1, D), lambda b, t: (b, 0, 0)),
            pl.BlockSpec((D, n), lambda b, t: (0, 0)),
        ],
        out_specs=pl.BlockSpec((None, 2, half, n), lambda b, t: (b, 0, t, 0)),
        out_shape=jax.ShapeDtypeStruct((B, 2, S // 2, n), BF16),
        compiler_params=_cparams(("arbitrary", "arbitrary")),
        name="modmm",
    )(x_pairs, g, scale, shift, w)


def _rope(x, cos, sin_lo, sin_hi, n):
    return x * cos + pltpu.roll(x, LANE - n, 1) * sin_lo + pltpu.roll(x, n, 1) * sin_hi


def _attn_prep_kernel(x_ref, ctx_ref, g_ref, sc_ref, sh_ref, csc_ref, csh_ref, tab_ref,
                      win_ref, wqb_ref, wk_ref, wv_ref, qag_ref, kvag_ref, qng_ref, kng_ref,
                      q_ref, k_ref, vt_ref):
    t = pl.program_id(1)
    is_ctx = t == 0
    src = jnp.where(is_ctx, ctx_ref[...], x_ref[...])
    scale = jnp.where(is_ctx, csc_ref[...], sc_ref[...])
    shift = jnp.where(is_ctx, csh_ref[...], sh_ref[...])
    h = _rms_mod(src, g_ref[...], scale, shift).astype(BF16)
    z = jnp.dot(h, win_ref[...], preferred_element_type=F32)

    cos_m, slo_m, shi_m = tab_ref[0], tab_ref[1], tab_ref[2]
    cos_g, slo_g, shi_g = tab_ref[3], tab_ref[4], tab_ref[5]
    ones_col = (lax.broadcasted_iota(jnp.int32, (1, LANE), 1) == HEAD).astype(F32)

    o_kr = Q_LORA + KV_LORA
    o_gq = o_kr + LANE
    o_gk = o_gq + GQA_HEADS * LANE
    o_gv = o_gk + GQA_KV * LANE

    def rms(v, g):
        return v * lax.rsqrt(jnp.mean(v * v, axis=-1, keepdims=True) + EPS) * g

    def head_norm(v, g):
        ss = jnp.sum(v * v, axis=-1, keepdims=True)
        return v * lax.rsqrt(ss * (1.0 / HEAD) + EPS) * g

    ckv = rms(z[:, Q_LORA:o_kr], kvag_ref[...]).astype(BF16)
    kn = jnp.dot(ckv, wk_ref[...], preferred_element_type=F32)
    vm = jnp.dot(ckv, wv_ref[...], preferred_element_type=F32)
    kr = _rope(z[:, o_kr:o_gq], cos_m, slo_m, shi_m, MLA_ROPE // 4)
    for hh in range(MLA_HEADS):
        sl = slice(hh * LANE, (hh + 1) * LANE)
        k_ref[:, sl] = (kn[:, sl] + kr).astype(BF16)
        vt_ref[sl, :] = (vm[:, sl] + ones_col).T.astype(BF16)
    rep = GQA_HEADS // GQA_KV
    for j in range(GQA_KV):
        gk = head_norm(z[:, o_gk + j * LANE:o_gk + (j + 1) * LANE], kng_ref[...])
        gk = _rope(gk, cos_g, slo_g, shi_g, HEAD // 4).astype(BF16)
        gvt = (z[:, o_gv + j * LANE:o_gv + (j + 1) * LANE] + ones_col).T.astype(BF16)
        for r in range(rep):
            hh = MLA_HEADS + j * rep + r
            sl = slice(hh * LANE, (hh + 1) * LANE)
            k_ref[:, sl] = gk
            vt_ref[sl, :] = gvt

    @pl.when(t > 0)
    def _():
        cq = rms(z[:, :Q_LORA], qag_ref[...]).astype(BF16)
        q = jnp.dot(cq, wqb_ref[...], preferred_element_type=F32)
        s_mla = LOG2E / math.sqrt(HEAD + MLA_ROPE)
        s_gqa = LOG2E / math.sqrt(HEAD)
        for hh in range(MLA_HEADS):
            sl = slice(hh * LANE, (hh + 1) * LANE)
            q_ref[:, sl] = (_rope(q[:, sl], cos_m, slo_m, shi_m, MLA_ROPE // 4) * s_mla).astype(BF16)
        for hh in range(GQA_HEADS):
            gq = head_norm(z[:, o_gq + hh * LANE:o_gq + (hh + 1) * LANE], qng_ref[...])
            sl = slice((MLA_HEADS + hh) * LANE, (MLA_HEADS + hh + 1) * LANE)
            q_ref[:, sl] = (_rope(gq, cos_g, slo_g, shi_g, HEAD // 4) * s_gqa).astype(BF16)


def _attn_prep(x, ctx, g, sc, sh, csc, csh, tabs, win, wqb, wk, wv, qag, kvag, qng, kng):
    nw = win.shape[1]
    lat = lambda b, t: (b, jnp.maximum(t - 1, 0), 0)
    full2 = lambda b, t: (0, 0)
    return pl.pallas_call(
        _attn_prep_kernel,
        grid=(B, T // TP),
        in_specs=[
            pl.BlockSpec((None, TP, D), lat),
            pl.BlockSpec((None, CTX, D), lambda b, t: (b, 0, 0)),
            pl.BlockSpec((1, D), full2),
            pl.BlockSpec((None, 1, D), lambda b, t: (b, 0, 0)),
            pl.BlockSpec((None, 1, D), lambda b, t: (b, 0, 0)),
            pl.BlockSpec((1, D), full2),
            pl.BlockSpec((1, D), full2),
            pl.BlockSpec((6, TP, LANE), lambda b, t: (0, t, 0)),
            pl.BlockSpec((D, nw), full2),
            pl.BlockSpec((Q_LORA, MLA_HEADS * LANE), full2),
            pl.BlockSpec((KV_LORA, MLA_HEADS * LANE), full2),
            pl.BlockSpec((KV_LORA, MLA_HEADS * LANE), full2),
            pl.BlockSpec((1, Q_LORA), full2),
            pl.BlockSpec((1, KV_LORA), full2),
            pl.BlockSpec((1, LANE), full2),
            pl.BlockSpec((1, LANE), full2),
        ],
        out_specs=[
            pl.BlockSpec((None, TP, N_HEADS * LANE), lat),
            pl.BlockSpec((None, TP, N_HEADS * LANE), lambda b, t: (b, t, 0)),
            pl.BlockSpec((None, N_HEADS * LANE, TP), lambda b, t: (b, 0, t)),
        ],
        out_shape=[
            jax.ShapeDtypeStruct((B, S, N_HEADS * LANE), BF16),
            jax.ShapeDtypeStruct((B, T, N_HEADS * LANE), BF16),
            jax.ShapeDtypeStruct((B, N_HEADS * LANE, T), BF16),
        ],
        compiler_params=_cparams(("arbitrary", "arbitrary")),
        name="attn_prep",
    )(x, ctx, g, sc, sh, csc, csh, tabs, win, wqb, wk, wv, qag, kvag, qng, kng)


def _attn_kernel(q_ref, k_ref, vt_ref, o_ref, s_buf, p_buf, m_buf):
    def scores(h):
        sl = slice(h * LANE, (h + 1) * LANE)
        st = lax.dot_general(k_ref[:, sl], q_ref[:, sl], (((1,), (1,)), ((), ())),
                             preferred_element_type=F32)
        s_buf[h % 2] = st
        m_buf[h % 2] = jnp.max(st, axis=0, keepdims=True)

    scores(0)
    for h in range(HEADS_PER_STEP):
        sl = slice(h * LANE, (h + 1) * LANE)
        if h + 1 < HEADS_PER_STEP:
            scores(h + 1)
        p_buf[h % 2] = jnp.exp2(s_buf[h % 2] - m_buf[h % 2]).astype(BF16)
        ot = jnp.dot(vt_ref[sl, :], p_buf[h % 2], preferred_element_type=F32)
        ot = ot / ot[HEAD:HEAD + 1, :]
        o_ref[:, sl] = ot.T.astype(BF16)


def _attention(q, k, vt):
    nl = HEADS_PER_STEP * LANE
    return pl.pallas_call(
        _attn_kernel,
        grid=(B, N_HEADS // HEADS_PER_STEP, S // TQ),
        in_specs=[
            pl.BlockSpec((None, TQ, nl), lambda b, g, i: (b, i, g)),
            pl.BlockSpec((None, T, nl), lambda b, g, i: (b, 0, g)),
            pl.BlockSpec((None, nl, T), lambda b, g, i: (b, g, 0)),
        ],
        out_specs=pl.BlockSpec((None, TQ, nl), lambda b, g, i: (b, i, g)),
        out_shape=jax.ShapeDtypeStruct((B, S, N_HEADS * LANE), BF16),
        scratch_shapes=[pltpu.VMEM((2, T, TQ), F32), pltpu.VMEM((2, T, TQ), BF16),
                        pltpu.VMEM((2, 1, TQ), F32)],
        compiler_params=_cparams(("arbitrary", "arbitrary", "arbitrary")),
        name="attention",
    )(q, k, vt)


def _linres_kernel(a_ref, w_ref, res_ref, gate_ref, o_ref):
    y = jnp.dot(a_ref[...], w_ref[...], preferred_element_type=F32)
    o_ref[...] = res_ref[...] + gate_ref[...] * y


def _linres(a, w, res, gate):
    k = a.shape[-1]
    return pl.pallas_call(
        _linres_kernel,
        grid=(B, S // TM),
        in_specs=[
            pl.BlockSpec((None, TM, k), lambda b, t: (b, t, 0)),
            pl.BlockSpec((k, D), lambda b, t: (0, 0)),
            pl.BlockSpec((None, TM, D), lambda b, t: (b, t, 0)),
            pl.BlockSpec((None, 1, D), lambda b, t: (b, 0, 0)),
        ],
        out_specs=pl.BlockSpec((None, TM, D), lambda b, t: (b, t, 0)),
        out_shape=jax.ShapeDtypeStruct((B, S, D), F32),
        compiler_params=_cparams(("arbitrary", "arbitrary")),
        name="linres",
    )(a, w, res, gate)


def _linres_pairs_kernel(a_ref, w_ref, res_ref, gate_ref, o_ref):
    half = a_ref.shape[1]
    a = jnp.concatenate([a_ref[0], a_ref[1]], axis=0)
    y = jnp.dot(a, w_ref[...], preferred_element_type=F32)
    gate = gate_ref[...]
    o_ref[:, :D] = res_ref[:, :D] + gate * y[:half]
    o_ref[:, D:] = res_ref[:, D:] + gate * y[half:]


def _linres_pairs(a, w, res_pairs, gate):
    k = a.shape[-1]
    half = TM // 2
    return pl.pallas_call(
        _linres_pairs_kernel,
        grid=(B, S // TM),
        in_specs=[
            pl.BlockSpec((None, 2, half, k), lambda b, t: (b, 0, t, 0)),
            pl.BlockSpec((k, D), lambda b, t: (0, 0)),
            pl.BlockSpec((None, half, 2 * D), lambda b, t: (b, t, 0)),
            pl.BlockSpec((None, 1, D), lambda b, t: (b, 0, 0)),
        ],
        out_specs=pl.BlockSpec((None, half, 2 * D), lambda b, t: (b, t, 0)),
        out_shape=jax.ShapeDtypeStruct((B, S // 2, 2 * D), F32),
        compiler_params=_cparams(("arbitrary", "arbitrary")),
        name="linres_pairs",
    )(a, w, res_pairs, gate)


def _mlp_kernel(x_ref, g_ref, sc_ref, sh_ref, gate_ref, w1_ref, w2_ref, fg_ref, o_ref, *,
                final, f_chunk):
    x = x_ref[...]
    h = _rms_mod(x, g_ref[...], sc_ref[...], sh_ref[...]).astype(BF16)
    acc = jnp.zeros(x.shape, F32)
    for j in range(D_FF // f_chunk):
        sl = slice(j * f_chunk, (j + 1) * f_chunk)
        a = jnp.maximum(jnp.dot(h, w1_ref[:, sl], preferred_element_type=F32), 0.0)
        acc = acc + jnp.dot((a * a).astype(BF16), w2_ref[sl, :], preferred_element_type=F32)
    y = x + gate_ref[...] * acc
    if final:
        y = y * lax.rsqrt(jnp.mean(y * y, axis=-1, keepdims=True) + EPS) * fg_ref[...]
    o_ref[...] = y


def _mlp(x, g, scale, shift, gate, w1, w2, final_g, final):
    const = lambda b, t: (0, 0)
    return pl.pallas_call(
        functools.partial(_mlp_kernel, final=final, f_chunk=1024),
        grid=(B, S // TM),
        in_specs=[
            pl.BlockSpec((None, TM, D), lambda b, t: (b, t, 0)),
            pl.BlockSpec((1, D), const),
            pl.BlockSpec((None, 1, D), lambda b, t: (b, 0, 0)),
            pl.BlockSpec((None, 1, D), lambda b, t: (b, 0, 0)),
            pl.BlockSpec((None, 1, D), lambda b, t: (b, 0, 0)),
            pl.BlockSpec((D, D_FF), const, pipeline_mode=pl.Buffered(1)),
            pl.BlockSpec((D_FF, D), const, pipeline_mode=pl.Buffered(1)),
            pl.BlockSpec((1, D), const),
        ],
        out_specs=pl.BlockSpec((None, TM, D), lambda b, t: (b, t, 0)),
        out_shape=jax.ShapeDtypeStruct((B, S, D), F32),
        compiler_params=_cparams(("arbitrary", "arbitrary")),
        name="mlp",
    )(x, g, scale, shift, gate, w1, w2, final_g)


def _build_trig(cos_dst, sin_dst, tw_ref, a0, nblk, b0_of_block):
    for i in range(nblk):
        ca = tw_ref[0, a0 + i:a0 + i + 1, :]
        sa = tw_ref[1, a0 + i:a0 + i + 1, :]
        b0 = b0_of_block(i)
        cb = tw_ref[0, b0:b0 + TWB, :]
        sb = tw_ref[1, b0:b0 + TWB, :]
        rs = slice(i * TWB, (i + 1) * TWB)
        cos_dst[rs, :] = (ca * cb - sa * sb).astype(BF16)
        sin_dst[rs, :] = (sa * cb + ca * sb).astype(BF16)


def _hyena_filter_kernel(feat_ref, w1_ref, b1_ref, w2_ref, b2_ref, w3_ref, b3_ref, fr_ref,
                         w4f0_ref, w4f1_ref, w4b0_ref, w4b1_ref, dl_ref, tw_ref,
                         kr_ref, ki_ref, co_ref, so_ref):
    @pl.when(pl.program_id(0) == 0)
    def _():
        nblk = S // TWB
        _build_trig(co_ref, so_ref, tw_ref, 0, nblk,
                    lambda i: nblk + (0 if i < nblk // 2 else TWB))

    hp = lax.Precision.HIGHEST
    hid = jnp.sin(fr_ref[0:1, :] * (jnp.dot(feat_ref[...], w1_ref[...], precision=hp,
                                            preferred_element_type=F32) + b1_ref[...]))
    hid = jnp.sin(fr_ref[1:2, :] * (jnp.dot(hid, w2_ref[...], precision=hp,
                                            preferred_element_type=F32) + b2_ref[...]))
    hid = jnp.sin(fr_ref[2:3, :] * (jnp.dot(hid, w3_ref[...], precision=hp,
                                            preferred_element_type=F32) + b3_ref[...]))
    row = lax.broadcasted_iota(jnp.int32, (S, TC), 0)
    t_norm = row.astype(F32) / S
    window = jnp.exp(-t_norm * dl_ref[...])
    for order, (wf_ref, wb_ref) in enumerate(((w4f0_ref, w4b0_ref), (w4f1_ref, w4b1_ref))):
        hf = jnp.dot(hid, wf_ref[...], precision=hp, preferred_element_type=F32) * window
        hb = jnp.dot(hid, wb_ref[...], precision=hp, preferred_element_type=F32) * window
        ss = jnp.sum(hf * hf + hb * hb, axis=0, keepdims=True)
        nrm = lax.rsqrt(ss + EPS)
        hf = hf * nrm
        hb = jnp.where(row == 0, 0.0, hb * nrm)
        hsum = (hf + hb).astype(BF16)
        hdif = (hf - hb).astype(BF16)
        for r in range(S // DFT_ROWS):
            rs = slice(r * DFT_ROWS, (r + 1) * DFT_ROWS)
            kr_ref[order, rs, :] = (2.0 / NFFT) * jnp.dot(co_ref[rs, :], hsum,
                                                          preferred_element_type=F32)
            ki_ref[order, rs, :] = (-2.0 / NFFT) * jnp.dot(so_ref[rs, :], hdif,
                                                           preferred_element_type=F32)


def _hyena_filters(feats, w1, b1, w2, b2, w3, b3, freq, w4, deltas, tw):
    nc = D // TC
    const = lambda c: (0, 0)
    w4spec = lambda k: pl.BlockSpec((HY_HID, TC), lambda c, k=k: (0, k * nc + c))
    return pl.pallas_call(
        _hyena_filter_kernel,
        grid=(nc,),
        in_specs=[
            pl.BlockSpec((S, HY_EMB_PAD), const),
            pl.BlockSpec((HY_EMB_PAD, HY_HID), const),
            pl.BlockSpec((1, HY_HID), const),
            pl.BlockSpec((HY_HID, HY_HID), const),
            pl.BlockSpec((1, HY_HID), const),
            pl.BlockSpec((HY_HID, HY_HID), const),
            pl.BlockSpec((1, HY_HID), const),
            pl.BlockSpec((3, HY_HID), const),
            w4spec(0), w4spec(1), w4spec(2), w4spec(3),
            pl.BlockSpec((1, TC), lambda c: (0, c)),
            pl.BlockSpec(tw.shape, lambda c: (0, 0, 0), pipeline_mode=pl.Buffered(1)),
        ],
        out_specs=[
            pl.BlockSpec((2, S, TC), lambda c: (0, 0, c)),
            pl.BlockSpec((2, S, TC), lambda c: (0, 0, c)),
        ],
        out_shape=[
            jax.ShapeDtypeStruct((2, S, D), F32),
            jax.ShapeDtypeStruct((2, S, D), F32),
        ],
        scratch_shapes=[pltpu.VMEM((S, S), BF16), pltpu.VMEM((S, S), BF16)],
        compiler_params=_cparams(("arbitrary",)),
        name="hyena_filters",
    )(feats, w1, b1, w2, b2, w3, b3, freq, w4, w4, w4, w4, deltas, tw)


def _hyena_conv_kernel(x1_ref, x2_ref, v_ref, cw1_ref, cw2_ref, cwv_ref, cb1_ref, cb2_ref,
                       cbv_ref, kr_ref, ki_ref, skip_ref, tw_ref, y_ref,
                       cos_tab, sin_tab, y_buf, g_buf, u_buf, p_buf):
    hs = S // 2
    nblk = hs // TWB

    @pl.when((pl.program_id(0) == 0) & (pl.program_id(1) == 0))
    def _():
        for fam in range(4):
            base = fam * (nblk + TWB)
            _build_trig(cos_tab.at[fam], sin_tab.at[fam], tw_ref, base, nblk,
                        lambda i, base=base: base + nblk)

    row = lax.broadcasted_iota(jnp.int32, (hs, TC), 0)

    def short_conv(z_ref, w_ref, b_ref, dst):
        ze = z_ref[0].astype(F32)
        zo = z_ref[1].astype(F32)
        zo_prev = jnp.where(row == 0, 0.0, pltpu.roll(zo, 1, 0))
        ze_next = jnp.where(row == hs - 1, 0.0, pltpu.roll(ze, hs - 1, 0))
        w0, w1, w2, b = w_ref[0:1, :], w_ref[1:2, :], w_ref[2:3, :], b_ref[...]
        dst[0] = b + zo_prev * w0 + ze * w1 + zo * w2
        dst[1] = b + ze * w0 + zo * w1 + ze_next * w2

    chunks = [slice(r * DFT_ROWS, (r + 1) * DFT_ROWS) for r in range(hs // DFT_ROWS)]

    def mm(tab, fam, rs, rhs):
        return jnp.dot(tab[fam, rs, :], rhs, preferred_element_type=F32)

    short_conv(v_ref, cwv_ref, cbv_ref, y_buf)
    gates = ((x1_ref, cw1_ref, cb1_ref), (x2_ref, cw2_ref, cb2_ref))
    for order, (z_ref, w_ref, b_ref) in enumerate(gates):
        u_buf[...] = y_buf[...].astype(BF16)
        for rs in chunks:
            up = slice(hs + rs.start, hs + rs.stop)
            ea, eb = mm(cos_tab, 0, rs, u_buf[0]), mm(sin_tab, 0, rs, u_buf[0])
            oa, ob = mm(cos_tab, 1, rs, u_buf[1]), mm(sin_tab, 1, rs, u_buf[1])
            xre, xim = ea + oa, -(eb + ob)
            ure, uim = eb - ob, oa - ea
            kr, ki = kr_ref[order, rs, :], ki_ref[order, rs, :]
            yre, yim = xre * kr - xim * ki, xre * ki + xim * kr
            kr, ki = kr_ref[order, up, :], ki_ref[order, up, :]
            vre, vim = ure * kr - uim * ki, ure * ki + uim * kr
            p_buf[0, rs, :] = (yre - vim).astype(BF16)
            p_buf[1, rs, :] = (vre - yim).astype(BF16)
            p_buf[2, rs, :] = (yre + vim).astype(BF16)
            p_buf[3, rs, :] = (-(vre + yim)).astype(BF16)
        short_conv(z_ref, w_ref, b_ref, g_buf)
        skip = skip_ref[order:order + 1, :]
        for rs in chunks:
            ce = mm(cos_tab, 2, rs, p_buf[0]) + mm(sin_tab, 2, rs, p_buf[1])
            co = mm(cos_tab, 3, rs, p_buf[2]) + mm(sin_tab, 3, rs, p_buf[3])
            y_buf[0, rs, :] = g_buf[0, rs, :] * (ce + y_buf[0, rs, :] * skip)
            y_buf[1, rs, :] = g_buf[1, rs, :] * (co + y_buf[1, rs, :] * skip)
    y_ref[...] = y_buf[...].astype(y_ref.dtype)


def _hyena_conv(z, conv_w, conv_b, kr, ki, skip, tw):
    nc = D // TC
    hs = S // 2
    zspec = lambda k: pl.BlockSpec((None, 2, hs, TC), lambda c, b, k=k: (b, 0, 0, k * nc + c))
    wspec = lambda k: pl.BlockSpec((3, TC), lambda c, b, k=k: (0, k * nc + c))
    bspec = lambda k: pl.BlockSpec((1, TC), lambda c, b, k=k: (0, k * nc + c))
    return pl.pallas_call(
        _hyena_conv_kernel,
        grid=(nc, B),
        in_specs=[
            zspec(0), zspec(1), zspec(2),
            wspec(0), wspec(1), wspec(2),
            bspec(0), bspec(1), bspec(2),
            pl.BlockSpec((2, S, TC), lambda c, b: (0, 0, c), pipeline_mode=pl.Buffered(1)),
            pl.BlockSpec((2, S, TC), lambda c, b: (0, 0, c), pipeline_mode=pl.Buffered(1)),
            pl.BlockSpec((2, TC), lambda c, b: (0, c)),
            pl.BlockSpec(tw.shape, lambda c, b: (0, 0, 0), pipeline_mode=pl.Buffered(1)),
        ],
        out_specs=pl.BlockSpec((None, 2, hs, TC), lambda c, b: (b, 0, 0, c)),
        out_shape=jax.ShapeDtypeStruct((B, 2, hs, D), BF16),
        scratch_shapes=[pltpu.VMEM((4, hs, hs), BF16), pltpu.VMEM((4, hs, hs), BF16),
                        pltpu.VMEM((2, hs, TC), F32), pltpu.VMEM((2, hs, TC), F32),
                        pltpu.VMEM((2, hs, TC), BF16), pltpu.VMEM((4, hs, TC), BF16)],
        compiler_params=_cparams(("arbitrary", "arbitrary")),
        name="hyena_conv",
    )(z, z, z, conv_w, conv_w, conv_w, conv_b, conv_b, conv_b, kr, ki, skip, tw)


def _pad_heads(w, heads, d, front=0):
    k = w.shape[0]
    w = w.reshape(k, heads, d)
    w = jnp.pad(w, ((0, 0), (0, 0), (front, LANE - d - front)))
    return w.reshape(k, heads * LANE)


def _rope_tables():
    pos = jnp.arange(S, dtype=jnp.int32)
    rowf = (pos // GRID_W).astype(F32)
    colf = (pos % GRID_W).astype(F32)

    def pattern(base, half):
        n = half // 2
        inv = ROPE_THETA ** (-jnp.arange(n, dtype=F32) / n)
        cos_cols, lo_cols, hi_cols = [], [], []
        for p in (rowf, colf):
            ang = p[:, None] * inv[None]
            c, s = jnp.cos(ang), jnp.sin(ang)
            zero = jnp.zeros_like(s)
            cos_cols += [c, c]
            lo_cols += [-s, zero]
            hi_cols += [zero, s]
        width = 2 * half

        def place(cols, fill):
            body = jnp.concatenate(cols, axis=1)
            return jnp.concatenate([jnp.full((S, base), fill, F32), body,
                                    jnp.full((S, LANE - base - width), fill, F32)], axis=1)
        return place(cos_cols, 1.0), place(lo_cols, 0.0), place(hi_cols, 0.0)

    tabs = jnp.stack(pattern(HEAD, MLA_ROPE // 2) + pattern(0, HEAD // 2))
    ident = jnp.stack([jnp.ones((CTX, LANE), F32), jnp.zeros((CTX, LANE), F32),
                       jnp.zeros((CTX, LANE), F32)] * 2)
    return jnp.concatenate([ident, tabs], axis=1)


def _trig_factors(rows):
    ang = (rows % (4 * NFFT)).astype(F32) * (2.0 * math.pi / (4 * NFFT))
    return jnp.stack([jnp.cos(ang), jnp.sin(ang)])


def _conv_trig_factors():
    hs = S // 2
    i = jnp.arange(hs // TWB, dtype=jnp.int32)[:, None]
    j = jnp.arange(TWB, dtype=jnp.int32)[:, None]
    c = jnp.arange(hs, dtype=jnp.int32)[None, :]
    fams = []
    for odd in (1, 3):
        fams += [2 * TWB * i * (4 * c + odd), (2 * j + 1) * (4 * c + odd)]
    for odd in (1, 3):
        fams += [4 * TWB * i * (2 * c + 1), (4 * j + odd) * (2 * c + 1)]
    return _trig_factors(jnp.concatenate(fams, axis=0))


def _filter_trig_factors():
    nblk = S // TWB
    i = jnp.arange(nblk, dtype=jnp.int32)[:, None]
    j = jnp.arange(TWB, dtype=jnp.int32)[:, None]
    n = jnp.arange(S, dtype=jnp.int32)[None, :]
    blk = 2 * TWB * i * 2 * n
    blk = jnp.where(i < nblk // 2, blk, -blk)
    lo = (2 * j + 1) * 2 * n
    hi = (2 * (S + S // 2 - 1) + 1 - 2 * j) * 2 * n
    return _trig_factors(jnp.concatenate([blk, lo, hi], axis=0))


def _hyena_features():
    t = jnp.arange(S, dtype=F32)
    t_norm = t / S
    w = 2.0 * math.pi * t / S
    bands = jnp.linspace(1e-4, HY_BANDS - 1, HY_BANDS, dtype=F32)
    fw = w[:, None] * bands[None]
    feats = jnp.concatenate([t_norm[:, None], jnp.cos(fw), -jnp.sin(fw)], axis=-1)
    feats = jnp.pad(feats, ((0, 0), (0, HY_EMB_PAD - HY_EMB)))
    max_decay = math.log(1e-2) / 0.3
    min_decay = math.log(1e-2) / 1.5
    deltas = jnp.abs(jnp.linspace(min_decay, max_decay, D, dtype=F32))
    return feats, deltas.reshape(1, D)


def kernel(x, c, ctx, c_ctx, w_mod, b_mod, norm1_g, norm2_g, mlp_w1, mlp_w2, a_w_in, a_q_a_g, a_w_q_b, a_kv_a_g, a_w_kv_b, a_q_norm_g, a_k_norm_g, a_w_out, h_w_in, h_conv_w, h_conv_b, h_f_w1, h_f_b1, h_f_w2, h_f_b2, h_f_w3, h_f_b3, h_f_freq, h_f_w4, h_skip, h_w_out, final_g):
    assert x.shape == (B, S, D) and ctx.shape == (B, CTX, D) and w_mod.shape[0] == 2

    c_rows = jnp.concatenate([c, c_ctx[None], jnp.zeros((MOD_ROWS - B - 1, D), F32)], axis=0)
    mods = _mods(c_rows, w_mod, b_mod)

    def lat_mod(i, k):
        return mods[i, :B, k * D:(k + 1) * D].reshape(B, 1, D)

    def ctx_mod(i, k):
        return mods[i, B:B + 1, k * D:(k + 1) * D]

    w_in = a_w_in[0]
    o_kr = Q_LORA + KV_LORA
    o_gq = o_kr + MLA_ROPE
    o_gk = o_gq + GQA_HEADS * HEAD
    o_gv = o_gk + GQA_KV * HEAD
    win = jnp.concatenate([
        w_in[:, :o_kr],
        _pad_heads(w_in[:, o_kr:o_gq], 1, MLA_ROPE, front=HEAD),
        _pad_heads(w_in[:, o_gq:o_gk], GQA_HEADS, HEAD),
        _pad_heads(w_in[:, o_gk:o_gv], GQA_KV, HEAD),
        _pad_heads(w_in[:, o_gv:], GQA_KV, HEAD),
    ], axis=1).astype(BF16)
    wqb = _pad_heads(a_w_q_b[0], MLA_HEADS, HEAD + MLA_ROPE).astype(BF16)
    wkv = a_w_kv_b[0].reshape(KV_LORA, MLA_HEADS, 2 * HEAD)
    wk = _pad_heads(wkv[:, :, :HEAD].reshape(KV_LORA, MLA_HEADS * HEAD), MLA_HEADS, HEAD).astype(BF16)
    wv = _pad_heads(wkv[:, :, HEAD:].reshape(KV_LORA, MLA_HEADS * HEAD), MLA_HEADS, HEAD).astype(BF16)
    qng = jnp.pad(a_q_norm_g[0], (0, LANE - HEAD)).reshape(1, LANE)
    kng = jnp.pad(a_k_norm_g[0], (0, LANE - HEAD)).reshape(1, LANE)
    wout = _pad_heads(a_w_out[0].T, N_HEADS, HEAD).T.astype(BF16)

    q, k, vt = _attn_prep(x, ctx, norm1_g[0:1], lat_mod(0, 1), lat_mod(0, 0),
                          ctx_mod(0, 1), ctx_mod(0, 0), _rope_tables(), win, wqb, wk, wv,
                          a_q_a_g[0:1], a_kv_a_g[0:1], qng, kng)
    o = _attention(q, k, vt)
    x = _linres(o, wout, x, lat_mod(0, 2))
    x = _mlp(x, norm2_g[0:1], lat_mod(0, 4), lat_mod(0, 3), lat_mod(0, 5),
             mlp_w1[0].astype(BF16), mlp_w2[0].astype(BF16), final_g.reshape(1, D), final=False)

    feats, deltas = _hyena_features()
    kr, ki = _hyena_filters(
        feats, jnp.pad(h_f_w1[0], ((0, HY_EMB_PAD - HY_EMB), (0, 0))), h_f_b1[0:1],
        h_f_w2[0], h_f_b2[0:1], h_f_w3[0], h_f_b3[0:1], h_f_freq[0], h_f_w4[0], deltas,
        _filter_trig_factors())
    x_pairs = x.reshape(B, S // 2, 2 * D)
    z = _modmm(x_pairs, norm1_g[1:2], lat_mod(1, 1), lat_mod(1, 0), h_w_in[0].astype(BF16))
    y = _hyena_conv(z, h_conv_w[0], h_conv_b[0:1], kr, ki, h_skip[0], _conv_trig_factors())
    x = _linres_pairs(y, h_w_out[0].astype(BF16), x_pairs, lat_mod(1, 2)).reshape(B, S, D)
    x = _mlp(x, norm2_g[1:2], lat_mod(1, 4), lat_mod(1, 3), lat_mod(1, 5),
             mlp_w1[1].astype(BF16), mlp_w2[1].astype(BF16), final_g.reshape(1, D), final=True)
    return x
```

```python
import functools
import math

import jax
import jax.numpy as jnp
from jax import lax
from jax.experimental import pallas as pl
from jax.experimental.pallas import tpu as pltpu

F32 = jnp.float32
BF16 = jnp.bfloat16

D = 1024
B = 16
S = 2048
CTX = 256
T = CTX + S
GRID_W = 64
D_FF = 4 * D
N_MOD = 6
HEAD = 64
MLA_HEADS = 8
MLA_ROPE = 32
Q_LORA = 384
KV_LORA = 256
GQA_HEADS = 8
GQA_KV = 2
N_HEADS = MLA_HEADS + GQA_HEADS
LANE = 128
LOG2E = 1.4426950408889634
ROPE_THETA = 10000.0
EPS = 1e-6
HY_BANDS = 8
HY_EMB = 1 + 2 * HY_BANDS
HY_EMB_PAD = 32
HY_HID = 64
NFFT = 2 * S

VMEM_LIMIT = 60 * 1024 * 1024

MOD_ROWS = 24
TM = 512
TP = 256
TQ = 512
HEADS_PER_STEP = 8
TC = 256
DFT_ROWS = 512
TWB = 128


def _cparams(sem):
    return pltpu.CompilerParams(dimension_semantics=sem, vmem_limit_bytes=VMEM_LIMIT)


def _rms_mod(x, g, scale, shift):
    ms = jnp.mean(x * x, axis=-1, keepdims=True)
    return x * lax.rsqrt(ms + EPS) * (g * (1.0 + scale)) + shift


def _mods_kernel(c_ref, w_ref, b_ref, o_ref):
    c = c_ref[...]
    s = c * (1.0 / (1.0 + jnp.exp(-c)))
    o_ref[...] = jnp.dot(s.astype(BF16), w_ref[...].astype(BF16),
                         preferred_element_type=F32) + b_ref[...]


def _mods(c_rows, w_mod, b_mod):
    depth = w_mod.shape[0]
    tn = 1024
    return pl.pallas_call(
        _mods_kernel,
        grid=(depth, N_MOD * D // tn),
        in_specs=[
            pl.BlockSpec((MOD_ROWS, D), lambda i, j: (0, 0)),
            pl.BlockSpec((None, D, tn), lambda i, j: (i, 0, j)),
            pl.BlockSpec((None, 1, tn), lambda i, j: (i, 0, j)),
        ],
        out_specs=pl.BlockSpec((None, MOD_ROWS, tn), lambda i, j: (i, 0, j)),
        out_shape=jax.ShapeDtypeStruct((depth, MOD_ROWS, N_MOD * D), F32),
        compiler_params=_cparams(("arbitrary", "arbitrary")),
        name="mods",
    )(c_rows, w_mod, b_mod.reshape(depth, 1, N_MOD * D))


def _modmm_kernel(x_ref, g_ref, sc_ref, sh_ref, w_ref, o_ref, *, n_chunk):
    h = _rms_mod(x_ref[...], g_ref[...], sc_ref[...], sh_ref[...]).astype(BF16)
    n = w_ref.shape[1]
    for j in range(n // n_chunk):
        sl = slice(j * n_chunk, (j + 1) * n_chunk)
        o_ref[:, sl] = jnp.dot(h, w_ref[:, sl], preferred_element_type=F32).astype(o_ref.dtype)


def _modmm(x, g, scale, shift, w):
    n = w.shape[1]
    return pl.pallas_call(
        functools.partial(_modmm_kernel, n_chunk=1024),
        grid=(B, S // TM),
        in_specs=[
            pl.BlockSpec((None, TM, D), lambda b, t: (b, t, 0)),
            pl.BlockSpec((1, D), lambda b, t: (0, 0)),
            pl.BlockSpec((None, 1, D), lambda b, t: (b, 0, 0)),
            pl.BlockSpec((None, 1, D), lambda b, t: (b, 0, 0)),
            pl.BlockSpec((D, n), lambda b, t: (0, 0)),
        ],
        out_specs=pl.BlockSpec((None, TM, n), lambda b, t: (b, t, 0)),
        out_shape=jax.ShapeDtypeStruct((B, S, n), BF16),
        compiler_params=_cparams(("arbitrary", "arbitrary")),
        name="modmm",
    )(x, g, scale, shift, w)


def _rope_tiles(xs, cos, sin_lo, sin_hi, n):
    lo = [pltpu.roll(x, LANE - n, 1) for x in xs]
    hi = [pltpu.roll(x, n, 1) for x in xs]
    return [x * cos + a * sin_lo + b * sin_hi for x, a, b in zip(xs, lo, hi)]


def _attn_prep_kernel(x_ref, ctx_ref, g_ref, sc_ref, sh_ref, csc_ref, csh_ref, tab_ref,
                      win_ref, wqb_ref, wk_ref, wv_ref, qag_ref, kvag_ref, qng_ref, kng_ref,
                      q_ref, k_ref, vt_ref):
    t = pl.program_id(1)
    is_ctx = t == 0
    src = jnp.where(is_ctx, ctx_ref[...], x_ref[...])
    scale = jnp.where(is_ctx, csc_ref[...], sc_ref[...])
    shift = jnp.where(is_ctx, csh_ref[...], sh_ref[...])
    h = _rms_mod(src, g_ref[...], scale, shift).astype(BF16)
    z = jnp.dot(h, win_ref[...], preferred_element_type=F32)

    cos_m, slo_m, shi_m = tab_ref[0], tab_ref[1], tab_ref[2]
    cos_g, slo_g, shi_g = tab_ref[3], tab_ref[4], tab_ref[5]
    ones_col = (lax.broadcasted_iota(jnp.int32, (1, LANE), 1) == HEAD).astype(F32)

    o_kr = Q_LORA + KV_LORA
    o_gq = o_kr + LANE
    o_gk = o_gq + GQA_HEADS * LANE
    o_gv = o_gk + GQA_KV * LANE

    def rms(v, g):
        return v * lax.rsqrt(jnp.mean(v * v, axis=-1, keepdims=True) + EPS) * g

    def tile(v, off, i):
        return v[:, off + i * LANE:off + (i + 1) * LANE]

    s_mla = LOG2E / math.sqrt(HEAD + MLA_ROPE)
    s_gqa = LOG2E / math.sqrt(HEAD)

    cq = rms(z[:, :Q_LORA], qag_ref[...] * s_mla).astype(BF16)
    ckv = rms(z[:, Q_LORA:o_kr], kvag_ref[...]).astype(BF16)
    q = jnp.dot(cq, wqb_ref[...], preferred_element_type=F32)
    kn = jnp.dot(ckv, wk_ref[...], preferred_element_type=F32)
    vm = jnp.dot(ckv, wv_ref[...], preferred_element_type=F32)

    gqk = [tile(z, o_gq, i) for i in range(GQA_HEADS)] + [tile(z, o_gk, j) for j in range(GQA_KV)]
    gains = [qng_ref[...] * s_gqa] * GQA_HEADS + [kng_ref[...]] * GQA_KV
    ssq = [jnp.sum(v * v, axis=-1, keepdims=True) for v in gqk]
    gqk = [v * lax.rsqrt(s * (1.0 / HEAD) + EPS) * g for v, s, g in zip(gqk, ssq, gains)]
    gqk = _rope_tiles(gqk, cos_g, slo_g, shi_g, HEAD // 4)

    mla = [tile(q, 0, i) for i in range(MLA_HEADS)] + [z[:, o_kr:o_gq]]
    mla = _rope_tiles(mla, cos_m, slo_m, shi_m, MLA_ROPE // 4)
    kr = mla[MLA_HEADS]

    for hh in range(MLA_HEADS):
        sl = slice(hh * LANE, (hh + 1) * LANE)
        q_ref[:, sl] = mla[hh].astype(BF16)
        k_ref[:, sl] = (kn[:, sl] + kr).astype(BF16)
        vt_ref[sl, :] = (vm[:, sl] + ones_col).T.astype(BF16)
    for hh in range(GQA_HEADS):
        sl = slice((MLA_HEADS + hh) * LANE, (MLA_HEADS + hh + 1) * LANE)
        q_ref[:, sl] = gqk[hh].astype(BF16)
    rep = GQA_HEADS // GQA_KV
    for j in range(GQA_KV):
        gk = gqk[GQA_HEADS + j].astype(BF16)
        gvt = (tile(z, o_gv, j) + ones_col).T.astype(BF16)
        for r in range(rep):
            hh = MLA_HEADS + j * rep + r
            sl = slice(hh * LANE, (hh + 1) * LANE)
            k_ref[:, sl] = gk
            vt_ref[sl, :] = gvt


def _attn_prep(x, ctx, g, sc, sh, csc, csh, tabs, win, wqb, wk, wv, qag, kvag, qng, kng):
    nw = win.shape[1]
    lat = lambda b, t: (b, jnp.maximum(t - 1, 0), 0)
    full2 = lambda b, t: (0, 0)
    return pl.pallas_call(
        _attn_prep_kernel,
        grid=(B, T // TP),
        in_specs=[
            pl.BlockSpec((None, TP, D), lat),
            pl.BlockSpec((None, CTX, D), lambda b, t: (b, 0, 0)),
            pl.BlockSpec((1, D), full2),
            pl.BlockSpec((None, 1, D), lambda b, t: (b, 0, 0)),
            pl.BlockSpec((None, 1, D), lambda b, t: (b, 0, 0)),
            pl.BlockSpec((1, D), full2),
            pl.BlockSpec((1, D), full2),
            pl.BlockSpec((6, TP, LANE), lambda b, t: (0, t, 0)),
            pl.BlockSpec((D, nw), full2),
            pl.BlockSpec((Q_LORA, MLA_HEADS * LANE), full2),
            pl.BlockSpec((KV_LORA, MLA_HEADS * LANE), full2),
            pl.BlockSpec((KV_LORA, MLA_HEADS * LANE), full2),
            pl.BlockSpec((1, Q_LORA), full2),
            pl.BlockSpec((1, KV_LORA), full2),
            pl.BlockSpec((1, LANE), full2),
            pl.BlockSpec((1, LANE), full2),
        ],
        out_specs=[
            pl.BlockSpec((None, TP, N_HEADS * LANE), lat),
            pl.BlockSpec((None, TP, N_HEADS * LANE), lambda b, t: (b, t, 0)),
            pl.BlockSpec((None, N_HEADS * LANE, TP), lambda b, t: (b, 0, t)),
        ],
        out_shape=[
            jax.ShapeDtypeStruct((B, S, N_HEADS * LANE), BF16),
            jax.ShapeDtypeStruct((B, T, N_HEADS * LANE), BF16),
            jax.ShapeDtypeStruct((B, N_HEADS * LANE, T), BF16),
        ],
        compiler_params=_cparams(("arbitrary", "arbitrary")),
        name="attn_prep",
    )(x, ctx, g, sc, sh, csc, csh, tabs, win, wqb, wk, wv, qag, kvag, qng, kng)


def _attn_kernel(q_ref, k_ref, vt_ref, o_ref, s_buf, p_buf, m_buf):
    def scores(h):
        sl = slice(h * LANE, (h + 1) * LANE)
        st = lax.dot_general(k_ref[:, sl], q_ref[:, sl], (((1,), (1,)), ((), ())),
                             preferred_element_type=F32)
        s_buf[h % 2] = st
        m_buf[h % 2] = jnp.max(st, axis=0, keepdims=True)

    scores(0)
    for h in range(HEADS_PER_STEP):
        sl = slice(h * LANE, (h + 1) * LANE)
        if h + 1 < HEADS_PER_STEP:
            scores(h + 1)
        p_buf[h % 2] = jnp.exp2(s_buf[h % 2] - m_buf[h % 2]).astype(BF16)
        ot = jnp.dot(vt_ref[sl, :], p_buf[h % 2], preferred_element_type=F32)
        ot = ot[:HEAD, :] / ot[HEAD:HEAD + 1, :]
        if h % 2 == 0:
            even_head = ot
        else:
            pair = jnp.concatenate([even_head, ot], axis=0)
            o_ref[:, (h // 2) * LANE:(h // 2 + 1) * LANE] = pair.T.astype(BF16)


def _attention(q, k, vt):
    nl = HEADS_PER_STEP * LANE
    return pl.pallas_call(
        _attn_kernel,
        grid=(B, N_HEADS // HEADS_PER_STEP, S // TQ),
        in_specs=[
            pl.BlockSpec((None, TQ, nl), lambda b, g, i: (b, i, g)),
            pl.BlockSpec((None, T, nl), lambda b, g, i: (b, 0, g)),
            pl.BlockSpec((None, nl, T), lambda b, g, i: (b, g, 0)),
        ],
        out_specs=pl.BlockSpec((None, TQ, HEADS_PER_STEP * HEAD), lambda b, g, i: (b, i, g)),
        out_shape=jax.ShapeDtypeStruct((B, S, N_HEADS * HEAD), BF16),
        scratch_shapes=[pltpu.VMEM((2, T, TQ), F32), pltpu.VMEM((2, T, TQ), BF16),
                        pltpu.VMEM((2, 1, TQ), F32)],
        compiler_params=_cparams(("arbitrary", "arbitrary", "arbitrary")),
        name="attention",
    )(q, k, vt)


def _mlp_kernel(a_ref, wo_ref, g1_ref, x_ref, g_ref, sc_ref, sh_ref, gate_ref, w1_ref, w2_ref,
                fg_ref, o_ref, *, final, f_chunk):
    x = x_ref[...] + g1_ref[...] * jnp.dot(a_ref[...], wo_ref[...], preferred_element_type=F32)
    h = _rms_mod(x, g_ref[...], sc_ref[...], sh_ref[...]).astype(BF16)
    acc = jnp.zeros(x.shape, F32)
    for j in range(D_FF // f_chunk):
        sl = slice(j * f_chunk, (j + 1) * f_chunk)
        a = jnp.maximum(jnp.dot(h, w1_ref[:, sl], preferred_element_type=F32), 0.0)
        acc = acc + jnp.dot((a * a).astype(BF16), w2_ref[sl, :], preferred_element_type=F32)
    y = x + gate_ref[...] * acc
    if final:
        y = y * lax.rsqrt(jnp.mean(y * y, axis=-1, keepdims=True) + EPS) * fg_ref[...]
    o_ref[...] = y


def _mlp(a, wo, gate1, x, g, scale, shift, gate, w1, w2, final_g, final):
    const = lambda b, t: (0, 0)
    return pl.pallas_call(
        functools.partial(_mlp_kernel, final=final, f_chunk=1024),
        grid=(B, S // TM),
        in_specs=[
            pl.BlockSpec((None, TM, D), lambda b, t: (b, t, 0)),
            pl.BlockSpec((D, D), const, pipeline_mode=pl.Buffered(1)),
            pl.BlockSpec((None, 1, D), lambda b, t: (b, 0, 0)),
            pl.BlockSpec((None, TM, D), lambda b, t: (b, t, 0)),
            pl.BlockSpec((1, D), const),
            pl.BlockSpec((None, 1, D), lambda b, t: (b, 0, 0)),
            pl.BlockSpec((None, 1, D), lambda b, t: (b, 0, 0)),
            pl.BlockSpec((None, 1, D), lambda b, t: (b, 0, 0)),
            pl.BlockSpec((D, D_FF), const, pipeline_mode=pl.Buffered(1)),
            pl.BlockSpec((D_FF, D), const, pipeline_mode=pl.Buffered(1)),
            pl.BlockSpec((1, D), const),
        ],
        out_specs=pl.BlockSpec((None, TM, D), lambda b, t: (b, t, 0)),
        out_shape=jax.ShapeDtypeStruct((B, S, D), F32),
        compiler_params=_cparams(("arbitrary", "arbitrary")),
        name="mlp",
    )(a, wo, gate1, x, g, scale, shift, gate, w1, w2, final_g)


def _build_trig(cos_dst, sin_dst, tw_ref, a0, nblk, b0_of_block):
    for i in range(nblk):
        ca = tw_ref[0, a0 + i:a0 + i + 1, :]
        sa = tw_ref[1, a0 + i:a0 + i + 1, :]
        b0 = b0_of_block(i)
        cb = tw_ref[0, b0:b0 + TWB, :]
        sb = tw_ref[1, b0:b0 + TWB, :]
        rs = slice(i * TWB, (i + 1) * TWB)
        cos_dst[rs, :] = (ca * cb - sa * sb).astype(BF16)
        sin_dst[rs, :] = (sa * cb + ca * sb).astype(BF16)


def _hyena_filter_kernel(feat_ref, w1_ref, b1_ref, w2_ref, b2_ref, w3_ref, b3_ref, fr_ref,
                         w4f0_ref, w4f1_ref, w4b0_ref, w4b1_ref, dl_ref, tw_ref,
                         kr_ref, ki_ref, co_ref, so_ref):
    @pl.when(pl.program_id(0) == 0)
    def _():
        nblk = S // TWB
        _build_trig(co_ref, so_ref, tw_ref, 0, nblk,
                    lambda i: nblk + (0 if i < nblk // 2 else TWB))

    hp = lax.Precision.HIGHEST
    hid = jnp.sin(fr_ref[0:1, :] * (jnp.dot(feat_ref[...], w1_ref[...], precision=hp,
                                            preferred_element_type=F32) + b1_ref[...]))
    hid = jnp.sin(fr_ref[1:2, :] * (jnp.dot(hid, w2_ref[...], precision=hp,
                                            preferred_element_type=F32) + b2_ref[...]))
    hid = jnp.sin(fr_ref[2:3, :] * (jnp.dot(hid, w3_ref[...], precision=hp,
                                            preferred_element_type=F32) + b3_ref[...]))
    row = lax.broadcasted_iota(jnp.int32, (S, TC), 0)
    t_norm = row.astype(F32) / S
    window = jnp.exp(-t_norm * dl_ref[...])
    for order, (wf_ref, wb_ref) in enumerate(((w4f0_ref, w4b0_ref), (w4f1_ref, w4b1_ref))):
        hf = jnp.dot(hid, wf_ref[...], precision=hp, preferred_element_type=F32) * window
        hb = jnp.dot(hid, wb_ref[...], precision=hp, preferred_element_type=F32) * window
        ss = jnp.sum(hf * hf + hb * hb, axis=0, keepdims=True)
        nrm = lax.rsqrt(ss + EPS)
        hf = hf * nrm
        hb = jnp.where(row == 0, 0.0, hb * nrm)
        hsum = (hf + hb).astype(BF16)
        hdif = (hf - hb).astype(BF16)
        for r in range(S // DFT_ROWS):
            rs = slice(r * DFT_ROWS, (r + 1) * DFT_ROWS)
            kr_ref[order, rs, :] = (2.0 / NFFT) * jnp.dot(co_ref[rs, :], hsum,
                                                          preferred_element_type=F32)
            ki_ref[order, rs, :] = (-2.0 / NFFT) * jnp.dot(so_ref[rs, :], hdif,
                                                           preferred_element_type=F32)


def _hyena_filters(feats, w1, b1, w2, b2, w3, b3, freq, w4, deltas, tw):
    nc = D // TC
    const = lambda c: (0, 0)
    w4spec = lambda k: pl.BlockSpec((HY_HID, TC), lambda c, k=k: (0, k * nc + c))
    return pl.pallas_call(
        _hyena_filter_kernel,
        grid=(nc,),
        in_specs=[
            pl.BlockSpec((S, HY_EMB_PAD), const),
            pl.BlockSpec((HY_EMB_PAD, HY_HID), const),
            pl.BlockSpec((1, HY_HID), const),
            pl.BlockSpec((HY_HID, HY_HID), const),
            pl.BlockSpec((1, HY_HID), const),
            pl.BlockSpec((HY_HID, HY_HID), const),
            pl.BlockSpec((1, HY_HID), const),
            pl.BlockSpec((3, HY_HID), const),
            w4spec(0), w4spec(1), w4spec(2), w4spec(3),
            pl.BlockSpec((1, TC), lambda c: (0, c)),
            pl.BlockSpec(tw.shape, lambda c: (0, 0, 0), pipeline_mode=pl.Buffered(1)),
        ],
        out_specs=[
            pl.BlockSpec((2, S, TC), lambda c: (0, 0, c)),
            pl.BlockSpec((2, S, TC), lambda c: (0, 0, c)),
        ],
        out_shape=[
            jax.ShapeDtypeStruct((2, S, D), F32),
            jax.ShapeDtypeStruct((2, S, D), F32),
        ],
        scratch_shapes=[pltpu.VMEM((S, S), BF16), pltpu.VMEM((S, S), BF16)],
        compiler_params=_cparams(("arbitrary",)),
        name="hyena_filters",
    )(feats, w1, b1, w2, b2, w3, b3, freq, w4, w4, w4, w4, deltas, tw)


def _hyena_conv_kernel(x1_ref, x2_ref, v_ref, cw1_ref, cw2_ref, cwv_ref, cb1_ref, cb2_ref,
                       cbv_ref, kr_ref, ki_ref, skip_ref, tw_ref, y_ref,
                       cos_tab, sin_tab, y_buf, g_buf, u_buf, p_buf, nat_buf):
    hs = S // 2
    lanes = [slice(l * LANE, (l + 1) * LANE) for l in range(TC // LANE)]

    def split_even_odd(z_ref):
        for l, ls in enumerate(lanes):
            nat_buf[l] = z_ref[:, ls].astype(F32)
        even = jnp.concatenate([nat_buf[l, pl.ds(0, hs, stride=2), :] for l in range(len(lanes))], axis=1)
        odd = jnp.concatenate([nat_buf[l, pl.ds(1, hs, stride=2), :] for l in range(len(lanes))], axis=1)
        return even, odd
    nblk = hs // TWB

    @pl.when((pl.program_id(0) == 0) & (pl.program_id(1) == 0))
    def _():
        for fam in range(4):
            base = fam * (nblk + TWB)
            _build_trig(cos_tab.at[fam], sin_tab.at[fam], tw_ref, base, nblk,
                        lambda i, base=base: base + nblk)

    row = lax.broadcasted_iota(jnp.int32, (hs, TC), 0)

    def short_conv(z_ref, w_ref, b_ref, dst):
        ze, zo = split_even_odd(z_ref)
        zo_prev = jnp.where(row == 0, 0.0, pltpu.roll(zo, 1, 0))
        ze_next = jnp.where(row == hs - 1, 0.0, pltpu.roll(ze, hs - 1, 0))
        w0, w1, w2, b = w_ref[0:1, :], w_ref[1:2, :], w_ref[2:3, :], b_ref[...]
        dst[0] = b + zo_prev * w0 + ze * w1 + zo * w2
        dst[1] = b + ze * w0 + zo * w1 + ze_next * w2

    chunks = [slice(r * DFT_ROWS, (r + 1) * DFT_ROWS) for r in range(hs // DFT_ROWS)]

    def mm(tab, fam, rs, rhs):
        return jnp.dot(tab[fam, rs, :], rhs, preferred_element_type=F32)

    short_conv(v_ref, cwv_ref, cbv_ref, y_buf)
    gates = ((x1_ref, cw1_ref, cb1_ref), (x2_ref, cw2_ref, cb2_ref))
    for order, (z_ref, w_ref, b_ref) in enumerate(gates):
        u_buf[...] = y_buf[...].astype(BF16)
        for rs in chunks:
            up = slice(hs + rs.start, hs + rs.stop)
            ea, eb = mm(cos_tab, 0, rs, u_buf[0]), mm(sin_tab, 0, rs, u_buf[0])
            oa, ob = mm(cos_tab, 1, rs, u_buf[1]), mm(sin_tab, 1, rs, u_buf[1])
            xre, xim = ea + oa, -(eb + ob)
            ure, uim = eb - ob, oa - ea
            kr, ki = kr_ref[order, rs, :], ki_ref[order, rs, :]
            yre, yim = xre * kr - xim * ki, xre * ki + xim * kr
            kr, ki = kr_ref[order, up, :], ki_ref[order, up, :]
            vre, vim = ure * kr - uim * ki, ure * ki + uim * kr
            p_buf[0, rs, :] = (yre - vim).astype(BF16)
            p_buf[1, rs, :] = (vre - yim).astype(BF16)
            p_buf[2, rs, :] = (yre + vim).astype(BF16)
            p_buf[3, rs, :] = (-(vre + yim)).astype(BF16)
        short_conv(z_ref, w_ref, b_ref, g_buf)
        skip = skip_ref[order:order + 1, :]
        for rs in chunks:
            ce = mm(cos_tab, 2, rs, p_buf[0]) + mm(sin_tab, 2, rs, p_buf[1])
            co = mm(cos_tab, 3, rs, p_buf[2]) + mm(sin_tab, 3, rs, p_buf[3])
            y_buf[0, rs, :] = g_buf[0, rs, :] * (ce + y_buf[0, rs, :] * skip)
            y_buf[1, rs, :] = g_buf[1, rs, :] * (co + y_buf[1, rs, :] * skip)
    for l, ls in enumerate(lanes):
        nat_buf[l, pl.ds(0, hs, stride=2), :] = y_buf[0, :, ls]
        nat_buf[l, pl.ds(1, hs, stride=2), :] = y_buf[1, :, ls]
    for l, ls in enumerate(lanes):
        y_ref[:, ls] = nat_buf[l].astype(y_ref.dtype)


def _hyena_conv(z, conv_w, conv_b, kr, ki, skip, tw):
    nc = D // TC
    hs = S // 2
    zspec = lambda k: pl.BlockSpec((None, S, TC), lambda c, b, k=k: (b, 0, k * nc + c))
    wspec = lambda k: pl.BlockSpec((3, TC), lambda c, b, k=k: (0, k * nc + c))
    bspec = lambda k: pl.BlockSpec((1, TC), lambda c, b, k=k: (0, k * nc + c))
    return pl.pallas_call(
        _hyena_conv_kernel,
        grid=(nc, B),
        in_specs=[
            zspec(0), zspec(1), zspec(2),
            wspec(0), wspec(1), wspec(2),
            bspec(0), bspec(1), bspec(2),
            pl.BlockSpec((2, S, TC), lambda c, b: (0, 0, c), pipeline_mode=pl.Buffered(1)),
            pl.BlockSpec((2, S, TC), lambda c, b: (0, 0, c), pipeline_mode=pl.Buffered(1)),
            pl.BlockSpec((2, TC), lambda c, b: (0, c)),
            pl.BlockSpec(tw.shape, lambda c, b: (0, 0, 0), pipeline_mode=pl.Buffered(1)),
        ],
        out_specs=pl.BlockSpec((None, S, TC), lambda c, b: (b, 0, c)),
        out_shape=jax.ShapeDtypeStruct((B, S, D), BF16),
        scratch_shapes=[pltpu.VMEM((4, hs, hs), BF16), pltpu.VMEM((4, hs, hs), BF16),
                        pltpu.VMEM((2, hs, TC), F32), pltpu.VMEM((2, hs, TC), F32),
                        pltpu.VMEM((2, hs, TC), BF16), pltpu.VMEM((4, hs, TC), BF16),
                        pltpu.VMEM((TC // LANE, S, LANE), F32)],
        compiler_params=_cparams(("arbitrary", "arbitrary")),
        name="hyena_conv",
    )(z, z, z, conv_w, conv_w, conv_w, conv_b, conv_b, conv_b, kr, ki, skip, tw)


def _pad_heads(w, heads, d, front=0):
    k = w.shape[0]
    w = w.reshape(k, heads, d)
    w = jnp.pad(w, ((0, 0), (0, 0), (front, LANE - d - front)))
    return w.reshape(k, heads * LANE)


def _rope_tables():
    pos = jnp.arange(S, dtype=jnp.int32)
    rowf = (pos // GRID_W).astype(F32)
    colf = (pos % GRID_W).astype(F32)

    def pattern(base, half):
        n = half // 2
        inv = ROPE_THETA ** (-jnp.arange(n, dtype=F32) / n)
        cos_cols, lo_cols, hi_cols = [], [], []
        for p in (rowf, colf):
            ang = p[:, None] * inv[None]
            c, s = jnp.cos(ang), jnp.sin(ang)
            zero = jnp.zeros_like(s)
            cos_cols += [c, c]
            lo_cols += [-s, zero]
            hi_cols += [zero, s]
        width = 2 * half

        def place(cols, fill):
            body = jnp.concatenate(cols, axis=1)
            return jnp.concatenate([jnp.full((S, base), fill, F32), body,
                                    jnp.full((S, LANE - base - width), fill, F32)], axis=1)
        return place(cos_cols, 1.0), place(lo_cols, 0.0), place(hi_cols, 0.0)

    tabs = jnp.stack(pattern(HEAD, MLA_ROPE // 2) + pattern(0, HEAD // 2))
    ident = jnp.stack([jnp.ones((CTX, LANE), F32), jnp.zeros((CTX, LANE), F32),
                       jnp.zeros((CTX, LANE), F32)] * 2)
    return jnp.concatenate([ident, tabs], axis=1)


def _trig_factors(rows):
    ang = (rows % (4 * NFFT)).astype(F32) * (2.0 * math.pi / (4 * NFFT))
    return jnp.stack([jnp.cos(ang), jnp.sin(ang)])


def _conv_trig_factors():
    hs = S // 2
    i = jnp.arange(hs // TWB, dtype=jnp.int32)[:, None]
    j = jnp.arange(TWB, dtype=jnp.int32)[:, None]
    c = jnp.arange(hs, dtype=jnp.int32)[None, :]
    fams = []
    for odd in (1, 3):
        fams += [2 * TWB * i * (4 * c + odd), (2 * j + 1) * (4 * c + odd)]
    for odd in (1, 3):
        fams += [4 * TWB * i * (2 * c + 1), (4 * j + odd) * (2 * c + 1)]
    return _trig_factors(jnp.concatenate(fams, axis=0))


def _filter_trig_factors():
    nblk = S // TWB
    i = jnp.arange(nblk, dtype=jnp.int32)[:, None]
    j = jnp.arange(TWB, dtype=jnp.int32)[:, None]
    n = jnp.arange(S, dtype=jnp.int32)[None, :]
    blk = 2 * TWB * i * 2 * n
    blk = jnp.where(i < nblk // 2, blk, -blk)
    lo = (2 * j + 1) * 2 * n
    hi = (2 * (S + S // 2 - 1) + 1 - 2 * j) * 2 * n
    return _trig_factors(jnp.concatenate([blk, lo, hi], axis=0))


def _hyena_features():
    t = jnp.arange(S, dtype=F32)
    t_norm = t / S
    w = 2.0 * math.pi * t / S
    bands = jnp.linspace(1e-4, HY_BANDS - 1, HY_BANDS, dtype=F32)
    fw = w[:, None] * bands[None]
    feats = jnp.concatenate([t_norm[:, None], jnp.cos(fw), -jnp.sin(fw)], axis=-1)
    feats = jnp.pad(feats, ((0, 0), (0, HY_EMB_PAD - HY_EMB)))
    max_decay = math.log(1e-2) / 0.3
    min_decay = math.log(1e-2) / 1.5
    deltas = jnp.abs(jnp.linspace(min_decay, max_decay, D, dtype=F32))
    return feats, deltas.reshape(1, D)


def kernel(x, c, ctx, c_ctx, w_mod, b_mod, norm1_g, norm2_g, mlp_w1, mlp_w2, a_w_in, a_q_a_g, a_w_q_b, a_kv_a_g, a_w_kv_b, a_q_norm_g, a_k_norm_g, a_w_out, h_w_in, h_conv_w, h_conv_b, h_f_w1, h_f_b1, h_f_w2, h_f_b2, h_f_w3, h_f_b3, h_f_freq, h_f_w4, h_skip, h_w_out, final_g):
    assert x.shape == (B, S, D) and ctx.shape == (B, CTX, D) and w_mod.shape[0] == 2

    c_rows = jnp.concatenate([c, c_ctx[None], jnp.zeros((MOD_ROWS - B - 1, D), F32)], axis=0)
    mods = _mods(c_rows, w_mod, b_mod)

    def lat_mod(i, k):
        return mods[i, :B, k * D:(k + 1) * D].reshape(B, 1, D)

    def ctx_mod(i, k):
        return mods[i, B:B + 1, k * D:(k + 1) * D]

    w_in = a_w_in[0]
    o_kr = Q_LORA + KV_LORA
    o_gq = o_kr + MLA_ROPE
    o_gk = o_gq + GQA_HEADS * HEAD
    o_gv = o_gk + GQA_KV * HEAD
    win = jnp.concatenate([
        w_in[:, :o_kr],
        _pad_heads(w_in[:, o_kr:o_gq], 1, MLA_ROPE, front=HEAD),
        _pad_heads(w_in[:, o_gq:o_gk], GQA_HEADS, HEAD),
        _pad_heads(w_in[:, o_gk:o_gv], GQA_KV, HEAD),
        _pad_heads(w_in[:, o_gv:], GQA_KV, HEAD),
    ], axis=1).astype(BF16)
    wqb = _pad_heads(a_w_q_b[0], MLA_HEADS, HEAD + MLA_ROPE).astype(BF16)
    wkv = a_w_kv_b[0].reshape(KV_LORA, MLA_HEADS, 2 * HEAD)
    wk = _pad_heads(wkv[:, :, :HEAD].reshape(KV_LORA, MLA_HEADS * HEAD), MLA_HEADS, HEAD).astype(BF16)
    wv = _pad_heads(wkv[:, :, HEAD:].reshape(KV_LORA, MLA_HEADS * HEAD), MLA_HEADS, HEAD).astype(BF16)
    qng = jnp.pad(a_q_norm_g[0], (0, LANE - HEAD)).reshape(1, LANE)
    kng = jnp.pad(a_k_norm_g[0], (0, LANE - HEAD)).reshape(1, LANE)

    q, k, vt = _attn_prep(x, ctx, norm1_g[0:1], lat_mod(0, 1), lat_mod(0, 0),
                          ctx_mod(0, 1), ctx_mod(0, 0), _rope_tables(), win, wqb, wk, wv,
                          a_q_a_g[0:1], a_kv_a_g[0:1], qng, kng)
    o = _attention(q, k, vt)
    x = _mlp(o, a_w_out[0].astype(BF16), lat_mod(0, 2), x,
             norm2_g[0:1], lat_mod(0, 4), lat_mod(0, 3), lat_mod(0, 5),
             mlp_w1[0].astype(BF16), mlp_w2[0].astype(BF16), final_g.reshape(1, D), final=False)

    feats, deltas = _hyena_features()
    kr, ki = _hyena_filters(
        feats, jnp.pad(h_f_w1[0], ((0, HY_EMB_PAD - HY_EMB), (0, 0))), h_f_b1[0:1],
        h_f_w2[0], h_f_b2[0:1], h_f_w3[0], h_f_b3[0:1], h_f_freq[0], h_f_w4[0], deltas,
        _filter_trig_factors())
    z = _modmm(x, norm1_g[1:2], lat_mod(1, 1), lat_mod(1, 0), h_w_in[0].astype(BF16))
    y = _hyena_conv(z, h_conv_w[0], h_conv_b[0:1], kr, ki, h_skip[0], _conv_trig_factors())
    x = _mlp(y, h_w_out[0].astype(BF16), lat_mod(1, 2), x,
             norm2_g[1:2], lat_mod(1, 4), lat_mod(1, 3), lat_mod(1, 5),
             mlp_w1[1].astype(BF16), mlp_w2[1].astype(BF16), final_g.reshape(1, D), final=True)
    return x
```

```python
import functools
import math

import jax
import jax.numpy as jnp
from jax import lax
from jax.experimental import pallas as pl
from jax.experimental.pallas import tpu as pltpu

F32 = jnp.float32
BF16 = jnp.bfloat16

D = 1024
B = 16
S = 2048
CTX = 256
T = CTX + S
GRID_W = 64
D_FF = 4 * D
N_MOD = 6
HEAD = 64
MLA_HEADS = 8
MLA_ROPE = 32
Q_LORA = 384
KV_LORA = 256
GQA_HEADS = 8
GQA_KV = 2
N_HEADS = MLA_HEADS + GQA_HEADS
LANE = 128
LOG2E = 1.4426950408889634
ROPE_THETA = 10000.0
EPS = 1e-6
HY_BANDS = 8
HY_EMB = 1 + 2 * HY_BANDS
HY_EMB_PAD = 32
HY_HID = 64
NFFT = 2 * S

VMEM_LIMIT = 60 * 1024 * 1024

MOD_ROWS = 24
TM = 512
TP = 256
TQ = 512
HEADS_PER_STEP = 8
TC = 256
DFT_ROWS = 512
TWB = 128


def _cparams(sem):
    return pltpu.CompilerParams(dimension_semantics=sem, vmem_limit_bytes=VMEM_LIMIT)


def _rms_mod(x, g, scale, shift):
    ms = jnp.mean(x * x, axis=-1, keepdims=True)
    return x * lax.rsqrt(ms + EPS) * (g * (1.0 + scale)) + shift


def _mods_kernel(c_ref, w_ref, b_ref, o_ref):
    c = c_ref[...]
    s = c * (1.0 / (1.0 + jnp.exp(-c)))
    o_ref[...] = jnp.dot(s.astype(BF16), w_ref[...].astype(BF16),
                         preferred_element_type=F32) + b_ref[...]


def _mods(c_rows, w_mod, b_mod):
    depth = w_mod.shape[0]
    tn = 1024
    return pl.pallas_call(
        _mods_kernel,
        grid=(depth, N_MOD * D // tn),
        in_specs=[
            pl.BlockSpec((MOD_ROWS, D), lambda i, j: (0, 0)),
            pl.BlockSpec((None, D, tn), lambda i, j: (i, 0, j)),
            pl.BlockSpec((None, 1, tn), lambda i, j: (i, 0, j)),
        ],
        out_specs=pl.BlockSpec((None, MOD_ROWS, tn), lambda i, j: (i, 0, j)),
        out_shape=jax.ShapeDtypeStruct((depth, MOD_ROWS, N_MOD * D), F32),
        compiler_params=_cparams(("arbitrary", "arbitrary")),
        name="mods",
    )(c_rows, w_mod, b_mod.reshape(depth, 1, N_MOD * D))


def _modmm_kernel(x_ref, g_ref, sc_ref, sh_ref, w_ref, o_ref, *, n_chunk):
    h = _rms_mod(x_ref[...], g_ref[...], sc_ref[...], sh_ref[...]).astype(BF16)
    n = w_ref.shape[1]
    per = n_chunk // TC
    for j in range(n // n_chunk):
        sl = slice(j * n_chunk, (j + 1) * n_chunk)
        y = jnp.dot(h, w_ref[:, sl], preferred_element_type=F32).astype(o_ref.dtype)
        for c in range(per):
            o_ref[j * per + c] = y[:, c * TC:(c + 1) * TC]


def _modmm(x, g, scale, shift, w):
    n = w.shape[1]
    return pl.pallas_call(
        functools.partial(_modmm_kernel, n_chunk=1024),
        grid=(B, S // TM),
        in_specs=[
            pl.BlockSpec((None, TM, D), lambda b, t: (b, t, 0)),
            pl.BlockSpec((1, D), lambda b, t: (0, 0)),
            pl.BlockSpec((None, 1, D), lambda b, t: (b, 0, 0)),
            pl.BlockSpec((None, 1, D), lambda b, t: (b, 0, 0)),
            pl.BlockSpec((D, n), lambda b, t: (0, 0)),
        ],
        out_specs=pl.BlockSpec((None, n // TC, TM, TC), lambda b, t: (b, 0, t, 0)),
        out_shape=jax.ShapeDtypeStruct((B, n // TC, S, TC), BF16),
        compiler_params=_cparams(("arbitrary", "arbitrary")),
        name="modmm",
    )(x, g, scale, shift, w)


def _rope_tiles(xs, cos, sin_lo, sin_hi, n):
    lo = [pltpu.roll(x, LANE - n, 1) for x in xs]
    hi = [pltpu.roll(x, n, 1) for x in xs]
    return [x * cos + a * sin_lo + b * sin_hi for x, a, b in zip(xs, lo, hi)]


def _attn_prep_kernel(x_ref, ctx_ref, g_ref, sc_ref, sh_ref, csc_ref, csh_ref, tab_ref,
                      win_ref, wqb_ref, wk_ref, wv_ref, qag_ref, kvag_ref, qng_ref, kng_ref,
                      q_ref, k_ref, vt_ref):
    t = pl.program_id(1)
    is_ctx = t == 0
    src = jnp.where(is_ctx, ctx_ref[...], x_ref[...])
    scale = jnp.where(is_ctx, csc_ref[...], sc_ref[...])
    shift = jnp.where(is_ctx, csh_ref[...], sh_ref[...])
    h = _rms_mod(src, g_ref[...], scale, shift).astype(BF16)
    z = jnp.dot(h, win_ref[...], preferred_element_type=F32)

    cos_m, slo_m, shi_m = tab_ref[0], tab_ref[1], tab_ref[2]
    cos_g, slo_g, shi_g = tab_ref[3], tab_ref[4], tab_ref[5]
    ones_col = (lax.broadcasted_iota(jnp.int32, (1, LANE), 1) == HEAD).astype(F32)

    o_kr = Q_LORA + KV_LORA
    o_gq = o_kr + LANE
    o_gk = o_gq + GQA_HEADS * LANE
    o_gv = o_gk + GQA_KV * LANE

    def rms(v, g):
        return v * lax.rsqrt(jnp.mean(v * v, axis=-1, keepdims=True) + EPS) * g

    def tile(v, off, i):
        return v[:, off + i * LANE:off + (i + 1) * LANE]

    s_mla = LOG2E / math.sqrt(HEAD + MLA_ROPE)
    s_gqa = LOG2E / math.sqrt(HEAD)

    cq = rms(z[:, :Q_LORA], qag_ref[...] * s_mla).astype(BF16)
    ckv = rms(z[:, Q_LORA:o_kr], kvag_ref[...]).astype(BF16)
    q = jnp.dot(cq, wqb_ref[...], preferred_element_type=F32)
    kn = jnp.dot(ckv, wk_ref[...], preferred_element_type=F32)
    vm = jnp.dot(ckv, wv_ref[...], preferred_element_type=F32)

    gqk = [tile(z, o_gq, i) for i in range(GQA_HEADS)] + [tile(z, o_gk, j) for j in range(GQA_KV)]
    gains = [qng_ref[...] * s_gqa] * GQA_HEADS + [kng_ref[...]] * GQA_KV
    ssq = [jnp.sum(v * v, axis=-1, keepdims=True) for v in gqk]
    gqk = [v * lax.rsqrt(s * (1.0 / HEAD) + EPS) * g for v, s, g in zip(gqk, ssq, gains)]
    gqk = _rope_tiles(gqk, cos_g, slo_g, shi_g, HEAD // 4)

    mla = [tile(q, 0, i) for i in range(MLA_HEADS)] + [z[:, o_kr:o_gq]]
    mla = _rope_tiles(mla, cos_m, slo_m, shi_m, MLA_ROPE // 4)
    kr = mla[MLA_HEADS]

    def put(ref, head, val):
        grp, hh = divmod(head, HEADS_PER_STEP)
        ref[grp, :, hh * LANE:(hh + 1) * LANE] = val

    for hh in range(MLA_HEADS):
        sl = slice(hh * LANE, (hh + 1) * LANE)
        put(q_ref, hh, mla[hh].astype(BF16))
        put(k_ref, hh, (kn[:, sl] + kr).astype(BF16))
        vt_ref[sl, :] = (vm[:, sl] + ones_col).T.astype(BF16)
    for hh in range(GQA_HEADS):
        put(q_ref, MLA_HEADS + hh, gqk[hh].astype(BF16))
    rep = GQA_HEADS // GQA_KV
    for j in range(GQA_KV):
        gk = gqk[GQA_HEADS + j].astype(BF16)
        gvt = (tile(z, o_gv, j) + ones_col).T.astype(BF16)
        for r in range(rep):
            hh = MLA_HEADS + j * rep + r
            put(k_ref, hh, gk)
            vt_ref[hh * LANE:(hh + 1) * LANE, :] = gvt


def _attn_prep(x, ctx, g, sc, sh, csc, csh, tabs, win, wqb, wk, wv, qag, kvag, qng, kng):
    nw = win.shape[1]
    lat = lambda b, t: (b, jnp.maximum(t - 1, 0), 0)
    groups = N_HEADS // HEADS_PER_STEP
    nl = HEADS_PER_STEP * LANE
    full2 = lambda b, t: (0, 0)
    return pl.pallas_call(
        _attn_prep_kernel,
        grid=(B, T // TP),
        in_specs=[
            pl.BlockSpec((None, TP, D), lat),
            pl.BlockSpec((None, CTX, D), lambda b, t: (b, 0, 0)),
            pl.BlockSpec((1, D), full2),
            pl.BlockSpec((None, 1, D), lambda b, t: (b, 0, 0)),
            pl.BlockSpec((None, 1, D), lambda b, t: (b, 0, 0)),
            pl.BlockSpec((1, D), full2),
            pl.BlockSpec((1, D), full2),
            pl.BlockSpec((6, TP, LANE), lambda b, t: (0, t, 0)),
            pl.BlockSpec((D, nw), full2),
            pl.BlockSpec((Q_LORA, MLA_HEADS * LANE), full2),
            pl.BlockSpec((KV_LORA, MLA_HEADS * LANE), full2),
            pl.BlockSpec((KV_LORA, MLA_HEADS * LANE), full2),
            pl.BlockSpec((1, Q_LORA), full2),
            pl.BlockSpec((1, KV_LORA), full2),
            pl.BlockSpec((1, LANE), full2),
            pl.BlockSpec((1, LANE), full2),
        ],
        out_specs=[
            pl.BlockSpec((None, groups, TP, nl), lambda b, t: (b, 0, jnp.maximum(t - 1, 0), 0)),
            pl.BlockSpec((None, groups, TP, nl), lambda b, t: (b, 0, t, 0)),
            pl.BlockSpec((None, N_HEADS * LANE, TP), lambda b, t: (b, 0, t)),
        ],
        out_shape=[
            jax.ShapeDtypeStruct((B, groups, S, nl), BF16),
            jax.ShapeDtypeStruct((B, groups, T, nl), BF16),
            jax.ShapeDtypeStruct((B, N_HEADS * LANE, T), BF16),
        ],
        compiler_params=_cparams(("arbitrary", "arbitrary")),
        name="attn_prep",
    )(x, ctx, g, sc, sh, csc, csh, tabs, win, wqb, wk, wv, qag, kvag, qng, kng)


def _attn_kernel(q_ref, k_ref, vt_ref, o_ref, s_buf, p_buf, m_buf):
    def scores(h):
        sl = slice(h * LANE, (h + 1) * LANE)
        st = lax.dot_general(k_ref[:, sl], q_ref[:, sl], (((1,), (1,)), ((), ())),
                             preferred_element_type=F32)
        s_buf[h % 2] = st
        m_buf[h % 2] = jnp.max(st, axis=0, keepdims=True)

    scores(0)
    for h in range(HEADS_PER_STEP):
        sl = slice(h * LANE, (h + 1) * LANE)
        if h + 1 < HEADS_PER_STEP:
            scores(h + 1)
        p_buf[h % 2] = jnp.exp2(s_buf[h % 2] - m_buf[h % 2]).astype(BF16)
        ot = jnp.dot(vt_ref[sl, :], p_buf[h % 2], preferred_element_type=F32)
        ot = ot[:HEAD, :] / ot[HEAD:HEAD + 1, :]
        if h % 2 == 0:
            even_head = ot
        else:
            pair = jnp.concatenate([even_head, ot], axis=0)
            o_ref[:, (h // 2) * LANE:(h // 2 + 1) * LANE] = pair.T.astype(BF16)


def _attention(q, k, vt):
    nl = HEADS_PER_STEP * LANE
    groups = N_HEADS // HEADS_PER_STEP
    return pl.pallas_call(
        _attn_kernel,
        grid=(B, groups, S // TQ),
        in_specs=[
            pl.BlockSpec((None, None, TQ, nl), lambda b, g, i: (b, g, i, 0)),
            pl.BlockSpec((None, None, T, nl), lambda b, g, i: (b, g, 0, 0)),
            pl.BlockSpec((None, nl, T), lambda b, g, i: (b, g, 0)),
        ],
        out_specs=pl.BlockSpec((None, None, TQ, HEADS_PER_STEP * HEAD), lambda b, g, i: (b, g, i, 0)),
        out_shape=jax.ShapeDtypeStruct((B, groups, S, HEADS_PER_STEP * HEAD), BF16),
        scratch_shapes=[pltpu.VMEM((2, T, TQ), F32), pltpu.VMEM((2, T, TQ), BF16),
                        pltpu.VMEM((2, 1, TQ), F32)],
        compiler_params=_cparams(("arbitrary", "arbitrary", "arbitrary")),
        name="attention",
    )(q, k, vt)


def _mlp_kernel(a_ref, wo_ref, g1_ref, x_ref, g_ref, sc_ref, sh_ref, gate_ref, w1_ref, w2_ref,
                fg_ref, o_ref, *, final, f_chunk):
    a = jnp.concatenate([a_ref[j] for j in range(a_ref.shape[0])], axis=1)
    x = x_ref[...] + g1_ref[...] * jnp.dot(a, wo_ref[...], preferred_element_type=F32)
    h = _rms_mod(x, g_ref[...], sc_ref[...], sh_ref[...]).astype(BF16)
    acc = jnp.zeros(x.shape, F32)
    for j in range(D_FF // f_chunk):
        sl = slice(j * f_chunk, (j + 1) * f_chunk)
        a = jnp.maximum(jnp.dot(h, w1_ref[:, sl], preferred_element_type=F32), 0.0)
        acc = acc + jnp.dot((a * a).astype(BF16), w2_ref[sl, :], preferred_element_type=F32)
    y = x + gate_ref[...] * acc
    if final:
        y = y * lax.rsqrt(jnp.mean(y * y, axis=-1, keepdims=True) + EPS) * fg_ref[...]
    o_ref[...] = y


def _mlp(a, wo, gate1, x, g, scale, shift, gate, w1, w2, final_g, final):
    const = lambda b, t: (0, 0)
    groups, width = a.shape[1], a.shape[3]
    return pl.pallas_call(
        functools.partial(_mlp_kernel, final=final, f_chunk=1024),
        grid=(B, S // TM),
        in_specs=[
            pl.BlockSpec((None, groups, TM, width), lambda b, t: (b, 0, t, 0)),
            pl.BlockSpec((D, D), const, pipeline_mode=pl.Buffered(1)),
            pl.BlockSpec((None, 1, D), lambda b, t: (b, 0, 0)),
            pl.BlockSpec((None, TM, D), lambda b, t: (b, t, 0)),
            pl.BlockSpec((1, D), const),
            pl.BlockSpec((None, 1, D), lambda b, t: (b, 0, 0)),
            pl.BlockSpec((None, 1, D), lambda b, t: (b, 0, 0)),
            pl.BlockSpec((None, 1, D), lambda b, t: (b, 0, 0)),
            pl.BlockSpec((D, D_FF), const, pipeline_mode=pl.Buffered(1)),
            pl.BlockSpec((D_FF, D), const, pipeline_mode=pl.Buffered(1)),
            pl.BlockSpec((1, D), const),
        ],
        out_specs=pl.BlockSpec((None, TM, D), lambda b, t: (b, t, 0)),
        out_shape=jax.ShapeDtypeStruct((B, S, D), F32),
        compiler_params=_cparams(("arbitrary", "arbitrary")),
        name="mlp",
    )(a, wo, gate1, x, g, scale, shift, gate, w1, w2, final_g)


def _build_trig(cos_dst, sin_dst, tw_ref, a0, nblk, b0_of_block):
    for i in range(nblk):
        ca = tw_ref[0, a0 + i:a0 + i + 1, :]
        sa = tw_ref[1, a0 + i:a0 + i + 1, :]
        b0 = b0_of_block(i)
        cb = tw_ref[0, b0:b0 + TWB, :]
        sb = tw_ref[1, b0:b0 + TWB, :]
        rs = slice(i * TWB, (i + 1) * TWB)
        cos_dst[rs, :] = (ca * cb - sa * sb).astype(BF16)
        sin_dst[rs, :] = (sa * cb + ca * sb).astype(BF16)


def _hyena_filter_kernel(feat_ref, w1_ref, b1_ref, w2_ref, b2_ref, w3_ref, b3_ref, fr_ref,
                         w4f0_ref, w4f1_ref, w4b0_ref, w4b1_ref, dl_ref, tw_ref,
                         kr_ref, ki_ref, co_ref, so_ref):
    @pl.when(pl.program_id(0) == 0)
    def _():
        nblk = S // TWB
        _build_trig(co_ref, so_ref, tw_ref, 0, nblk,
                    lambda i: nblk + (0 if i < nblk // 2 else TWB))

    hp = lax.Precision.HIGHEST
    hid = jnp.sin(fr_ref[0:1, :] * (jnp.dot(feat_ref[...], w1_ref[...], precision=hp,
                                            preferred_element_type=F32) + b1_ref[...]))
    hid = jnp.sin(fr_ref[1:2, :] * (jnp.dot(hid, w2_ref[...], precision=hp,
                                            preferred_element_type=F32) + b2_ref[...]))
    hid = jnp.sin(fr_ref[2:3, :] * (jnp.dot(hid, w3_ref[...], precision=hp,
                                            preferred_element_type=F32) + b3_ref[...]))
    row = lax.broadcasted_iota(jnp.int32, (S, TC), 0)
    t_norm = row.astype(F32) / S
    window = jnp.exp(-t_norm * dl_ref[...])
    for order, (wf_ref, wb_ref) in enumerate(((w4f0_ref, w4b0_ref), (w4f1_ref, w4b1_ref))):
        hf = jnp.dot(hid, wf_ref[...], precision=hp, preferred_element_type=F32) * window
        hb = jnp.dot(hid, wb_ref[...], precision=hp, preferred_element_type=F32) * window
        ss = jnp.sum(hf * hf + hb * hb, axis=0, keepdims=True)
        nrm = lax.rsqrt(ss + EPS)
        hf = hf * nrm
        hb = jnp.where(row == 0, 0.0, hb * nrm)
        hsum = (hf + hb).astype(BF16)
        hdif = (hf - hb).astype(BF16)
        for r in range(S // DFT_ROWS):
            rs = slice(r * DFT_ROWS, (r + 1) * DFT_ROWS)
            kr_ref[order, rs, :] = (2.0 / NFFT) * jnp.dot(co_ref[rs, :], hsum,
                                                          preferred_element_type=F32)
            ki_ref[order, rs, :] = (-2.0 / NFFT) * jnp.dot(so_ref[rs, :], hdif,
                                                           preferred_element_type=F32)


def _hyena_filters(feats, w1, b1, w2, b2, w3, b3, freq, w4, deltas, tw):
    nc = D // TC
    const = lambda c: (0, 0)
    w4spec = lambda k: pl.BlockSpec((HY_HID, TC), lambda c, k=k: (0, k * nc + c))
    return pl.pallas_call(
        _hyena_filter_kernel,
        grid=(nc,),
        in_specs=[
            pl.BlockSpec((S, HY_EMB_PAD), const),
            pl.BlockSpec((HY_EMB_PAD, HY_HID), const),
            pl.BlockSpec((1, HY_HID), const),
            pl.BlockSpec((HY_HID, HY_HID), const),
            pl.BlockSpec((1, HY_HID), const),
            pl.BlockSpec((HY_HID, HY_HID), const),
            pl.BlockSpec((1, HY_HID), const),
            pl.BlockSpec((3, HY_HID), const),
            w4spec(0), w4spec(1), w4spec(2), w4spec(3),
            pl.BlockSpec((1, TC), lambda c: (0, c)),
            pl.BlockSpec(tw.shape, lambda c: (0, 0, 0), pipeline_mode=pl.Buffered(1)),
        ],
        out_specs=[
            pl.BlockSpec((2, S, TC), lambda c: (0, 0, c)),
            pl.BlockSpec((2, S, TC), lambda c: (0, 0, c)),
        ],
        out_shape=[
            jax.ShapeDtypeStruct((2, S, D), F32),
            jax.ShapeDtypeStruct((2, S, D), F32),
        ],
        scratch_shapes=[pltpu.VMEM((S, S), BF16), pltpu.VMEM((S, S), BF16)],
        compiler_params=_cparams(("arbitrary",)),
        name="hyena_filters",
    )(feats, w1, b1, w2, b2, w3, b3, freq, w4, w4, w4, w4, deltas, tw)


def _hyena_conv_kernel(x1_ref, x2_ref, v_ref, cw1_ref, cw2_ref, cwv_ref, cb1_ref, cb2_ref,
                       cbv_ref, kr_ref, ki_ref, skip_ref, tw_ref, y_ref,
                       cos_tab, sin_tab, y_buf, g_buf, u_buf, p_buf, nat_buf):
    hs = S // 2
    lanes = [slice(l * LANE, (l + 1) * LANE) for l in range(TC // LANE)]

    def split_even_odd(z_ref):
        for l, ls in enumerate(lanes):
            nat_buf[l] = z_ref[:, ls].astype(F32)
        even = jnp.concatenate([nat_buf[l, pl.ds(0, hs, stride=2), :] for l in range(len(lanes))], axis=1)
        odd = jnp.concatenate([nat_buf[l, pl.ds(1, hs, stride=2), :] for l in range(len(lanes))], axis=1)
        return even, odd
    nblk = hs // TWB

    @pl.when((pl.program_id(0) == 0) & (pl.program_id(1) == 0))
    def _():
        for fam in range(4):
            base = fam * (nblk + TWB)
            _build_trig(cos_tab.at[fam], sin_tab.at[fam], tw_ref, base, nblk,
                        lambda i, base=base: base + nblk)

    row = lax.broadcasted_iota(jnp.int32, (hs, TC), 0)

    def short_conv(z_ref, w_ref, b_ref, dst):
        ze, zo = split_even_odd(z_ref)
        zo_prev = jnp.where(row == 0, 0.0, pltpu.roll(zo, 1, 0))
        ze_next = jnp.where(row == hs - 1, 0.0, pltpu.roll(ze, hs - 1, 0))
        w0, w1, w2, b = w_ref[0:1, :], w_ref[1:2, :], w_ref[2:3, :], b_ref[...]
        dst[0] = b + zo_prev * w0 + ze * w1 + zo * w2
        dst[1] = b + ze * w0 + zo * w1 + ze_next * w2

    chunks = [slice(r * DFT_ROWS, (r + 1) * DFT_ROWS) for r in range(hs // DFT_ROWS)]

    def mm(tab, fam, rs, rhs):
        return jnp.dot(tab[fam, rs, :], rhs, preferred_element_type=F32)

    short_conv(v_ref, cwv_ref, cbv_ref, y_buf)
    gates = ((x1_ref, cw1_ref, cb1_ref), (x2_ref, cw2_ref, cb2_ref))
    for order, (z_ref, w_ref, b_ref) in enumerate(gates):
        u_buf[...] = y_buf[...].astype(BF16)
        for rs in chunks:
            up = slice(hs + rs.start, hs + rs.stop)
            ea, eb = mm(cos_tab, 0, rs, u_buf[0]), mm(sin_tab, 0, rs, u_buf[0])
            oa, ob = mm(cos_tab, 1, rs, u_buf[1]), mm(sin_tab, 1, rs, u_buf[1])
            xre, xim = ea + oa, -(eb + ob)
            ure, uim = eb - ob, oa - ea
            kr, ki = kr_ref[order, rs, :], ki_ref[order, rs, :]
            yre, yim = xre * kr - xim * ki, xre * ki + xim * kr
            kr, ki = kr_ref[order, up, :], ki_ref[order, up, :]
            vre, vim = ure * kr - uim * ki, ure * ki + uim * kr
            p_buf[0, rs, :] = (yre - vim).astype(BF16)
            p_buf[1, rs, :] = (vre - yim).astype(BF16)
            p_buf[2, rs, :] = (yre + vim).astype(BF16)
            p_buf[3, rs, :] = (-(vre + yim)).astype(BF16)
        short_conv(z_ref, w_ref, b_ref, g_buf)
        skip = skip_ref[order:order + 1, :]
        for rs in chunks:
            ce = mm(cos_tab, 2, rs, p_buf[0]) + mm(sin_tab, 2, rs, p_buf[1])
            co = mm(cos_tab, 3, rs, p_buf[2]) + mm(sin_tab, 3, rs, p_buf[3])
            y_buf[0, rs, :] = g_buf[0, rs, :] * (ce + y_buf[0, rs, :] * skip)
            y_buf[1, rs, :] = g_buf[1, rs, :] * (co + y_buf[1, rs, :] * skip)
    for l, ls in enumerate(lanes):
        nat_buf[l, pl.ds(0, hs, stride=2), :] = y_buf[0, :, ls]
        nat_buf[l, pl.ds(1, hs, stride=2), :] = y_buf[1, :, ls]
    for l, ls in enumerate(lanes):
        y_ref[:, ls] = nat_buf[l].astype(y_ref.dtype)


def _hyena_conv(z, conv_w, conv_b, kr, ki, skip, tw):
    nc = D // TC
    hs = S // 2
    zspec = lambda k: pl.BlockSpec((None, None, S, TC), lambda c, b, k=k: (b, k * nc + c, 0, 0))
    wspec = lambda k: pl.BlockSpec((3, TC), lambda c, b, k=k: (0, k * nc + c))
    bspec = lambda k: pl.BlockSpec((1, TC), lambda c, b, k=k: (0, k * nc + c))
    return pl.pallas_call(
        _hyena_conv_kernel,
        grid=(nc, B),
        in_specs=[
            zspec(0), zspec(1), zspec(2),
            wspec(0), wspec(1), wspec(2),
            bspec(0), bspec(1), bspec(2),
            pl.BlockSpec((2, S, TC), lambda c, b: (0, 0, c), pipeline_mode=pl.Buffered(1)),
            pl.BlockSpec((2, S, TC), lambda c, b: (0, 0, c), pipeline_mode=pl.Buffered(1)),
            pl.BlockSpec((2, TC), lambda c, b: (0, c)),
            pl.BlockSpec(tw.shape, lambda c, b: (0, 0, 0), pipeline_mode=pl.Buffered(1)),
        ],
        out_specs=pl.BlockSpec((None, None, S, TC), lambda c, b: (b, c, 0, 0)),
        out_shape=jax.ShapeDtypeStruct((B, nc, S, TC), BF16),
        scratch_shapes=[pltpu.VMEM((4, hs, hs), BF16), pltpu.VMEM((4, hs, hs), BF16),
                        pltpu.VMEM((2, hs, TC), F32), pltpu.VMEM((2, hs, TC), F32),
                        pltpu.VMEM((2, hs, TC), BF16), pltpu.VMEM((4, hs, TC), BF16),
                        pltpu.VMEM((TC // LANE, S, LANE), F32)],
        compiler_params=_cparams(("arbitrary", "arbitrary")),
        name="hyena_conv",
    )(z, z, z, conv_w, conv_w, conv_w, conv_b, conv_b, conv_b, kr, ki, skip, tw)


def _pad_heads(w, heads, d, front=0):
    k = w.shape[0]
    w = w.reshape(k, heads, d)
    w = jnp.pad(w, ((0, 0), (0, 0), (front, LANE - d - front)))
    return w.reshape(k, heads * LANE)


def _rope_tables():
    pos = jnp.arange(S, dtype=jnp.int32)
    rowf = (pos // GRID_W).astype(F32)
    colf = (pos % GRID_W).astype(F32)

    def pattern(base, half):
        n = half // 2
        inv = ROPE_THETA ** (-jnp.arange(n, dtype=F32) / n)
        cos_cols, lo_cols, hi_cols = [], [], []
        for p in (rowf, colf):
            ang = p[:, None] * inv[None]
            c, s = jnp.cos(ang), jnp.sin(ang)
            zero = jnp.zeros_like(s)
            cos_cols += [c, c]
            lo_cols += [-s, zero]
            hi_cols += [zero, s]
        width = 2 * half

        def place(cols, fill):
            body = jnp.concatenate(cols, axis=1)
            return jnp.concatenate([jnp.full((S, base), fill, F32), body,
                                    jnp.full((S, LANE - base - width), fill, F32)], axis=1)
        return place(cos_cols, 1.0), place(lo_cols, 0.0), place(hi_cols, 0.0)

    tabs = jnp.stack(pattern(HEAD, MLA_ROPE // 2) + pattern(0, HEAD // 2))
    ident = jnp.stack([jnp.ones((CTX, LANE), F32), jnp.zeros((CTX, LANE), F32),
                       jnp.zeros((CTX, LANE), F32)] * 2)
    return jnp.concatenate([ident, tabs], axis=1)


def _trig_factors(rows):
    ang = (rows % (4 * NFFT)).astype(F32) * (2.0 * math.pi / (4 * NFFT))
    return jnp.stack([jnp.cos(ang), jnp.sin(ang)])


def _conv_trig_factors():
    hs = S // 2
    i = jnp.arange(hs // TWB, dtype=jnp.int32)[:, None]
    j = jnp.arange(TWB, dtype=jnp.int32)[:, None]
    c = jnp.arange(hs, dtype=jnp.int32)[None, :]
    fams = []
    for odd in (1, 3):
        fams += [2 * TWB * i * (4 * c + odd), (2 * j + 1) * (4 * c + odd)]
    for odd in (1, 3):
        fams += [4 * TWB * i * (2 * c + 1), (4 * j + odd) * (2 * c + 1)]
    return _trig_factors(jnp.concatenate(fams, axis=0))


def _filter_trig_factors():
    nblk = S // TWB
    i = jnp.arange(nblk, dtype=jnp.int32)[:, None]
    j = jnp.arange(TWB, dtype=jnp.int32)[:, None]
    n = jnp.arange(S, dtype=jnp.int32)[None, :]
    blk = 2 * TWB * i * 2 * n
    blk = jnp.where(i < nblk // 2, blk, -blk)
    lo = (2 * j + 1) * 2 * n
    hi = (2 * (S + S // 2 - 1) + 1 - 2 * j) * 2 * n
    return _trig_factors(jnp.concatenate([blk, lo, hi], axis=0))


def _hyena_features():
    t = jnp.arange(S, dtype=F32)
    t_norm = t / S
    w = 2.0 * math.pi * t / S
    bands = jnp.linspace(1e-4, HY_BANDS - 1, HY_BANDS, dtype=F32)
    fw = w[:, None] * bands[None]
    feats = jnp.concatenate([t_norm[:, None], jnp.cos(fw), -jnp.sin(fw)], axis=-1)
    feats = jnp.pad(feats, ((0, 0), (0, HY_EMB_PAD - HY_EMB)))
    max_decay = math.log(1e-2) / 0.3
    min_decay = math.log(1e-2) / 1.5
    deltas = jnp.abs(jnp.linspace(min_decay, max_decay, D, dtype=F32))
    return feats, deltas.reshape(1, D)


def kernel(x, c, ctx, c_ctx, w_mod, b_mod, norm1_g, norm2_g, mlp_w1, mlp_w2, a_w_in, a_q_a_g, a_w_q_b, a_kv_a_g, a_w_kv_b, a_q_norm_g, a_k_norm_g, a_w_out, h_w_in, h_conv_w, h_conv_b, h_f_w1, h_f_b1, h_f_w2, h_f_b2, h_f_w3, h_f_b3, h_f_freq, h_f_w4, h_skip, h_w_out, final_g):
    assert x.shape == (B, S, D) and ctx.shape == (B, CTX, D) and w_mod.shape[0] == 2

    c_rows = jnp.concatenate([c, c_ctx[None], jnp.zeros((MOD_ROWS - B - 1, D), F32)], axis=0)
    mods = _mods(c_rows, w_mod, b_mod)

    def lat_mod(i, k):
        return mods[i, :B, k * D:(k + 1) * D].reshape(B, 1, D)

    def ctx_mod(i, k):
        return mods[i, B:B + 1, k * D:(k + 1) * D]

    w_in = a_w_in[0]
    o_kr = Q_LORA + KV_LORA
    o_gq = o_kr + MLA_ROPE
    o_gk = o_gq + GQA_HEADS * HEAD
    o_gv = o_gk + GQA_KV * HEAD
    win = jnp.concatenate([
        w_in[:, :o_kr],
        _pad_heads(w_in[:, o_kr:o_gq], 1, MLA_ROPE, front=HEAD),
        _pad_heads(w_in[:, o_gq:o_gk], GQA_HEADS, HEAD),
        _pad_heads(w_in[:, o_gk:o_gv], GQA_KV, HEAD),
        _pad_heads(w_in[:, o_gv:], GQA_KV, HEAD),
    ], axis=1).astype(BF16)
    wqb = _pad_heads(a_w_q_b[0], MLA_HEADS, HEAD + MLA_ROPE).astype(BF16)
    wkv = a_w_kv_b[0].reshape(KV_LORA, MLA_HEADS, 2 * HEAD)
    wk = _pad_heads(wkv[:, :, :HEAD].reshape(KV_LORA, MLA_HEADS * HEAD), MLA_HEADS, HEAD).astype(BF16)
    wv = _pad_heads(wkv[:, :, HEAD:].reshape(KV_LORA, MLA_HEADS * HEAD), MLA_HEADS, HEAD).astype(BF16)
    qng = jnp.pad(a_q_norm_g[0], (0, LANE - HEAD)).reshape(1, LANE)
    kng = jnp.pad(a_k_norm_g[0], (0, LANE - HEAD)).reshape(1, LANE)

    q, k, vt = _attn_prep(x, ctx, norm1_g[0:1], lat_mod(0, 1), lat_mod(0, 0),
                          ctx_mod(0, 1), ctx_mod(0, 0), _rope_tables(), win, wqb, wk, wv,
                          a_q_a_g[0:1], a_kv_a_g[0:1], qng, kng)
    o = _attention(q, k, vt)
    x = _mlp(o, a_w_out[0].astype(BF16), lat_mod(0, 2), x,
             norm2_g[0:1], lat_mod(0, 4), lat_mod(0, 3), lat_mod(0, 5),
             mlp_w1[0].astype(BF16), mlp_w2[0].astype(BF16), final_g.reshape(1, D), final=False)

    feats, deltas = _hyena_features()
    kr, ki = _hyena_filters(
        feats, jnp.pad(h_f_w1[0], ((0, HY_EMB_PAD - HY_EMB), (0, 0))), h_f_b1[0:1],
        h_f_w2[0], h_f_b2[0:1], h_f_w3[0], h_f_b3[0:1], h_f_freq[0], h_f_w4[0], deltas,
        _filter_trig_factors())
    z = _modmm(x, norm1_g[1:2], lat_mod(1, 1), lat_mod(1, 0), h_w_in[0].astype(BF16))
    y = _hyena_conv(z, h_conv_w[0], h_conv_b[0:1], kr, ki, h_skip[0], _conv_trig_factors())
    x = _mlp(y, h_w_out[0].astype(BF16), lat_mod(1, 2), x,
             norm2_g[1:2], lat_mod(1, 4), lat_mod(1, 3), lat_mod(1, 5),
             mlp_w1[1].astype(BF16), mlp_w2[1].astype(BF16), final_g.reshape(1, D), final=True)
    return x
```

```python
import functools
import math

import jax
import jax.numpy as jnp
from jax import lax
from jax.experimental import pallas as pl
from jax.experimental.pallas import tpu as pltpu

F32 = jnp.float32
BF16 = jnp.bfloat16

D = 1024
B = 16
S = 2048
CTX = 256
T = CTX + S
GRID_W = 64
D_FF = 4 * D
N_MOD = 6
HEAD = 64
MLA_HEADS = 8
MLA_ROPE = 32
Q_LORA = 384
KV_LORA = 256
GQA_HEADS = 8
GQA_KV = 2
N_HEADS = MLA_HEADS + GQA_HEADS
LANE = 128
LOG2E = 1.4426950408889634
ROPE_THETA = 10000.0
EPS = 1e-6
HY_BANDS = 8
HY_EMB = 1 + 2 * HY_BANDS
HY_EMB_PAD = 32
HY_HID = 64
NFFT = 2 * S

VMEM_LIMIT = 60 * 1024 * 1024

MOD_ROWS = 24
TM = 512
TP = 256
TQ = 512
HEADS_PER_STEP = 8
TC = 256
TWB = 128
TW_PAD = 8


def _cparams(sem):
    return pltpu.CompilerParams(dimension_semantics=sem, vmem_limit_bytes=VMEM_LIMIT)


def _rms_mod(x, g, scale, shift):
    ms = jnp.mean(x * x, axis=-1, keepdims=True)
    return x * lax.rsqrt(ms + EPS) * (g * (1.0 + scale)) + shift


def _mods_kernel(c_ref, w_ref, b_ref, o_ref):
    c = c_ref[...]
    s = c * (1.0 / (1.0 + jnp.exp(-c)))
    o_ref[...] = jnp.dot(s.astype(BF16), w_ref[...].astype(BF16),
                         preferred_element_type=F32) + b_ref[...]


def _mods(c_rows, w_mod, b_mod):
    depth = w_mod.shape[0]
    tn = 1024
    return pl.pallas_call(
        _mods_kernel,
        grid=(depth, N_MOD * D // tn),
        in_specs=[
            pl.BlockSpec((MOD_ROWS, D), lambda i, j: (0, 0)),
            pl.BlockSpec((None, D, tn), lambda i, j: (i, 0, j)),
            pl.BlockSpec((None, 1, tn), lambda i, j: (i, 0, j)),
        ],
        out_specs=pl.BlockSpec((None, MOD_ROWS, tn), lambda i, j: (i, 0, j)),
        out_shape=jax.ShapeDtypeStruct((depth, MOD_ROWS, N_MOD * D), F32),
        compiler_params=_cparams(("arbitrary", "arbitrary")),
        name="mods",
    )(c_rows, w_mod, b_mod.reshape(depth, 1, N_MOD * D))


def _modmm_kernel(x_ref, g_ref, sc_ref, sh_ref, w_ref, o_ref, *, n_chunk):
    h = _rms_mod(x_ref[...], g_ref[...], sc_ref[...], sh_ref[...]).astype(BF16)
    n = w_ref.shape[1]
    per = n_chunk // TC
    for j in range(n // n_chunk):
        sl = slice(j * n_chunk, (j + 1) * n_chunk)
        y = jnp.dot(h, w_ref[:, sl], preferred_element_type=F32).astype(o_ref.dtype)
        for c in range(per):
            o_ref[j * per + c] = y[:, c * TC:(c + 1) * TC]


def _modmm(x, g, scale, shift, w):
    n = w.shape[1]
    return pl.pallas_call(
        functools.partial(_modmm_kernel, n_chunk=1024),
        grid=(B, S // TM),
        in_specs=[
            pl.BlockSpec((None, TM, D), lambda b, t: (b, t, 0)),
            pl.BlockSpec((1, D), lambda b, t: (0, 0)),
            pl.BlockSpec((None, 1, D), lambda b, t: (b, 0, 0)),
            pl.BlockSpec((None, 1, D), lambda b, t: (b, 0, 0)),
            pl.BlockSpec((D, n), lambda b, t: (0, 0)),
        ],
        out_specs=pl.BlockSpec((None, n // TC, TM, TC), lambda b, t: (b, 0, t, 0)),
        out_shape=jax.ShapeDtypeStruct((B, n // TC, S, TC), BF16),
        compiler_params=_cparams(("arbitrary", "arbitrary")),
        name="modmm",
    )(x, g, scale, shift, w)


def _rope_tiles(xs, cos, sin_lo, sin_hi, n):
    lo = [pltpu.roll(x, LANE - n, 1) for x in xs]
    hi = [pltpu.roll(x, n, 1) for x in xs]
    return [x * cos + a * sin_lo + b * sin_hi for x, a, b in zip(xs, lo, hi)]


def _attn_prep_kernel(x_ref, ctx_ref, g_ref, sc_ref, sh_ref, csc_ref, csh_ref, tab_ref,
                      win_ref, wqb_ref, wk_ref, wv_ref, qag_ref, kvag_ref, qng_ref, kng_ref,
                      q_ref, k_ref, vt_ref):
    t = pl.program_id(1)
    is_ctx = t == 0
    src = jnp.where(is_ctx, ctx_ref[...], x_ref[...])
    scale = jnp.where(is_ctx, csc_ref[...], sc_ref[...])
    shift = jnp.where(is_ctx, csh_ref[...], sh_ref[...])
    h = _rms_mod(src, g_ref[...], scale, shift).astype(BF16)
    z = jnp.dot(h, win_ref[...], preferred_element_type=F32)

    cos_m, slo_m, shi_m = tab_ref[0], tab_ref[1], tab_ref[2]
    cos_g, slo_g, shi_g = tab_ref[3], tab_ref[4], tab_ref[5]
    ones_col = (lax.broadcasted_iota(jnp.int32, (1, LANE), 1) == HEAD).astype(F32)

    o_kr = Q_LORA + KV_LORA
    o_gq = o_kr + LANE
    o_gk = o_gq + GQA_HEADS * LANE
    o_gv = o_gk + GQA_KV * LANE

    def rms(v, g):
        return v * lax.rsqrt(jnp.mean(v * v, axis=-1, keepdims=True) + EPS) * g

    def tile(v, off, i):
        return v[:, off + i * LANE:off + (i + 1) * LANE]

    s_mla = LOG2E / math.sqrt(HEAD + MLA_ROPE)
    s_gqa = LOG2E / math.sqrt(HEAD)

    cq = rms(z[:, :Q_LORA], qag_ref[...] * s_mla).astype(BF16)
    ckv = rms(z[:, Q_LORA:o_kr], kvag_ref[...]).astype(BF16)
    q = jnp.dot(cq, wqb_ref[...], preferred_element_type=F32)
    kn = jnp.dot(ckv, wk_ref[...], preferred_element_type=F32)
    vm = jnp.dot(ckv, wv_ref[...], preferred_element_type=F32)

    gqk = [tile(z, o_gq, i) for i in range(GQA_HEADS)] + [tile(z, o_gk, j) for j in range(GQA_KV)]
    gains = [qng_ref[...] * s_gqa] * GQA_HEADS + [kng_ref[...]] * GQA_KV
    ssq = [jnp.sum(v * v, axis=-1, keepdims=True) for v in gqk]
    gqk = [v * lax.rsqrt(s * (1.0 / HEAD) + EPS) * g for v, s, g in zip(gqk, ssq, gains)]
    gqk = _rope_tiles(gqk, cos_g, slo_g, shi_g, HEAD // 4)

    mla = [tile(q, 0, i) for i in range(MLA_HEADS)] + [z[:, o_kr:o_gq]]
    mla = _rope_tiles(mla, cos_m, slo_m, shi_m, MLA_ROPE // 4)
    kr = mla[MLA_HEADS]

    def put(ref, head, val):
        grp, hh = divmod(head, HEADS_PER_STEP)
        ref[grp, :, hh * LANE:(hh + 1) * LANE] = val

    for hh in range(MLA_HEADS):
        sl = slice(hh * LANE, (hh + 1) * LANE)
        put(q_ref, hh, mla[hh].astype(BF16))
        put(k_ref, hh, (kn[:, sl] + kr).astype(BF16))
        vt_ref[sl, :] = (vm[:, sl] + ones_col).T.astype(BF16)
    for hh in range(GQA_HEADS):
        put(q_ref, MLA_HEADS + hh, gqk[hh].astype(BF16))
    rep = GQA_HEADS // GQA_KV
    for j in range(GQA_KV):
        gk = gqk[GQA_HEADS + j].astype(BF16)
        gvt = (tile(z, o_gv, j) + ones_col).T.astype(BF16)
        for r in range(rep):
            hh = MLA_HEADS + j * rep + r
            put(k_ref, hh, gk)
            vt_ref[hh * LANE:(hh + 1) * LANE, :] = gvt


def _attn_prep(x, ctx, g, sc, sh, csc, csh, tabs, win, wqb, wk, wv, qag, kvag, qng, kng):
    nw = win.shape[1]
    lat = lambda b, t: (b, jnp.maximum(t - 1, 0), 0)
    groups = N_HEADS // HEADS_PER_STEP
    nl = HEADS_PER_STEP * LANE
    full2 = lambda b, t: (0, 0)
    return pl.pallas_call(
        _attn_prep_kernel,
        grid=(B, T // TP),
        in_specs=[
            pl.BlockSpec((None, TP, D), lat),
            pl.BlockSpec((None, CTX, D), lambda b, t: (b, 0, 0)),
            pl.BlockSpec((1, D), full2),
            pl.BlockSpec((None, 1, D), lambda b, t: (b, 0, 0)),
            pl.BlockSpec((None, 1, D), lambda b, t: (b, 0, 0)),
            pl.BlockSpec((1, D), full2),
            pl.BlockSpec((1, D), full2),
            pl.BlockSpec((6, TP, LANE), lambda b, t: (0, t, 0)),
            pl.BlockSpec((D, nw), full2),
            pl.BlockSpec((Q_LORA, MLA_HEADS * LANE), full2),
            pl.BlockSpec((KV_LORA, MLA_HEADS * LANE), full2),
            pl.BlockSpec((KV_LORA, MLA_HEADS * LANE), full2),
            pl.BlockSpec((1, Q_LORA), full2),
            pl.BlockSpec((1, KV_LORA), full2),
            pl.BlockSpec((1, LANE), full2),
            pl.BlockSpec((1, LANE), full2),
        ],
        out_specs=[
            pl.BlockSpec((None, groups, TP, nl), lambda b, t: (b, 0, jnp.maximum(t - 1, 0), 0)),
            pl.BlockSpec((None, groups, TP, nl), lambda b, t: (b, 0, t, 0)),
            pl.BlockSpec((None, N_HEADS * LANE, TP), lambda b, t: (b, 0, t)),
        ],
        out_shape=[
            jax.ShapeDtypeStruct((B, groups, S, nl), BF16),
            jax.ShapeDtypeStruct((B, groups, T, nl), BF16),
            jax.ShapeDtypeStruct((B, N_HEADS * LANE, T), BF16),
        ],
        compiler_params=_cparams(("arbitrary", "arbitrary")),
        name="attn_prep",
    )(x, ctx, g, sc, sh, csc, csh, tabs, win, wqb, wk, wv, qag, kvag, qng, kng)


def _attn_kernel(q_ref, k_ref, vt_ref, o_ref, s0, s1, p0, p1, m0, m1):
    s_buf, p_buf, m_buf = (s0, s1), (p0, p1), (m0, m1)

    def scores(h):
        sl = slice(h * LANE, (h + 1) * LANE)
        st = lax.dot_general(k_ref[:, sl], q_ref[:, sl], (((1,), (1,)), ((), ())),
                             preferred_element_type=F32)
        s_buf[h % 2][...] = st
        m_buf[h % 2][...] = jnp.max(st, axis=0, keepdims=True)

    scores(0)
    for h in range(HEADS_PER_STEP):
        sl = slice(h * LANE, (h + 1) * LANE)
        cur = h % 2
        if h + 1 < HEADS_PER_STEP:
            scores(h + 1)
        p_buf[cur][...] = jnp.exp2(s_buf[cur][...] - m_buf[cur][...]).astype(BF16)
        ot = jnp.dot(vt_ref[sl, :], p_buf[cur][...], preferred_element_type=F32)
        ot = ot[:HEAD, :] / ot[HEAD:HEAD + 1, :]
        if h % 2 == 0:
            even_head = ot
        else:
            pair = jnp.concatenate([even_head, ot], axis=0)
            o_ref[:, (h // 2) * LANE:(h // 2 + 1) * LANE] = pair.T.astype(BF16)


def _attention(q, k, vt):
    nl = HEADS_PER_STEP * LANE
    groups = N_HEADS // HEADS_PER_STEP
    return pl.pallas_call(
        _attn_kernel,
        grid=(B, groups, S // TQ),
        in_specs=[
            pl.BlockSpec((None, None, TQ, nl), lambda b, g, i: (b, g, i, 0)),
            pl.BlockSpec((None, None, T, nl), lambda b, g, i: (b, g, 0, 0)),
            pl.BlockSpec((None, nl, T), lambda b, g, i: (b, g, 0)),
        ],
        out_specs=pl.BlockSpec((None, None, TQ, HEADS_PER_STEP * HEAD), lambda b, g, i: (b, g, i, 0)),
        out_shape=jax.ShapeDtypeStruct((B, groups, S, HEADS_PER_STEP * HEAD), BF16),
        scratch_shapes=[pltpu.VMEM((T, TQ), F32), pltpu.VMEM((T, TQ), F32),
                        pltpu.VMEM((T, TQ), BF16), pltpu.VMEM((T, TQ), BF16),
                        pltpu.VMEM((1, TQ), F32), pltpu.VMEM((1, TQ), F32)],
        compiler_params=_cparams(("arbitrary", "arbitrary", "arbitrary")),
        name="attention",
    )(q, k, vt)


def _mlp_kernel(a_ref, wo_ref, g1_ref, x_ref, g_ref, sc_ref, sh_ref, gate_ref, w1_ref, w2_ref,
                fg_ref, o_ref, *, final, f_chunk):
    a = jnp.concatenate([a_ref[j] for j in range(a_ref.shape[0])], axis=1)
    x = x_ref[...] + g1_ref[...] * jnp.dot(a, wo_ref[...], preferred_element_type=F32)
    h = _rms_mod(x, g_ref[...], sc_ref[...], sh_ref[...]).astype(BF16)
    acc = jnp.zeros(x.shape, F32)
    for j in range(D_FF // f_chunk):
        sl = slice(j * f_chunk, (j + 1) * f_chunk)
        a = jnp.maximum(jnp.dot(h, w1_ref[:, sl], preferred_element_type=F32), 0.0)
        acc = acc + jnp.dot((a * a).astype(BF16), w2_ref[sl, :], preferred_element_type=F32)
    y = x + gate_ref[...] * acc
    if final:
        y = y * lax.rsqrt(jnp.mean(y * y, axis=-1, keepdims=True) + EPS) * fg_ref[...]
    o_ref[...] = y


def _mlp(a, wo, gate1, x, g, scale, shift, gate, w1, w2, final_g, final):
    const = lambda b, t: (0, 0)
    groups, width = a.shape[1], a.shape[3]
    return pl.pallas_call(
        functools.partial(_mlp_kernel, final=final, f_chunk=1024),
        grid=(B, S // TM),
        in_specs=[
            pl.BlockSpec((None, groups, TM, width), lambda b, t: (b, 0, t, 0)),
            pl.BlockSpec((D, D), const, pipeline_mode=pl.Buffered(1)),
            pl.BlockSpec((None, 1, D), lambda b, t: (b, 0, 0)),
            pl.BlockSpec((None, TM, D), lambda b, t: (b, t, 0)),
            pl.BlockSpec((1, D), const),
            pl.BlockSpec((None, 1, D), lambda b, t: (b, 0, 0)),
            pl.BlockSpec((None, 1, D), lambda b, t: (b, 0, 0)),
            pl.BlockSpec((None, 1, D), lambda b, t: (b, 0, 0)),
            pl.BlockSpec((D, D_FF), const, pipeline_mode=pl.Buffered(1)),
            pl.BlockSpec((D_FF, D), const, pipeline_mode=pl.Buffered(1)),
            pl.BlockSpec((1, D), const),
        ],
        out_specs=pl.BlockSpec((None, TM, D), lambda b, t: (b, t, 0)),
        out_shape=jax.ShapeDtypeStruct((B, S, D), F32),
        compiler_params=_cparams(("arbitrary", "arbitrary")),
        name="mlp",
    )(a, wo, gate1, x, g, scale, shift, gate, w1, w2, final_g)


def _build_trig(cos_dst, sin_dst, tw_ref, a0, nblk, b0_of_block):
    for i in range(nblk):
        ca = tw_ref[0, a0 + i:a0 + i + 1, :]
        sa = tw_ref[1, a0 + i:a0 + i + 1, :]
        b0 = b0_of_block(i)
        cb = tw_ref[0, b0:b0 + TWB, :]
        sb = tw_ref[1, b0:b0 + TWB, :]
        rs = slice(i * TWB, (i + 1) * TWB)
        cos_dst[rs, :] = (ca * cb - sa * sb).astype(BF16)
        sin_dst[rs, :] = (sa * cb + ca * sb).astype(BF16)


def _hyena_filter_kernel(feat_ref, w1_ref, b1_ref, w2_ref, b2_ref, w3_ref, b3_ref, fr_ref,
                         w4f0_ref, w4f1_ref, w4b0_ref, w4b1_ref, dl_ref, tw_ref,
                         kr_ref, ki_ref, co_ref, so_ref, hid_hi, hid_lo):
    @pl.when(pl.program_id(0) == 0)
    def _():
        nblk = S // TWB
        _build_trig(co_ref, so_ref, tw_ref, 0, nblk,
                    lambda i: nblk + TWB * (i // (nblk // 4)))

        hp = lax.Precision.HIGHEST
        hid = jnp.sin(fr_ref[0:1, :] * (jnp.dot(feat_ref[...], w1_ref[...], precision=hp,
                                                preferred_element_type=F32) + b1_ref[...]))
        hid = jnp.sin(fr_ref[1:2, :] * (jnp.dot(hid, w2_ref[...], precision=hp,
                                                preferred_element_type=F32) + b2_ref[...]))
        hid = jnp.sin(fr_ref[2:3, :] * (jnp.dot(hid, w3_ref[...], precision=hp,
                                                preferred_element_type=F32) + b3_ref[...]))
        hi = hid.astype(BF16)
        hid_hi[...] = hi
        hid_lo[...] = (hid - hi.astype(F32)).astype(BF16)

    def dot3(w_ref):
        w = w_ref[...]
        w_hi = w.astype(BF16)
        w_lo = (w - w_hi.astype(F32)).astype(BF16)
        return (jnp.dot(hid_hi[...], w_hi, preferred_element_type=F32)
                + jnp.dot(hid_hi[...], w_lo, preferred_element_type=F32)
                + jnp.dot(hid_lo[...], w_hi, preferred_element_type=F32))

    row = lax.broadcasted_iota(jnp.int32, (S, TC), 0)
    t_norm = row.astype(F32) / S
    window = jnp.exp(-t_norm * dl_ref[...])
    for order, (wf_ref, wb_ref) in enumerate(((w4f0_ref, w4b0_ref), (w4f1_ref, w4b1_ref))):
        hf = dot3(wf_ref) * window
        hb = dot3(wb_ref) * window
        ss = jnp.sum(hf * hf + hb * hb, axis=0, keepdims=True)
        nrm = lax.rsqrt(ss + EPS)
        hf = hf * nrm
        hb = jnp.where(row == 0, 0.0, hb * nrm)
        hsum = (hf + hb).astype(BF16)
        hdif = (hf - hb).astype(BF16)
        for r in range(4):
            rs = slice(r * (S // 4), (r + 1) * (S // 4))
            scale = (2.0 if r < 2 else 1.0) / NFFT
            kr_ref[order, rs, :] = scale * jnp.dot(co_ref[rs, :], hsum, preferred_element_type=F32)
            ki_ref[order, rs, :] = -scale * jnp.dot(so_ref[rs, :], hdif, preferred_element_type=F32)


def _hyena_filters(feats, w1, b1, w2, b2, w3, b3, freq, w4, deltas, tw):
    nc = D // TC
    const = lambda c: (0, 0)
    w4spec = lambda k: pl.BlockSpec((HY_HID, TC), lambda c, k=k: (0, k * nc + c))
    return pl.pallas_call(
        _hyena_filter_kernel,
        grid=(nc,),
        in_specs=[
            pl.BlockSpec((S, HY_EMB_PAD), const),
            pl.BlockSpec((HY_EMB_PAD, HY_HID), const),
            pl.BlockSpec((1, HY_HID), const),
            pl.BlockSpec((HY_HID, HY_HID), const),
            pl.BlockSpec((1, HY_HID), const),
            pl.BlockSpec((HY_HID, HY_HID), const),
            pl.BlockSpec((1, HY_HID), const),
            pl.BlockSpec((3, HY_HID), const),
            w4spec(0), w4spec(1), w4spec(2), w4spec(3),
            pl.BlockSpec((1, TC), lambda c: (0, c)),
            pl.BlockSpec(tw.shape, lambda c: (0, 0, 0), pipeline_mode=pl.Buffered(1)),
        ],
        out_specs=[
            pl.BlockSpec((2, S, TC), lambda c: (0, 0, c)),
            pl.BlockSpec((2, S, TC), lambda c: (0, 0, c)),
        ],
        out_shape=[
            jax.ShapeDtypeStruct((2, S, D), F32),
            jax.ShapeDtypeStruct((2, S, D), F32),
        ],
        scratch_shapes=[pltpu.VMEM((S, S), BF16), pltpu.VMEM((S, S), BF16),
                        pltpu.VMEM((S, HY_HID), BF16), pltpu.VMEM((S, HY_HID), BF16)],
        compiler_params=_cparams(("arbitrary",)),
        name="hyena_filters",
    )(feats, w1, b1, w2, b2, w3, b3, freq, w4, w4, w4, w4, deltas, tw)


def _hyena_conv_kernel(x1_ref, x2_ref, v_ref, cw1_ref, cw2_ref, cwv_ref, cb1_ref, cb2_ref,
                       cbv_ref, kr_ref, ki_ref, skip_ref, tw_ref, y_ref,
                       cos_tab, sin_tab, y_buf, g_buf, u_buf, ab_buf, p_buf, nat_buf):
    qs = S // 4
    nblk = qs // TWB
    lanes = [slice(l * LANE, (l + 1) * LANE) for l in range(TC // LANE)]

    @pl.when((pl.program_id(0) == 0) & (pl.program_id(1) == 0))
    def _():
        for fam in range(8):
            base = fam * (TW_PAD + TWB)
            _build_trig(cos_tab.at[fam], sin_tab.at[fam], tw_ref, base, nblk,
                        lambda i, base=base: base + TW_PAD)

    def phases(z_ref):
        for l, ls in enumerate(lanes):
            nat_buf[l] = z_ref[:, ls].astype(F32)
        return [jnp.concatenate([nat_buf[l, pl.ds(r, qs, stride=4), :] for l in range(len(lanes))],
                                axis=1) for r in range(4)]

    row = lax.broadcasted_iota(jnp.int32, (qs, TC), 0)

    def short_conv(z_ref, w_ref, b_ref, dst):
        z = phases(z_ref)
        before = jnp.where(row == 0, 0.0, pltpu.roll(z[3], 1, 0))
        after = jnp.where(row == qs - 1, 0.0, pltpu.roll(z[0], qs - 1, 0))
        w0, w1, w2, b = w_ref[0:1, :], w_ref[1:2, :], w_ref[2:3, :], b_ref[...]
        prev = [before, z[0], z[1], z[2]]
        nxt = [z[1], z[2], z[3], after]
        for r in range(4):
            dst[r] = b + prev[r] * w0 + z[r] * w1 + nxt[r] * w2

    def mm(tab, fam, rhs):
        return jnp.dot(tab[fam], rhs, preferred_element_type=F32)

    slabs = [slice(r * TWB, (r + 1) * TWB) for r in range(qs // TWB)]

    short_conv(v_ref, cwv_ref, cbv_ref, y_buf)
    gates = ((x1_ref, cw1_ref, cb1_ref), (x2_ref, cw2_ref, cb2_ref))
    for order, (z_ref, w_ref, b_ref) in enumerate(gates):
        u_buf[...] = y_buf[...].astype(BF16)
        for r in range(4):
            ab_buf[r] = mm(cos_tab, r, u_buf[r])
            ab_buf[4 + r] = mm(sin_tab, r, u_buf[r])
        for rs in slabs:
            a = [ab_buf[r, rs, :] for r in range(4)]
            b = [ab_buf[4 + r, rs, :] for r in range(4)]
            el_r, el_i = a[0] + a[2], -(b[0] + b[2])
            p, q = a[0] - a[2], b[0] - b[2]
            eh_r, eh_i = p + q, q - p
            ol_r, ol_i = a[1] + a[3], -(b[1] + b[3])
            p, q = a[1] - a[3], b[1] - b[3]
            oh_r, oh_i = q - p, -(p + q)
            xs = ((el_r + ol_r, el_i + ol_i), (ol_i - el_i, ol_r - el_r),
                  (eh_r + oh_r, eh_i + oh_i), (oh_i - eh_i, oh_r - eh_r))
            z = []
            for g, (xr, xi) in enumerate(xs):
                ks = slice(g * qs + rs.start, g * qs + rs.stop)
                kr, ki = kr_ref[order, ks, :], ki_ref[order, ks, :]
                z.append((xr * kr - xi * ki, xr * ki + xi * kr))
            el_r, ol_r = z[0][0] - z[1][1], z[0][0] + z[1][1]
            el_i, ol_i = z[0][1] - z[1][0], z[0][1] + z[1][0]
            eh_r, oh_r = z[2][0] - z[3][1], z[2][0] + z[3][1]
            eh_i, oh_i = z[2][1] - z[3][0], z[2][1] + z[3][0]
            p, q = eh_r - eh_i, eh_r + eh_i
            p2, q2 = -(oh_r + oh_i), oh_r - oh_i
            ah = (el_r + p, ol_r + p2, el_r - p, ol_r - p2)
            bh = (q - el_i, q2 - ol_i, -(el_i + q), -(ol_i + q2))
            for r in range(4):
                p_buf[r, rs, :] = ah[r].astype(BF16)
                p_buf[4 + r, rs, :] = bh[r].astype(BF16)
        short_conv(z_ref, w_ref, b_ref, g_buf)
        skip = skip_ref[order:order + 1, :]
        for r in range(4):
            conv = mm(cos_tab, 4 + r, p_buf[r]) + mm(sin_tab, 4 + r, p_buf[4 + r])
            y_buf[r] = g_buf[r] * (conv + y_buf[r] * skip)
    for l, ls in enumerate(lanes):
        for r in range(4):
            nat_buf[l, pl.ds(r, qs, stride=4), :] = y_buf[r, :, ls]
    for l, ls in enumerate(lanes):
        y_ref[:, ls] = nat_buf[l].astype(y_ref.dtype)


def _hyena_conv(z, conv_w, conv_b, kr, ki, skip, tw):
    nc = D // TC
    qs = S // 4
    zspec = lambda k: pl.BlockSpec((None, None, S, TC), lambda c, b, k=k: (b, k * nc + c, 0, 0))
    wspec = lambda k: pl.BlockSpec((3, TC), lambda c, b, k=k: (0, k * nc + c))
    bspec = lambda k: pl.BlockSpec((1, TC), lambda c, b, k=k: (0, k * nc + c))
    return pl.pallas_call(
        _hyena_conv_kernel,
        grid=(nc, B),
        in_specs=[
            zspec(0), zspec(1), zspec(2),
            wspec(0), wspec(1), wspec(2),
            bspec(0), bspec(1), bspec(2),
            pl.BlockSpec((2, S, TC), lambda c, b: (0, 0, c), pipeline_mode=pl.Buffered(1)),
            pl.BlockSpec((2, S, TC), lambda c, b: (0, 0, c), pipeline_mode=pl.Buffered(1)),
            pl.BlockSpec((2, TC), lambda c, b: (0, c)),
            pl.BlockSpec(tw.shape, lambda c, b: (0, 0, 0), pipeline_mode=pl.Buffered(1)),
        ],
        out_specs=pl.BlockSpec((None, None, S, TC), lambda c, b: (b, c, 0, 0)),
        out_shape=jax.ShapeDtypeStruct((B, nc, S, TC), BF16),
        scratch_shapes=[pltpu.VMEM((8, qs, qs), BF16), pltpu.VMEM((8, qs, qs), BF16),
                        pltpu.VMEM((4, qs, TC), F32), pltpu.VMEM((4, qs, TC), F32),
                        pltpu.VMEM((4, qs, TC), BF16), pltpu.VMEM((8, qs, TC), F32),
                        pltpu.VMEM((8, qs, TC), BF16), pltpu.VMEM((TC // LANE, S, LANE), F32)],
        compiler_params=_cparams(("arbitrary", "arbitrary")),
        name="hyena_conv",
    )(z, z, z, conv_w, conv_w, conv_w, conv_b, conv_b, conv_b, kr, ki, skip, tw)


def _pad_heads(w, heads, d, front=0):
    k = w.shape[0]
    w = w.reshape(k, heads, d)
    w = jnp.pad(w, ((0, 0), (0, 0), (front, LANE - d - front)))
    return w.reshape(k, heads * LANE)


def _rope_tables():
    pos = jnp.arange(S, dtype=jnp.int32)
    rowf = (pos // GRID_W).astype(F32)
    colf = (pos % GRID_W).astype(F32)

    def pattern(base, half):
        n = half // 2
        inv = ROPE_THETA ** (-jnp.arange(n, dtype=F32) / n)
        cos_cols, lo_cols, hi_cols = [], [], []
        for p in (rowf, colf):
            ang = p[:, None] * inv[None]
            c, s = jnp.cos(ang), jnp.sin(ang)
            zero = jnp.zeros_like(s)
            cos_cols += [c, c]
            lo_cols += [-s, zero]
            hi_cols += [zero, s]
        width = 2 * half

        def place(cols, fill):
            body = jnp.concatenate(cols, axis=1)
            return jnp.concatenate([jnp.full((S, base), fill, F32), body,
                                    jnp.full((S, LANE - base - width), fill, F32)], axis=1)
        return place(cos_cols, 1.0), place(lo_cols, 0.0), place(hi_cols, 0.0)

    tabs = jnp.stack(pattern(HEAD, MLA_ROPE // 2) + pattern(0, HEAD // 2))
    ident = jnp.stack([jnp.ones((CTX, LANE), F32), jnp.zeros((CTX, LANE), F32),
                       jnp.zeros((CTX, LANE), F32)] * 2)
    return jnp.concatenate([ident, tabs], axis=1)


def _trig_factors(rows):
    ang = (rows % (4 * NFFT)).astype(F32) * (2.0 * math.pi / (4 * NFFT))
    return jnp.stack([jnp.cos(ang), jnp.sin(ang)])


def _conv_trig_factors():
    qs = S // 4
    i = jnp.arange(TW_PAD, dtype=jnp.int32)[:, None]
    j = jnp.arange(TWB, dtype=jnp.int32)[:, None]
    c = jnp.arange(qs, dtype=jnp.int32)[None, :]
    fams = []
    for r in range(4):
        odd = 2 * r + 1
        fams += [2 * TWB * i * (8 * c + odd), (2 * j + 1) * (8 * c + odd)]
    for r in range(4):
        odd = 2 * r + 1
        fams += [8 * TWB * i * (2 * c + 1), (8 * j + odd) * (2 * c + 1)]
    return _trig_factors(jnp.concatenate(fams, axis=0))


def _filter_trig_factors():
    nblk = S // TWB
    per_group = nblk // 4
    i = jnp.arange(nblk, dtype=jnp.int32)[:, None]
    j = jnp.arange(TWB, dtype=jnp.int32)[:, None]
    n = jnp.arange(S, dtype=jnp.int32)[None, :]
    grp = i // per_group
    blk = 2 * TWB * (i % per_group) * 2 * n
    blk = jnp.where((grp == 0) | (grp == 3), blk, -blk)
    in_block = [(2 * j + 1) * 2 * n, (2 * S - 1 - 2 * j) * 2 * n,
                (S - 1 - 2 * j) * 2 * n, (S + 1 + 2 * j) * 2 * n]
    return _trig_factors(jnp.concatenate([blk] + in_block, axis=0))


def _hyena_features():
    t = jnp.arange(S, dtype=F32)
    t_norm = t / S
    w = 2.0 * math.pi * t / S
    bands = jnp.linspace(1e-4, HY_BANDS - 1, HY_BANDS, dtype=F32)
    fw = w[:, None] * bands[None]
    feats = jnp.concatenate([t_norm[:, None], jnp.cos(fw), -jnp.sin(fw)], axis=-1)
    feats = jnp.pad(feats, ((0, 0), (0, HY_EMB_PAD - HY_EMB)))
    max_decay = math.log(1e-2) / 0.3
    min_decay = math.log(1e-2) / 1.5
    deltas = jnp.abs(jnp.linspace(min_decay, max_decay, D, dtype=F32))
    return feats, deltas.reshape(1, D)


def kernel(x, c, ctx, c_ctx, w_mod, b_mod, norm1_g, norm2_g, mlp_w1, mlp_w2, a_w_in, a_q_a_g, a_w_q_b, a_kv_a_g, a_w_kv_b, a_q_norm_g, a_k_norm_g, a_w_out, h_w_in, h_conv_w, h_conv_b, h_f_w1, h_f_b1, h_f_w2, h_f_b2, h_f_w3, h_f_b3, h_f_freq, h_f_w4, h_skip, h_w_out, final_g):
    assert x.shape == (B, S, D) and ctx.shape == (B, CTX, D) and w_mod.shape[0] == 2

    c_rows = jnp.concatenate([c, c_ctx[None], jnp.zeros((MOD_ROWS - B - 1, D), F32)], axis=0)
    mods = _mods(c_rows, w_mod, b_mod)

    def lat_mod(i, k):
        return mods[i, :B, k * D:(k + 1) * D].reshape(B, 1, D)

    def ctx_mod(i, k):
        return mods[i, B:B + 1, k * D:(k + 1) * D]

    w_in = a_w_in[0]
    o_kr = Q_LORA + KV_LORA
    o_gq = o_kr + MLA_ROPE
    o_gk = o_gq + GQA_HEADS * HEAD
    o_gv = o_gk + GQA_KV * HEAD
    win = jnp.concatenate([
        w_in[:, :o_kr],
        _pad_heads(w_in[:, o_kr:o_gq], 1, MLA_ROPE, front=HEAD),
        _pad_heads(w_in[:, o_gq:o_gk], GQA_HEADS, HEAD),
        _pad_heads(w_in[:, o_gk:o_gv], GQA_KV, HEAD),
        _pad_heads(w_in[:, o_gv:], GQA_KV, HEAD),
    ], axis=1).astype(BF16)
    wqb = _pad_heads(a_w_q_b[0], MLA_HEADS, HEAD + MLA_ROPE).astype(BF16)
    wkv = a_w_kv_b[0].reshape(KV_LORA, MLA_HEADS, 2 * HEAD)
    wk = _pad_heads(wkv[:, :, :HEAD].reshape(KV_LORA, MLA_HEADS * HEAD), MLA_HEADS, HEAD).astype(BF16)
    wv = _pad_heads(wkv[:, :, HEAD:].reshape(KV_LORA, MLA_HEADS * HEAD), MLA_HEADS, HEAD).astype(BF16)
    qng = jnp.pad(a_q_norm_g[0], (0, LANE - HEAD)).reshape(1, LANE)
    kng = jnp.pad(a_k_norm_g[0], (0, LANE - HEAD)).reshape(1, LANE)

    q, k, vt = _attn_prep(x, ctx, norm1_g[0:1], lat_mod(0, 1), lat_mod(0, 0),
                          ctx_mod(0, 1), ctx_mod(0, 0), _rope_tables(), win, wqb, wk, wv,
                          a_q_a_g[0:1], a_kv_a_g[0:1], qng, kng)
    o = _attention(q, k, vt)
    x = _mlp(o, a_w_out[0].astype(BF16), lat_mod(0, 2), x,
             norm2_g[0:1], lat_mod(0, 4), lat_mod(0, 3), lat_mod(0, 5),
             mlp_w1[0].astype(BF16), mlp_w2[0].astype(BF16), final_g.reshape(1, D), final=False)

    feats, deltas = _hyena_features()
    kr, ki = _hyena_filters(
        feats, jnp.pad(h_f_w1[0], ((0, HY_EMB_PAD - HY_EMB), (0, 0))), h_f_b1[0:1],
        h_f_w2[0], h_f_b2[0:1], h_f_w3[0], h_f_b3[0:1], h_f_freq[0], h_f_w4[0], deltas,
        _filter_trig_factors())
    z = _modmm(x, norm1_g[1:2], lat_mod(1, 1), lat_mod(1, 0), h_w_in[0].astype(BF16))
    y = _hyena_conv(z, h_conv_w[0], h_conv_b[0:1], kr, ki, h_skip[0], _conv_trig_factors())
    x = _mlp(y, h_w_out[0].astype(BF16), lat_mod(1, 2), x,
             norm2_g[1:2], lat_mod(1, 4), lat_mod(1, 3), lat_mod(1, 5),
             mlp_w1[1].astype(BF16), mlp_w2[1].astype(BF16), final_g.reshape(1, D), final=True)
    return x
```

```python
import functools
import math

import jax
import jax.numpy as jnp
import numpy as np
from jax import lax
from jax.experimental import pallas as pl
from jax.experimental.pallas import tpu as pltpu

F32 = jnp.float32
BF16 = jnp.bfloat16

D = 1024
B = 16
S = 2048
CTX = 256
T = CTX + S
GRID_W = 64
D_FF = 4 * D
N_MOD = 6
HEAD = 64
MLA_HEADS = 8
MLA_ROPE = 32
Q_LORA = 384
KV_LORA = 256
GQA_HEADS = 8
GQA_KV = 2
N_HEADS = MLA_HEADS + GQA_HEADS
LANE = 128
LOG2E = 1.4426950408889634
ROPE_THETA = 10000.0
EPS = 1e-6
HY_BANDS = 8
HY_EMB = 1 + 2 * HY_BANDS
HY_EMB_PAD = 32
HY_HID = 64
NFFT = 2 * S

VMEM_LIMIT = 60 * 1024 * 1024

MOD_ROWS = 24
TM = 512
TP = 256
TQ = 512
HEADS_PER_STEP = 8
TC = 256
TWB = 128
TW_PAD = 8
SLAB = 32


def _cparams(sem):
    return pltpu.CompilerParams(dimension_semantics=sem, vmem_limit_bytes=VMEM_LIMIT)


def _rms_mod(x, g, scale, shift):
    ms = jnp.mean(x * x, axis=-1, keepdims=True)
    return x * lax.rsqrt(ms + EPS) * (g * (1.0 + scale)) + shift


def _mods_kernel(c_ref, w_ref, b_ref, o_ref):
    c = c_ref[...]
    s = c * (1.0 / (1.0 + jnp.exp(-c)))
    o_ref[...] = jnp.dot(s.astype(BF16), w_ref[...].astype(BF16),
                         preferred_element_type=F32) + b_ref[...]


def _mods(c_rows, w_mod, b_mod):
    depth = w_mod.shape[0]
    tn = 1024
    return pl.pallas_call(
        _mods_kernel,
        grid=(depth, N_MOD * D // tn),
        in_specs=[
            pl.BlockSpec((MOD_ROWS, D), lambda i, j: (0, 0)),
            pl.BlockSpec((None, D, tn), lambda i, j: (i, 0, j)),
            pl.BlockSpec((None, 1, tn), lambda i, j: (i, 0, j)),
        ],
        out_specs=pl.BlockSpec((None, MOD_ROWS, tn), lambda i, j: (i, 0, j)),
        out_shape=jax.ShapeDtypeStruct((depth, MOD_ROWS, N_MOD * D), F32),
        compiler_params=_cparams(("arbitrary", "arbitrary")),
        name="mods",
    )(c_rows, w_mod, b_mod.reshape(depth, 1, N_MOD * D))


def _modmm_kernel(x_ref, g_ref, sc_ref, sh_ref, w_ref, o_ref, *, n_chunk):
    h = _rms_mod(x_ref[...], g_ref[...], sc_ref[...], sh_ref[...]).astype(BF16)
    n = w_ref.shape[1]
    per = n_chunk // TC
    for j in range(n // n_chunk):
        sl = slice(j * n_chunk, (j + 1) * n_chunk)
        y = jnp.dot(h, w_ref[:, sl], preferred_element_type=F32).astype(o_ref.dtype)
        for c in range(per):
            o_ref[j * per + c] = y[:, c * TC:(c + 1) * TC]


def _modmm(x, g, scale, shift, w):
    n = w.shape[1]
    return pl.pallas_call(
        functools.partial(_modmm_kernel, n_chunk=1024),
        grid=(B, S // TM),
        in_specs=[
            pl.BlockSpec((None, TM, D), lambda b, t: (b, t, 0)),
            pl.BlockSpec((1, D), lambda b, t: (0, 0)),
            pl.BlockSpec((None, 1, D), lambda b, t: (b, 0, 0)),
            pl.BlockSpec((None, 1, D), lambda b, t: (b, 0, 0)),
            pl.BlockSpec((D, n), lambda b, t: (0, 0)),
        ],
        out_specs=pl.BlockSpec((None, n // TC, TM, TC), lambda b, t: (b, 0, t, 0)),
        out_shape=jax.ShapeDtypeStruct((B, n // TC, S, TC), BF16),
        compiler_params=_cparams(("arbitrary", "arbitrary")),
        name="modmm",
    )(x, g, scale, shift, w)


def _rope_tiles(xs, cos, sin_lo, sin_hi, n):
    lo = [pltpu.roll(x, LANE - n, 1) for x in xs]
    hi = [pltpu.roll(x, n, 1) for x in xs]
    return [x * cos + a * sin_lo + b * sin_hi for x, a, b in zip(xs, lo, hi)]


def _attn_prep_kernel(x_ref, ctx_ref, g_ref, sc_ref, sh_ref, csc_ref, csh_ref, tab_ref,
                      win_ref, wqb_ref, wk_ref, wv_ref, qag_ref, kvag_ref, qng_ref, kng_ref,
                      q_ref, k_ref, vt_ref):
    t = pl.program_id(1)
    is_ctx = t == 0
    src = jnp.where(is_ctx, ctx_ref[...], x_ref[...])
    scale = jnp.where(is_ctx, csc_ref[...], sc_ref[...])
    shift = jnp.where(is_ctx, csh_ref[...], sh_ref[...])
    h = _rms_mod(src, g_ref[...], scale, shift).astype(BF16)
    z = jnp.dot(h, win_ref[...], preferred_element_type=F32)

    cos_m, slo_m, shi_m = tab_ref[0], tab_ref[1], tab_ref[2]
    cos_g, slo_g, shi_g = tab_ref[3], tab_ref[4], tab_ref[5]
    ones_col = (lax.broadcasted_iota(jnp.int32, (1, LANE), 1) == HEAD).astype(F32)

    o_kr = Q_LORA + KV_LORA
    o_gq = o_kr + LANE
    o_gk = o_gq + GQA_HEADS * LANE
    o_gv = o_gk + GQA_KV * LANE

    def rms(v, g):
        return v * lax.rsqrt(jnp.mean(v * v, axis=-1, keepdims=True) + EPS) * g

    def tile(v, off, i):
        return v[:, off + i * LANE:off + (i + 1) * LANE]

    s_mla = LOG2E / math.sqrt(HEAD + MLA_ROPE)
    s_gqa = LOG2E / math.sqrt(HEAD)

    cq = rms(z[:, :Q_LORA], qag_ref[...] * s_mla).astype(BF16)
    ckv = rms(z[:, Q_LORA:o_kr], kvag_ref[...]).astype(BF16)
    q = jnp.dot(cq, wqb_ref[...], preferred_element_type=F32)
    kn = jnp.dot(ckv, wk_ref[...], preferred_element_type=F32)
    vm = jnp.dot(ckv, wv_ref[...], preferred_element_type=F32)

    gqk = [tile(z, o_gq, i) for i in range(GQA_HEADS)] + [tile(z, o_gk, j) for j in range(GQA_KV)]
    gains = [qng_ref[...] * s_gqa] * GQA_HEADS + [kng_ref[...]] * GQA_KV
    ssq = [jnp.sum(v * v, axis=-1, keepdims=True) for v in gqk]
    gqk = [v * lax.rsqrt(s * (1.0 / HEAD) + EPS) * g for v, s, g in zip(gqk, ssq, gains)]
    gqk = _rope_tiles(gqk, cos_g, slo_g, shi_g, HEAD // 4)

    mla = [tile(q, 0, i) for i in range(MLA_HEADS)] + [z[:, o_kr:o_gq]]
    mla = _rope_tiles(mla, cos_m, slo_m, shi_m, MLA_ROPE // 4)
    kr = mla[MLA_HEADS]

    def put(ref, head, val):
        grp, hh = divmod(head, HEADS_PER_STEP)
        ref[grp, :, hh * LANE:(hh + 1) * LANE] = val

    for hh in range(MLA_HEADS):
        sl = slice(hh * LANE, (hh + 1) * LANE)
        put(q_ref, hh, mla[hh].astype(BF16))
        put(k_ref, hh, (kn[:, sl] + kr).astype(BF16))
        vt_ref[sl, :] = (vm[:, sl] + ones_col).T.astype(BF16)
    for hh in range(GQA_HEADS):
        put(q_ref, MLA_HEADS + hh, gqk[hh].astype(BF16))
    rep = GQA_HEADS // GQA_KV
    for j in range(GQA_KV):
        gk = gqk[GQA_HEADS + j].astype(BF16)
        gvt = (tile(z, o_gv, j) + ones_col).T.astype(BF16)
        for r in range(rep):
            hh = MLA_HEADS + j * rep + r
            put(k_ref, hh, gk)
            vt_ref[hh * LANE:(hh + 1) * LANE, :] = gvt


def _attn_prep(x, ctx, g, sc, sh, csc, csh, tabs, win, wqb, wk, wv, qag, kvag, qng, kng):
    nw = win.shape[1]
    lat = lambda b, t: (b, jnp.maximum(t - 1, 0), 0)
    groups = N_HEADS // HEADS_PER_STEP
    nl = HEADS_PER_STEP * LANE
    full2 = lambda b, t: (0, 0)
    return pl.pallas_call(
        _attn_prep_kernel,
        grid=(B, T // TP),
        in_specs=[
            pl.BlockSpec((None, TP, D), lat),
            pl.BlockSpec((None, CTX, D), lambda b, t: (b, 0, 0)),
            pl.BlockSpec((1, D), full2),
            pl.BlockSpec((None, 1, D), lambda b, t: (b, 0, 0)),
            pl.BlockSpec((None, 1, D), lambda b, t: (b, 0, 0)),
            pl.BlockSpec((1, D), full2),
            pl.BlockSpec((1, D), full2),
            pl.BlockSpec((6, TP, LANE), lambda b, t: (0, t, 0)),
            pl.BlockSpec((D, nw), full2),
            pl.BlockSpec((Q_LORA, MLA_HEADS * LANE), full2),
            pl.BlockSpec((KV_LORA, MLA_HEADS * LANE), full2),
            pl.BlockSpec((KV_LORA, MLA_HEADS * LANE), full2),
            pl.BlockSpec((1, Q_LORA), full2),
            pl.BlockSpec((1, KV_LORA), full2),
            pl.BlockSpec((1, LANE), full2),
            pl.BlockSpec((1, LANE), full2),
        ],
        out_specs=[
            pl.BlockSpec((None, groups, TP, nl), lambda b, t: (b, 0, jnp.maximum(t - 1, 0), 0)),
            pl.BlockSpec((None, groups, TP, nl), lambda b, t: (b, 0, t, 0)),
            pl.BlockSpec((None, N_HEADS * LANE, TP), lambda b, t: (b, 0, t)),
        ],
        out_shape=[
            jax.ShapeDtypeStruct((B, groups, S, nl), BF16),
            jax.ShapeDtypeStruct((B, groups, T, nl), BF16),
            jax.ShapeDtypeStruct((B, N_HEADS * LANE, T), BF16),
        ],
        compiler_params=_cparams(("arbitrary", "arbitrary")),
        name="attn_prep",
    )(x, ctx, g, sc, sh, csc, csh, tabs, win, wqb, wk, wv, qag, kvag, qng, kng)


def _attn_kernel(q_ref, k_ref, vt_ref, o_ref, s0, s1, p0, p1, m0, m1):
    s_buf, p_buf, m_buf = (s0, s1), (p0, p1), (m0, m1)

    def scores(h):
        sl = slice(h * LANE, (h + 1) * LANE)
        st = lax.dot_general(k_ref[:, sl], q_ref[:, sl], (((1,), (1,)), ((), ())),
                             preferred_element_type=F32)
        s_buf[h % 2][...] = st
        m_buf[h % 2][...] = jnp.max(st, axis=0, keepdims=True)

    scores(0)
    for h in range(HEADS_PER_STEP):
        sl = slice(h * LANE, (h + 1) * LANE)
        cur = h % 2
        if h + 1 < HEADS_PER_STEP:
            scores(h + 1)
        p_buf[cur][...] = jnp.exp2(s_buf[cur][...] - m_buf[cur][...]).astype(BF16)
        ot = jnp.dot(vt_ref[sl, :], p_buf[cur][...], preferred_element_type=F32)
        ot = ot[:HEAD, :] / ot[HEAD:HEAD + 1, :]
        if h % 2 == 0:
            even_head = ot
        else:
            pair = jnp.concatenate([even_head, ot], axis=0)
            o_ref[:, (h // 2) * LANE:(h // 2 + 1) * LANE] = pair.T.astype(BF16)


def _attention(q, k, vt):
    nl = HEADS_PER_STEP * LANE
    groups = N_HEADS // HEADS_PER_STEP
    return pl.pallas_call(
        _attn_kernel,
        grid=(B, groups, S // TQ),
        in_specs=[
            pl.BlockSpec((None, None, TQ, nl), lambda b, g, i: (b, g, i, 0)),
            pl.BlockSpec((None, None, T, nl), lambda b, g, i: (b, g, 0, 0)),
            pl.BlockSpec((None, nl, T), lambda b, g, i: (b, g, 0)),
        ],
        out_specs=pl.BlockSpec((None, None, TQ, HEADS_PER_STEP * HEAD), lambda b, g, i: (b, g, i, 0)),
        out_shape=jax.ShapeDtypeStruct((B, groups, S, HEADS_PER_STEP * HEAD), BF16),
        scratch_shapes=[pltpu.VMEM((T, TQ), F32), pltpu.VMEM((T, TQ), F32),
                        pltpu.VMEM((T, TQ), BF16), pltpu.VMEM((T, TQ), BF16),
                        pltpu.VMEM((1, TQ), F32), pltpu.VMEM((1, TQ), F32)],
        compiler_params=_cparams(("arbitrary", "arbitrary", "arbitrary")),
        name="attention",
    )(q, k, vt)


def _mlp_kernel(a_ref, wo_ref, g1_ref, x_ref, g_ref, sc_ref, sh_ref, gate_ref, w1_ref, w2_ref,
                fg_ref, o_ref, *, final, f_chunk):
    a = jnp.concatenate([a_ref[j] for j in range(a_ref.shape[0])], axis=1)
    x = x_ref[...] + g1_ref[...] * jnp.dot(a, wo_ref[...], preferred_element_type=F32)
    h = _rms_mod(x, g_ref[...], sc_ref[...], sh_ref[...]).astype(BF16)
    acc = jnp.zeros(x.shape, F32)
    for j in range(D_FF // f_chunk):
        sl = slice(j * f_chunk, (j + 1) * f_chunk)
        a = jnp.maximum(jnp.dot(h, w1_ref[:, sl], preferred_element_type=F32), 0.0)
        acc = acc + jnp.dot((a * a).astype(BF16), w2_ref[sl, :], preferred_element_type=F32)
    y = x + gate_ref[...] * acc
    if final:
        y = y * lax.rsqrt(jnp.mean(y * y, axis=-1, keepdims=True) + EPS) * fg_ref[...]
    o_ref[...] = y


def _mlp(a, wo, gate1, x, g, scale, shift, gate, w1, w2, final_g, final):
    const = lambda b, t: (0, 0)
    groups, width = a.shape[1], a.shape[3]
    return pl.pallas_call(
        functools.partial(_mlp_kernel, final=final, f_chunk=1024),
        grid=(B, S // TM),
        in_specs=[
            pl.BlockSpec((None, groups, TM, width), lambda b, t: (b, 0, t, 0)),
            pl.BlockSpec((D, D), const, pipeline_mode=pl.Buffered(1)),
            pl.BlockSpec((None, 1, D), lambda b, t: (b, 0, 0)),
            pl.BlockSpec((None, TM, D), lambda b, t: (b, t, 0)),
            pl.BlockSpec((1, D), const),
            pl.BlockSpec((None, 1, D), lambda b, t: (b, 0, 0)),
            pl.BlockSpec((None, 1, D), lambda b, t: (b, 0, 0)),
            pl.BlockSpec((None, 1, D), lambda b, t: (b, 0, 0)),
            pl.BlockSpec((D, D_FF), const, pipeline_mode=pl.Buffered(1)),
            pl.BlockSpec((D_FF, D), const, pipeline_mode=pl.Buffered(1)),
            pl.BlockSpec((1, D), const),
        ],
        out_specs=pl.BlockSpec((None, TM, D), lambda b, t: (b, t, 0)),
        out_shape=jax.ShapeDtypeStruct((B, S, D), F32),
        compiler_params=_cparams(("arbitrary", "arbitrary")),
        name="mlp",
    )(a, wo, gate1, x, g, scale, shift, gate, w1, w2, final_g)


def _build_trig(cos_dst, sin_dst, tw_ref, a0, nblk, b0_of_block):
    for i in range(nblk):
        ca = tw_ref[0, a0 + i:a0 + i + 1, :]
        sa = tw_ref[1, a0 + i:a0 + i + 1, :]
        b0 = b0_of_block(i)
        cb = tw_ref[0, b0:b0 + TWB, :]
        sb = tw_ref[1, b0:b0 + TWB, :]
        rs = slice(i * TWB, (i + 1) * TWB)
        cos_dst[rs, :] = (ca * cb - sa * sb).astype(BF16)
        sin_dst[rs, :] = (sa * cb + ca * sb).astype(BF16)


def _hyena_filter_kernel(feat_ref, w1_ref, b1_ref, w2_ref, b2_ref, w3_ref, b3_ref, fr_ref,
                         w4f0_ref, w4f1_ref, w4b0_ref, w4b1_ref, dl_ref, tw_ref,
                         kr_ref, ki_ref, co_ref, so_ref, hid_hi, hid_lo):
    @pl.when(pl.program_id(0) == 0)
    def _():
        nblk = S // TWB
        _build_trig(co_ref, so_ref, tw_ref, 0, nblk,
                    lambda i: nblk + TWB * (i // (nblk // 4)))

        hp = lax.Precision.HIGHEST
        hid = jnp.sin(fr_ref[0:1, :] * (jnp.dot(feat_ref[...], w1_ref[...], precision=hp,
                                                preferred_element_type=F32) + b1_ref[...]))
        hid = jnp.sin(fr_ref[1:2, :] * (jnp.dot(hid, w2_ref[...], precision=hp,
                                                preferred_element_type=F32) + b2_ref[...]))
        hid = jnp.sin(fr_ref[2:3, :] * (jnp.dot(hid, w3_ref[...], precision=hp,
                                                preferred_element_type=F32) + b3_ref[...]))
        hi = hid.astype(BF16)
        hid_hi[...] = hi
        hid_lo[...] = (hid - hi.astype(F32)).astype(BF16)

    def dot3(w_ref):
        w = w_ref[...]
        w_hi = w.astype(BF16)
        w_lo = (w - w_hi.astype(F32)).astype(BF16)
        return (jnp.dot(hid_hi[...], w_hi, preferred_element_type=F32)
                + jnp.dot(hid_hi[...], w_lo, preferred_element_type=F32)
                + jnp.dot(hid_lo[...], w_hi, preferred_element_type=F32))

    row = lax.broadcasted_iota(jnp.int32, (S, TC), 0)
    t_norm = row.astype(F32) / S
    window = jnp.exp(-t_norm * dl_ref[...])
    for order, (wf_ref, wb_ref) in enumerate(((w4f0_ref, w4b0_ref), (w4f1_ref, w4b1_ref))):
        hf = dot3(wf_ref) * window
        hb = dot3(wb_ref) * window
        ss = jnp.sum(hf * hf + hb * hb, axis=0, keepdims=True)
        nrm = lax.rsqrt(ss + EPS)
        hf = hf * nrm
        hb = jnp.where(row == 0, 0.0, hb * nrm)
        hsum = (hf + hb).astype(BF16)
        hdif = (hf - hb).astype(BF16)
        for r in range(4):
            rs = slice(r * (S // 4), (r + 1) * (S // 4))
            scale = (2.0 if r < 2 else 1.0) / NFFT
            kr_ref[order, rs, :] = scale * jnp.dot(co_ref[rs, :], hsum, preferred_element_type=F32)
            ki_ref[order, rs, :] = -scale * jnp.dot(so_ref[rs, :], hdif, preferred_element_type=F32)


def _hyena_filters(feats, w1, b1, w2, b2, w3, b3, freq, w4, deltas, tw):
    nc = D // TC
    const = lambda c: (0, 0)
    w4spec = lambda k: pl.BlockSpec((HY_HID, TC), lambda c, k=k: (0, k * nc + c))
    return pl.pallas_call(
        _hyena_filter_kernel,
        grid=(nc,),
        in_specs=[
            pl.BlockSpec((S, HY_EMB_PAD), const),
            pl.BlockSpec((HY_EMB_PAD, HY_HID), const),
            pl.BlockSpec((1, HY_HID), const),
            pl.BlockSpec((HY_HID, HY_HID), const),
            pl.BlockSpec((1, HY_HID), const),
            pl.BlockSpec((HY_HID, HY_HID), const),
            pl.BlockSpec((1, HY_HID), const),
            pl.BlockSpec((3, HY_HID), const),
            w4spec(0), w4spec(1), w4spec(2), w4spec(3),
            pl.BlockSpec((1, TC), lambda c: (0, c)),
            pl.BlockSpec(tw.shape, lambda c: (0, 0, 0), pipeline_mode=pl.Buffered(1)),
        ],
        out_specs=[
            pl.BlockSpec((2, S, TC), lambda c: (0, 0, c)),
            pl.BlockSpec((2, S, TC), lambda c: (0, 0, c)),
        ],
        out_shape=[
            jax.ShapeDtypeStruct((2, S, D), F32),
            jax.ShapeDtypeStruct((2, S, D), F32),
        ],
        scratch_shapes=[pltpu.VMEM((S, S), BF16), pltpu.VMEM((S, S), BF16),
                        pltpu.VMEM((S, HY_HID), BF16), pltpu.VMEM((S, HY_HID), BF16)],
        compiler_params=_cparams(("arbitrary",)),
        name="hyena_filters",
    )(feats, w1, b1, w2, b2, w3, b3, freq, w4, w4, w4, w4, deltas, tw)


def _hyena_conv_kernel(x1_ref, x2_ref, v_ref, cw1_ref, cw2_ref, cwv_ref, cb1_ref, cb2_ref,
                       cbv_ref, kr_ref, ki_ref, skip_ref, tw_ref, y_ref,
                       cos_tab, sin_tab, y_buf, g_buf, u_buf, ab_buf, p_buf, nat_buf):
    qs = S // 4
    nblk = qs // TWB
    lanes = [slice(l * LANE, (l + 1) * LANE) for l in range(TC // LANE)]

    @pl.when((pl.program_id(0) == 0) & (pl.program_id(1) == 0))
    def _():
        for fam in range(8):
            base = fam * (TW_PAD + TWB)
            _build_trig(cos_tab.at[fam], sin_tab.at[fam], tw_ref, base, nblk,
                        lambda i, base=base: base + TW_PAD)

    def phases(z_ref):
        for l, ls in enumerate(lanes):
            nat_buf[l] = z_ref[:, ls].astype(F32)
        return [jnp.concatenate([nat_buf[l, pl.ds(r, qs, stride=4), :] for l in range(len(lanes))],
                                axis=1) for r in range(4)]

    row = lax.broadcasted_iota(jnp.int32, (qs, TC), 0)

    def short_conv(z_ref, w_ref, b_ref, dst):
        z = phases(z_ref)
        before = jnp.where(row == 0, 0.0, pltpu.roll(z[3], 1, 0))
        after = jnp.where(row == qs - 1, 0.0, pltpu.roll(z[0], qs - 1, 0))
        w0, w1, w2, b = w_ref[0:1, :], w_ref[1:2, :], w_ref[2:3, :], b_ref[...]
        prev = [before, z[0], z[1], z[2]]
        nxt = [z[1], z[2], z[3], after]
        for r in range(4):
            dst[r] = b + prev[r] * w0 + z[r] * w1 + nxt[r] * w2

    def mm(tab, fam, rhs):
        return jnp.dot(tab[fam], rhs, preferred_element_type=F32)


    short_conv(v_ref, cwv_ref, cbv_ref, y_buf)
    gates = ((x1_ref, cw1_ref, cb1_ref), (x2_ref, cw2_ref, cb2_ref))
    for order, (z_ref, w_ref, b_ref) in enumerate(gates):
        u_buf[...] = y_buf[...].astype(BF16)
        for r in range(4):
            ab_buf[r] = mm(cos_tab, r, u_buf[r])
            ab_buf[4 + r] = mm(sin_tab, r, u_buf[r])
        for i in range(qs // SLAB):
            start = i * SLAB
            rs = slice(start, start + SLAB)
            a = [ab_buf[r, rs, :] for r in range(4)]
            b = [ab_buf[4 + r, rs, :] for r in range(4)]
            el_r, el_i = a[0] + a[2], -(b[0] + b[2])
            p, q = a[0] - a[2], b[0] - b[2]
            eh_r, eh_i = p + q, q - p
            ol_r, ol_i = a[1] + a[3], -(b[1] + b[3])
            p, q = a[1] - a[3], b[1] - b[3]
            oh_r, oh_i = q - p, -(p + q)
            xs = ((el_r + ol_r, el_i + ol_i), (ol_i - el_i, ol_r - el_r),
                  (eh_r + oh_r, eh_i + oh_i), (oh_i - eh_i, oh_r - eh_r))
            z = []
            for g, (xr, xi) in enumerate(xs):
                ks = slice(g * qs + start, g * qs + start + SLAB)
                kr, ki = kr_ref[order, ks, :], ki_ref[order, ks, :]
                z.append((xr * kr - xi * ki, xr * ki + xi * kr))
            el_r, ol_r = z[0][0] - z[1][1], z[0][0] + z[1][1]
            el_i, ol_i = z[0][1] - z[1][0], z[0][1] + z[1][0]
            eh_r, oh_r = z[2][0] - z[3][1], z[2][0] + z[3][1]
            eh_i, oh_i = z[2][1] - z[3][0], z[2][1] + z[3][0]
            p, q = eh_r - eh_i, eh_r + eh_i
            p2, q2 = -(oh_r + oh_i), oh_r - oh_i
            ah = (el_r + p, ol_r + p2, el_r - p, ol_r - p2)
            bh = (q - el_i, q2 - ol_i, -(el_i + q), -(ol_i + q2))
            for r in range(4):
                p_buf[r, rs, :] = ah[r].astype(BF16)
                p_buf[4 + r, rs, :] = bh[r].astype(BF16)
        short_conv(z_ref, w_ref, b_ref, g_buf)
        skip = skip_ref[order:order + 1, :]
        for r in range(4):
            conv = mm(cos_tab, 4 + r, p_buf[r]) + mm(sin_tab, 4 + r, p_buf[4 + r])
            y_buf[r] = g_buf[r] * (conv + y_buf[r] * skip)
    for l, ls in enumerate(lanes):
        for r in range(4):
            nat_buf[l, pl.ds(r, qs, stride=4), :] = y_buf[r, :, ls]
    for l, ls in enumerate(lanes):
        y_ref[:, ls] = nat_buf[l].astype(y_ref.dtype)


def _hyena_conv(z, conv_w, conv_b, kr, ki, skip, tw):
    nc = D // TC
    qs = S // 4
    zspec = lambda k: pl.BlockSpec((None, None, S, TC), lambda c, b, k=k: (b, k * nc + c, 0, 0))
    wspec = lambda k: pl.BlockSpec((3, TC), lambda c, b, k=k: (0, k * nc + c))
    bspec = lambda k: pl.BlockSpec((1, TC), lambda c, b, k=k: (0, k * nc + c))
    return pl.pallas_call(
        _hyena_conv_kernel,
        grid=(nc, B),
        in_specs=[
            zspec(0), zspec(1), zspec(2),
            wspec(0), wspec(1), wspec(2),
            bspec(0), bspec(1), bspec(2),
            pl.BlockSpec((2, S, TC), lambda c, b: (0, 0, c), pipeline_mode=pl.Buffered(1)),
            pl.BlockSpec((2, S, TC), lambda c, b: (0, 0, c), pipeline_mode=pl.Buffered(1)),
            pl.BlockSpec((2, TC), lambda c, b: (0, c)),
            pl.BlockSpec(tw.shape, lambda c, b: (0, 0, 0), pipeline_mode=pl.Buffered(1)),
        ],
        out_specs=pl.BlockSpec((None, None, S, TC), lambda c, b: (b, c, 0, 0)),
        out_shape=jax.ShapeDtypeStruct((B, nc, S, TC), BF16),
        scratch_shapes=[pltpu.VMEM((8, qs, qs), BF16), pltpu.VMEM((8, qs, qs), BF16),
                        pltpu.VMEM((4, qs, TC), F32), pltpu.VMEM((4, qs, TC), F32),
                        pltpu.VMEM((4, qs, TC), BF16), pltpu.VMEM((8, qs, TC), F32),
                        pltpu.VMEM((8, qs, TC), BF16), pltpu.VMEM((TC // LANE, S, LANE), F32)],
        compiler_params=_cparams(("arbitrary", "arbitrary")),
        name="hyena_conv",
    )(z, z, z, conv_w, conv_w, conv_w, conv_b, conv_b, conv_b, kr, ki, skip, tw)


def _pad_heads(w, heads, d, front=0):
    k = w.shape[0]
    w = w.reshape(k, heads, d)
    w = jnp.pad(w, ((0, 0), (0, 0), (front, LANE - d - front)))
    return w.reshape(k, heads * LANE)


@functools.lru_cache(maxsize=None)
def _rope_tables():
    pos = np.arange(S)
    rowf = (pos // GRID_W).astype(np.float64)
    colf = (pos % GRID_W).astype(np.float64)

    def pattern(base, half):
        n = half // 2
        inv = ROPE_THETA ** (-np.arange(n, dtype=np.float64) / n)
        cos_cols, lo_cols, hi_cols = [], [], []
        for p in (rowf, colf):
            ang = p[:, None] * inv[None]
            c, s = np.cos(ang), np.sin(ang)
            zero = np.zeros_like(s)
            cos_cols += [c, c]
            lo_cols += [-s, zero]
            hi_cols += [zero, s]
        width = 2 * half

        def place(cols, fill):
            body = np.concatenate(cols, axis=1)
            return np.concatenate([np.full((S, base), fill), body,
                                   np.full((S, LANE - base - width), fill)], axis=1)
        return place(cos_cols, 1.0), place(lo_cols, 0.0), place(hi_cols, 0.0)

    tabs = np.stack(pattern(HEAD, MLA_ROPE // 2) + pattern(0, HEAD // 2))
    ident = np.stack([np.ones((CTX, LANE)), np.zeros((CTX, LANE)), np.zeros((CTX, LANE))] * 2)
    return np.concatenate([ident, tabs], axis=1).astype(np.float32)


def _trig_factors(rows):
    ang = (rows % (4 * NFFT)).astype(np.float64) * (2.0 * math.pi / (4 * NFFT))
    return np.stack([np.cos(ang), np.sin(ang)]).astype(np.float32)


@functools.lru_cache(maxsize=None)
def _conv_trig_factors():
    qs = S // 4
    i = np.arange(TW_PAD)[:, None]
    j = np.arange(TWB)[:, None]
    c = np.arange(qs)[None, :]
    fams = []
    for r in range(4):
        odd = 2 * r + 1
        fams += [2 * TWB * i * (8 * c + odd), (2 * j + 1) * (8 * c + odd)]
    for r in range(4):
        odd = 2 * r + 1
        fams += [8 * TWB * i * (2 * c + 1), (8 * j + odd) * (2 * c + 1)]
    return _trig_factors(np.concatenate(fams, axis=0))


@functools.lru_cache(maxsize=None)
def _filter_trig_factors():
    nblk = S // TWB
    per_group = nblk // 4
    i = np.arange(nblk)[:, None]
    j = np.arange(TWB)[:, None]
    n = np.arange(S)[None, :]
    grp = i // per_group
    blk = 2 * TWB * (i % per_group) * 2 * n
    blk = np.where((grp == 0) | (grp == 3), blk, -blk)
    in_block = [(2 * j + 1) * 2 * n, (2 * S - 1 - 2 * j) * 2 * n,
                (S - 1 - 2 * j) * 2 * n, (S + 1 + 2 * j) * 2 * n]
    return _trig_factors(np.concatenate([blk] + in_block, axis=0))


@functools.lru_cache(maxsize=None)
def _hyena_features():
    t = np.arange(S, dtype=np.float64)
    t_norm = t / S
    w = 2.0 * math.pi * t / S
    bands = np.linspace(1e-4, HY_BANDS - 1, HY_BANDS)
    fw = w[:, None] * bands[None]
    feats = np.concatenate([t_norm[:, None], np.cos(fw), -np.sin(fw)], axis=-1)
    feats = np.pad(feats, ((0, 0), (0, HY_EMB_PAD - HY_EMB)))
    max_decay = math.log(1e-2) / 0.3
    min_decay = math.log(1e-2) / 1.5
    deltas = np.abs(np.linspace(min_decay, max_decay, D))
    return feats.astype(np.float32), deltas.reshape(1, D).astype(np.float32)


def kernel(x, c, ctx, c_ctx, w_mod, b_mod, norm1_g, norm2_g, mlp_w1, mlp_w2, a_w_in, a_q_a_g, a_w_q_b, a_kv_a_g, a_w_kv_b, a_q_norm_g, a_k_norm_g, a_w_out, h_w_in, h_conv_w, h_conv_b, h_f_w1, h_f_b1, h_f_w2, h_f_b2, h_f_w3, h_f_b3, h_f_freq, h_f_w4, h_skip, h_w_out, final_g):
    assert x.shape == (B, S, D) and ctx.shape == (B, CTX, D) and w_mod.shape[0] == 2

    c_rows = jnp.concatenate([c, c_ctx[None], jnp.zeros((MOD_ROWS - B - 1, D), F32)], axis=0)
    mods = _mods(c_rows, w_mod, b_mod)

    def lat_mod(i, k):
        return mods[i, :B, k * D:(k + 1) * D].reshape(B, 1, D)

    def ctx_mod(i, k):
        return mods[i, B:B + 1, k * D:(k + 1) * D]

    w_in = a_w_in[0]
    o_kr = Q_LORA + KV_LORA
    o_gq = o_kr + MLA_ROPE
    o_gk = o_gq + GQA_HEADS * HEAD
    o_gv = o_gk + GQA_KV * HEAD
    win = jnp.concatenate([
        w_in[:, :o_kr],
        _pad_heads(w_in[:, o_kr:o_gq], 1, MLA_ROPE, front=HEAD),
        _pad_heads(w_in[:, o_gq:o_gk], GQA_HEADS, HEAD),
        _pad_heads(w_in[:, o_gk:o_gv], GQA_KV, HEAD),
        _pad_heads(w_in[:, o_gv:], GQA_KV, HEAD),
    ], axis=1).astype(BF16)
    wqb = _pad_heads(a_w_q_b[0], MLA_HEADS, HEAD + MLA_ROPE).astype(BF16)
    wkv = a_w_kv_b[0].reshape(KV_LORA, MLA_HEADS, 2 * HEAD)
    wk = _pad_heads(wkv[:, :, :HEAD].reshape(KV_LORA, MLA_HEADS * HEAD), MLA_HEADS, HEAD).astype(BF16)
    wv = _pad_heads(wkv[:, :, HEAD:].reshape(KV_LORA, MLA_HEADS * HEAD), MLA_HEADS, HEAD).astype(BF16)
    qng = jnp.pad(a_q_norm_g[0], (0, LANE - HEAD)).reshape(1, LANE)
    kng = jnp.pad(a_k_norm_g[0], (0, LANE - HEAD)).reshape(1, LANE)

    q, k, vt = _attn_prep(x, ctx, norm1_g[0:1], lat_mod(0, 1), lat_mod(0, 0),
                          ctx_mod(0, 1), ctx_mod(0, 0), _rope_tables(), win, wqb, wk, wv,
                          a_q_a_g[0:1], a_kv_a_g[0:1], qng, kng)
    o = _attention(q, k, vt)
    x = _mlp(o, a_w_out[0].astype(BF16), lat_mod(0, 2), x,
             norm2_g[0:1], lat_mod(0, 4), lat_mod(0, 3), lat_mod(0, 5),
             mlp_w1[0].astype(BF16), mlp_w2[0].astype(BF16), final_g.reshape(1, D), final=False)

    feats, deltas = _hyena_features()
    kr, ki = _hyena_filters(
        feats, jnp.pad(h_f_w1[0], ((0, HY_EMB_PAD - HY_EMB), (0, 0))), h_f_b1[0:1],
        h_f_w2[0], h_f_b2[0:1], h_f_w3[0], h_f_b3[0:1], h_f_freq[0], h_f_w4[0], deltas,
        _filter_trig_factors())
    z = _modmm(x, norm1_g[1:2], lat_mod(1, 1), lat_mod(1, 0), h_w_in[0].astype(BF16))
    y = _hyena_conv(z, h_conv_w[0], h_conv_b[0:1], kr, ki, h_skip[0], _conv_trig_factors())
    x = _mlp(y, h_w_out[0].astype(BF16), lat_mod(1, 2), x,
             norm2_g[1:2], lat_mod(1, 4), lat_mod(1, 3), lat_mod(1, 5),
             mlp_w1[1].astype(BF16), mlp_w2[1].astype(BF16), final_g.reshape(1, D), final=True)
    return x
```

```python
import functools
import math

import jax
import jax.numpy as jnp
import numpy as np
from jax import lax
from jax.experimental import pallas as pl
from jax.experimental.pallas import tpu as pltpu

F32 = jnp.float32
BF16 = jnp.bfloat16

D = 1024
B = 16
S = 2048
CTX = 256
T = CTX + S
GRID_W = 64
D_FF = 4 * D
N_MOD = 6
HEAD = 64
MLA_HEADS = 8
MLA_ROPE = 32
Q_LORA = 384
KV_LORA = 256
GQA_HEADS = 8
GQA_KV = 2
N_HEADS = MLA_HEADS + GQA_HEADS
LANE = 128
LOG2E = 1.4426950408889634
ROPE_THETA = 10000.0
EPS = 1e-6
HY_BANDS = 8
HY_EMB = 1 + 2 * HY_BANDS
HY_EMB_PAD = 32
HY_HID = 64
NFFT = 2 * S

VMEM_LIMIT = 60 * 1024 * 1024

MOD_ROWS = 24
TM = 1024
TP = 256
TQ = 512
HEADS_PER_STEP = 8
TC = 256
TWB = 128
TW_PAD = 8
SLAB = 32


def _cparams(sem):
    return pltpu.CompilerParams(dimension_semantics=sem, vmem_limit_bytes=VMEM_LIMIT)


def _rms_mod(x, g, scale, shift):
    ms = jnp.mean(x * x, axis=-1, keepdims=True)
    return x * lax.rsqrt(ms + EPS) * (g * (1.0 + scale)) + shift


def _mods_kernel(c_ref, w_ref, b_ref, o_ref):
    c = c_ref[...]
    s = c * (1.0 / (1.0 + jnp.exp(-c)))
    o_ref[...] = jnp.dot(s.astype(BF16), w_ref[...].astype(BF16),
                         preferred_element_type=F32) + b_ref[...]


def _mods(c_rows, w_mod, b_mod):
    depth = w_mod.shape[0]
    tn = 1024
    return pl.pallas_call(
        _mods_kernel,
        grid=(depth, N_MOD * D // tn),
        in_specs=[
            pl.BlockSpec((MOD_ROWS, D), lambda i, j: (0, 0)),
            pl.BlockSpec((None, D, tn), lambda i, j: (i, 0, j)),
            pl.BlockSpec((None, 1, tn), lambda i, j: (i, 0, j)),
        ],
        out_specs=pl.BlockSpec((None, MOD_ROWS, tn), lambda i, j: (i, 0, j)),
        out_shape=jax.ShapeDtypeStruct((depth, MOD_ROWS, N_MOD * D), F32),
        compiler_params=_cparams(("arbitrary", "arbitrary")),
        name="mods",
    )(c_rows, w_mod, b_mod.reshape(depth, 1, N_MOD * D))


def _modmm_kernel(x_ref, g_ref, sc_ref, sh_ref, w_ref, o_ref, *, n_chunk):
    h = _rms_mod(x_ref[...], g_ref[...], sc_ref[...], sh_ref[...]).astype(BF16)
    n = w_ref.shape[1]
    per = n_chunk // TC
    for j in range(n // n_chunk):
        sl = slice(j * n_chunk, (j + 1) * n_chunk)
        y = jnp.dot(h, w_ref[:, sl], preferred_element_type=F32).astype(o_ref.dtype)
        for c in range(per):
            o_ref[j * per + c] = y[:, c * TC:(c + 1) * TC]


def _modmm(x, g, scale, shift, w):
    n = w.shape[1]
    return pl.pallas_call(
        functools.partial(_modmm_kernel, n_chunk=1024),
        grid=(B, S // TM),
        in_specs=[
            pl.BlockSpec((None, TM, D), lambda b, t: (b, t, 0)),
            pl.BlockSpec((1, D), lambda b, t: (0, 0)),
            pl.BlockSpec((None, 1, D), lambda b, t: (b, 0, 0)),
            pl.BlockSpec((None, 1, D), lambda b, t: (b, 0, 0)),
            pl.BlockSpec((D, n), lambda b, t: (0, 0)),
        ],
        out_specs=pl.BlockSpec((None, n // TC, TM, TC), lambda b, t: (b, 0, t, 0)),
        out_shape=jax.ShapeDtypeStruct((B, n // TC, S, TC), BF16),
        compiler_params=_cparams(("arbitrary", "arbitrary")),
        name="modmm",
    )(x, g, scale, shift, w)


def _rope_tiles(xs, cos, sin_lo, sin_hi, n):
    lo = [pltpu.roll(x, LANE - n, 1) for x in xs]
    hi = [pltpu.roll(x, n, 1) for x in xs]
    return [x * cos + a * sin_lo + b * sin_hi for x, a, b in zip(xs, lo, hi)]


def _attn_prep_kernel(*refs):
    *io, z_even, z_odd = refs
    t = pl.program_id(1)

    @pl.when((pl.program_id(0) == 0) & (t == 0))
    def _():
        z_odd[...] = jnp.zeros(z_odd.shape, F32)

    @pl.when(t % 2 == 0)
    def _():
        _attn_prep_step(*io, z_even, z_odd)

    @pl.when(t % 2 == 1)
    def _():
        _attn_prep_step(*io, z_odd, z_even)


def _attn_prep_step(x_ref, ctx_ref, g_ref, sc_ref, sh_ref, csc_ref, csh_ref, tab_ref,
                    win_ref, wqb_ref, wk_ref, wv_ref, qag_ref, kvag_ref, qng_ref, kng_ref,
                    q_ref, k_ref, vt_ref, z_next, z):
    t = pl.program_id(1)
    is_ctx = t == 0
    src = jnp.where(is_ctx, ctx_ref[...], x_ref[...])
    scale = jnp.where(is_ctx, csc_ref[...], sc_ref[...])
    shift = jnp.where(is_ctx, csh_ref[...], sh_ref[...])
    h = _rms_mod(src, g_ref[...], scale, shift).astype(BF16)
    z_next[...] = jnp.dot(h, win_ref[...], preferred_element_type=F32)

    cos_m, slo_m, shi_m = tab_ref[0], tab_ref[1], tab_ref[2]
    cos_g, slo_g, shi_g = tab_ref[3], tab_ref[4], tab_ref[5]
    ones_col = (lax.broadcasted_iota(jnp.int32, (1, LANE), 1) == HEAD).astype(F32)

    o_kr = Q_LORA + KV_LORA
    o_gq = o_kr + LANE
    o_gk = o_gq + GQA_HEADS * LANE
    o_gv = o_gk + GQA_KV * LANE

    def rms(v, g):
        return v * lax.rsqrt(jnp.mean(v * v, axis=-1, keepdims=True) + EPS) * g

    def tile(v, off, i):
        return v[:, off + i * LANE:off + (i + 1) * LANE]

    s_mla = LOG2E / math.sqrt(HEAD + MLA_ROPE)
    s_gqa = LOG2E / math.sqrt(HEAD)

    cq = rms(z[:, :Q_LORA], qag_ref[...] * s_mla).astype(BF16)
    ckv = rms(z[:, Q_LORA:o_kr], kvag_ref[...]).astype(BF16)
    q = jnp.dot(cq, wqb_ref[...], preferred_element_type=F32)
    kn = jnp.dot(ckv, wk_ref[...], preferred_element_type=F32)
    vm = jnp.dot(ckv, wv_ref[...], preferred_element_type=F32)

    gqk = [tile(z, o_gq, i) for i in range(GQA_HEADS)] + [tile(z, o_gk, j) for j in range(GQA_KV)]
    gains = [qng_ref[...] * s_gqa] * GQA_HEADS + [kng_ref[...]] * GQA_KV
    ssq = [jnp.sum(v * v, axis=-1, keepdims=True) for v in gqk]
    gqk = [v * lax.rsqrt(s * (1.0 / HEAD) + EPS) * g for v, s, g in zip(gqk, ssq, gains)]
    gqk = _rope_tiles(gqk, cos_g, slo_g, shi_g, HEAD // 4)

    mla = [tile(q, 0, i) for i in range(MLA_HEADS)] + [z[:, o_kr:o_gq]]
    mla = _rope_tiles(mla, cos_m, slo_m, shi_m, MLA_ROPE // 4)
    kr = mla[MLA_HEADS]

    def put(ref, head, val):
        grp, hh = divmod(head, HEADS_PER_STEP)
        ref[grp, :, hh * LANE:(hh + 1) * LANE] = val

    for hh in range(MLA_HEADS):
        sl = slice(hh * LANE, (hh + 1) * LANE)
        put(q_ref, hh, mla[hh].astype(BF16))
        put(k_ref, hh, (kn[:, sl] + kr).astype(BF16))
        vt_ref[sl, :] = (vm[:, sl] + ones_col).T.astype(BF16)
    for hh in range(GQA_HEADS):
        put(q_ref, MLA_HEADS + hh, gqk[hh].astype(BF16))
    rep = GQA_HEADS // GQA_KV
    for j in range(GQA_KV):
        gk = gqk[GQA_HEADS + j].astype(BF16)
        gvt = (tile(z, o_gv, j) + ones_col).T.astype(BF16)
        for r in range(rep):
            hh = MLA_HEADS + j * rep + r
            put(k_ref, hh, gk)
            vt_ref[hh * LANE:(hh + 1) * LANE, :] = gvt


def _attn_prep(x, ctx, g, sc, sh, csc, csh, tabs, win, wqb, wk, wv, qag, kvag, qng, kng):
    nw = win.shape[1]
    n_tiles = T // TP
    lat = lambda b, t: (b, jnp.clip(t - 1, 0, n_tiles - 2), 0)
    done = lambda t: jnp.maximum(t - 1, 0)
    groups = N_HEADS // HEADS_PER_STEP
    nl = HEADS_PER_STEP * LANE
    full2 = lambda b, t: (0, 0)
    return pl.pallas_call(
        _attn_prep_kernel,
        grid=(B, n_tiles + 1),
        in_specs=[
            pl.BlockSpec((None, TP, D), lat),
            pl.BlockSpec((None, CTX, D), lambda b, t: (b, 0, 0)),
            pl.BlockSpec((1, D), full2),
            pl.BlockSpec((None, 1, D), lambda b, t: (b, 0, 0)),
            pl.BlockSpec((None, 1, D), lambda b, t: (b, 0, 0)),
            pl.BlockSpec((1, D), full2),
            pl.BlockSpec((1, D), full2),
            pl.BlockSpec((6, TP, LANE), lambda b, t: (0, done(t), 0)),
            pl.BlockSpec((D, nw), full2),
            pl.BlockSpec((Q_LORA, MLA_HEADS * LANE), full2),
            pl.BlockSpec((KV_LORA, MLA_HEADS * LANE), full2),
            pl.BlockSpec((KV_LORA, MLA_HEADS * LANE), full2),
            pl.BlockSpec((1, Q_LORA), full2),
            pl.BlockSpec((1, KV_LORA), full2),
            pl.BlockSpec((1, LANE), full2),
            pl.BlockSpec((1, LANE), full2),
        ],
        out_specs=[
            pl.BlockSpec((None, groups, TP, nl), lambda b, t: (b, 0, jnp.maximum(done(t) - 1, 0), 0)),
            pl.BlockSpec((None, groups, TP, nl), lambda b, t: (b, 0, done(t), 0)),
            pl.BlockSpec((None, N_HEADS * LANE, TP), lambda b, t: (b, 0, done(t))),
        ],
        out_shape=[
            jax.ShapeDtypeStruct((B, groups, S, nl), BF16),
            jax.ShapeDtypeStruct((B, groups, T, nl), BF16),
            jax.ShapeDtypeStruct((B, N_HEADS * LANE, T), BF16),
        ],
        scratch_shapes=[pltpu.VMEM((TP, nw), F32), pltpu.VMEM((TP, nw), F32)],
        compiler_params=_cparams(("arbitrary", "arbitrary")),
        name="attn_prep",
    )(x, ctx, g, sc, sh, csc, csh, tabs, win, wqb, wk, wv, qag, kvag, qng, kng)


def _attn_kernel(q_ref, k_ref, vt_ref, o_ref, s0, s1, p0, p1, m0, m1):
    s_buf, p_buf, m_buf = (s0, s1), (p0, p1), (m0, m1)

    def scores(h):
        sl = slice(h * LANE, (h + 1) * LANE)
        st = lax.dot_general(k_ref[:, sl], q_ref[:, sl], (((1,), (1,)), ((), ())),
                             preferred_element_type=F32)
        s_buf[h % 2][...] = st
        m_buf[h % 2][...] = jnp.max(st, axis=0, keepdims=True)

    scores(0)
    for h in range(HEADS_PER_STEP):
        sl = slice(h * LANE, (h + 1) * LANE)
        cur = h % 2
        if h + 1 < HEADS_PER_STEP:
            scores(h + 1)
        p_buf[cur][...] = jnp.exp2(s_buf[cur][...] - m_buf[cur][...]).astype(BF16)
        ot = jnp.dot(vt_ref[sl, :], p_buf[cur][...], preferred_element_type=F32)
        ot = ot[:HEAD, :] / ot[HEAD:HEAD + 1, :]
        if h % 2 == 0:
            even_head = ot
        else:
            pair = jnp.concatenate([even_head, ot], axis=0)
            o_ref[:, (h // 2) * LANE:(h // 2 + 1) * LANE] = pair.T.astype(BF16)


def _attention(q, k, vt):
    nl = HEADS_PER_STEP * LANE
    groups = N_HEADS // HEADS_PER_STEP
    return pl.pallas_call(
        _attn_kernel,
        grid=(B, groups, S // TQ),
        in_specs=[
            pl.BlockSpec((None, None, TQ, nl), lambda b, g, i: (b, g, i, 0)),
            pl.BlockSpec((None, None, T, nl), lambda b, g, i: (b, g, 0, 0)),
            pl.BlockSpec((None, nl, T), lambda b, g, i: (b, g, 0)),
        ],
        out_specs=pl.BlockSpec((None, None, TQ, HEADS_PER_STEP * HEAD), lambda b, g, i: (b, g, i, 0)),
        out_shape=jax.ShapeDtypeStruct((B, groups, S, HEADS_PER_STEP * HEAD), BF16),
        scratch_shapes=[pltpu.VMEM((T, TQ), F32), pltpu.VMEM((T, TQ), F32),
                        pltpu.VMEM((T, TQ), BF16), pltpu.VMEM((T, TQ), BF16),
                        pltpu.VMEM((1, TQ), F32), pltpu.VMEM((1, TQ), F32)],
        compiler_params=_cparams(("arbitrary", "arbitrary", "arbitrary")),
        name="attention",
    )(q, k, vt)


def _mlp_kernel(a_ref, wo_ref, g1_ref, x_ref, g_ref, sc_ref, sh_ref, gate_ref, w1_ref, w2_ref,
                fg_ref, o_ref, *, final, f_chunk):
    a = jnp.concatenate([a_ref[j] for j in range(a_ref.shape[0])], axis=1)
    x = x_ref[...] + g1_ref[...] * jnp.dot(a, wo_ref[...], preferred_element_type=F32)
    h = _rms_mod(x, g_ref[...], sc_ref[...], sh_ref[...]).astype(BF16)
    acc = jnp.zeros(x.shape, F32)
    for j in range(D_FF // f_chunk):
        sl = slice(j * f_chunk, (j + 1) * f_chunk)
        a = jnp.maximum(jnp.dot(h, w1_ref[:, sl], preferred_element_type=F32), 0.0)
        acc = acc + jnp.dot((a * a).astype(BF16), w2_ref[sl, :], preferred_element_type=F32)
    y = x + gate_ref[...] * acc
    if final:
        y = y * lax.rsqrt(jnp.mean(y * y, axis=-1, keepdims=True) + EPS) * fg_ref[...]
    o_ref[...] = y


def _mlp(a, wo, gate1, x, g, scale, shift, gate, w1, w2, final_g, final):
    const = lambda b, t: (0, 0)
    groups, width = a.shape[1], a.shape[3]
    return pl.pallas_call(
        functools.partial(_mlp_kernel, final=final, f_chunk=1024),
        grid=(B, S // TM),
        in_specs=[
            pl.BlockSpec((None, groups, TM, width), lambda b, t: (b, 0, t, 0)),
            pl.BlockSpec((D, D), const, pipeline_mode=pl.Buffered(1)),
            pl.BlockSpec((None, 1, D), lambda b, t: (b, 0, 0)),
            pl.BlockSpec((None, TM, D), lambda b, t: (b, t, 0)),
            pl.BlockSpec((1, D), const),
            pl.BlockSpec((None, 1, D), lambda b, t: (b, 0, 0)),
            pl.BlockSpec((None, 1, D), lambda b, t: (b, 0, 0)),
            pl.BlockSpec((None, 1, D), lambda b, t: (b, 0, 0)),
            pl.BlockSpec((D, D_FF), const, pipeline_mode=pl.Buffered(1)),
            pl.BlockSpec((D_FF, D), const, pipeline_mode=pl.Buffered(1)),
            pl.BlockSpec((1, D), const),
        ],
        out_specs=pl.BlockSpec((None, TM, D), lambda b, t: (b, t, 0)),
        out_shape=jax.ShapeDtypeStruct((B, S, D), F32),
        compiler_params=_cparams(("arbitrary", "arbitrary")),
        name="mlp",
    )(a, wo, gate1, x, g, scale, shift, gate, w1, w2, final_g)


def _build_trig(cos_dst, sin_dst, tw_ref, a0, nblk, b0_of_block):
    for i in range(nblk):
        ca = tw_ref[0, a0 + i:a0 + i + 1, :]
        sa = tw_ref[1, a0 + i:a0 + i + 1, :]
        b0 = b0_of_block(i)
        cb = tw_ref[0, b0:b0 + TWB, :]
        sb = tw_ref[1, b0:b0 + TWB, :]
        rs = slice(i * TWB, (i + 1) * TWB)
        cos_dst[rs, :] = (ca * cb - sa * sb).astype(BF16)
        sin_dst[rs, :] = (sa * cb + ca * sb).astype(BF16)


def _hyena_filter_kernel(feat_ref, w1_ref, b1_ref, w2_ref, b2_ref, w3_ref, b3_ref, fr_ref,
                         w4f0_ref, w4f1_ref, w4b0_ref, w4b1_ref, dl_ref, tw_ref,
                         kr_ref, ki_ref, co_ref, so_ref, hid_hi, hid_lo):
    @pl.when(pl.program_id(0) == 0)
    def _():
        nblk = S // TWB
        _build_trig(co_ref, so_ref, tw_ref, 0, nblk,
                    lambda i: nblk + TWB * (i // (nblk // 4)))

        hp = lax.Precision.HIGHEST
        hid = jnp.sin(fr_ref[0:1, :] * (jnp.dot(feat_ref[...], w1_ref[...], precision=hp,
                                                preferred_element_type=F32) + b1_ref[...]))
        hid = jnp.sin(fr_ref[1:2, :] * (jnp.dot(hid, w2_ref[...], precision=hp,
                                                preferred_element_type=F32) + b2_ref[...]))
        hid = jnp.sin(fr_ref[2:3, :] * (jnp.dot(hid, w3_ref[...], precision=hp,
                                                preferred_element_type=F32) + b3_ref[...]))
        hi = hid.astype(BF16)
        hid_hi[...] = hi
        hid_lo[...] = (hid - hi.astype(F32)).astype(BF16)

    def dot3(w_ref):
        w = w_ref[...]
        w_hi = w.astype(BF16)
        w_lo = (w - w_hi.astype(F32)).astype(BF16)
        return (jnp.dot(hid_hi[...], w_hi, preferred_element_type=F32)
                + jnp.dot(hid_hi[...], w_lo, preferred_element_type=F32)
                + jnp.dot(hid_lo[...], w_hi, preferred_element_type=F32))

    row = lax.broadcasted_iota(jnp.int32, (S, TC), 0)
    t_norm = row.astype(F32) / S
    window = jnp.exp(-t_norm * dl_ref[...])
    for order, (wf_ref, wb_ref) in enumerate(((w4f0_ref, w4b0_ref), (w4f1_ref, w4b1_ref))):
        hf = dot3(wf_ref) * window
        hb = dot3(wb_ref) * window
        ss = jnp.sum(hf * hf + hb * hb, axis=0, keepdims=True)
        nrm = lax.rsqrt(ss + EPS)
        hf = hf * nrm
        hb = jnp.where(row == 0, 0.0, hb * nrm)
        hsum = (hf + hb).astype(BF16)
        hdif = (hf - hb).astype(BF16)
        for r in range(4):
            rs = slice(r * (S // 4), (r + 1) * (S // 4))
            scale = (2.0 if r < 2 else 1.0) / NFFT
            kr_ref[order, rs, :] = scale * jnp.dot(co_ref[rs, :], hsum, preferred_element_type=F32)
            ki_ref[order, rs, :] = -scale * jnp.dot(so_ref[rs, :], hdif, preferred_element_type=F32)


def _hyena_filters(feats, w1, b1, w2, b2, w3, b3, freq, w4, deltas, tw):
    nc = D // TC
    const = lambda c: (0, 0)
    w4spec = lambda k: pl.BlockSpec((HY_HID, TC), lambda c, k=k: (0, k * nc + c))
    return pl.pallas_call(
        _hyena_filter_kernel,
        grid=(nc,),
        in_specs=[
            pl.BlockSpec((S, HY_EMB_PAD), const),
            pl.BlockSpec((HY_EMB_PAD, HY_HID), const),
            pl.BlockSpec((1, HY_HID), const),
            pl.BlockSpec((HY_HID, HY_HID), const),
            pl.BlockSpec((1, HY_HID), const),
            pl.BlockSpec((HY_HID, HY_HID), const),
            pl.BlockSpec((1, HY_HID), const),
            pl.BlockSpec((3, HY_HID), const),
            w4spec(0), w4spec(1), w4spec(2), w4spec(3),
            pl.BlockSpec((1, TC), lambda c: (0, c)),
            pl.BlockSpec(tw.shape, lambda c: (0, 0, 0), pipeline_mode=pl.Buffered(1)),
        ],
        out_specs=[
            pl.BlockSpec((2, S, TC), lambda c: (0, 0, c)),
            pl.BlockSpec((2, S, TC), lambda c: (0, 0, c)),
        ],
        out_shape=[
            jax.ShapeDtypeStruct((2, S, D), F32),
            jax.ShapeDtypeStruct((2, S, D), F32),
        ],
        scratch_shapes=[pltpu.VMEM((S, S), BF16), pltpu.VMEM((S, S), BF16),
                        pltpu.VMEM((S, HY_HID), BF16), pltpu.VMEM((S, HY_HID), BF16)],
        compiler_params=_cparams(("arbitrary",)),
        name="hyena_filters",
    )(feats, w1, b1, w2, b2, w3, b3, freq, w4, w4, w4, w4, deltas, tw)


def _hyena_conv_kernel(x1_ref, x2_ref, v_ref, cw1_ref, cw2_ref, cwv_ref, cb1_ref, cb2_ref,
                       cbv_ref, kr_ref, ki_ref, skip_ref, tw_ref, y_ref,
                       cos_tab, sin_tab, y_buf, g_buf, u_buf, ab_buf, p_buf, nat_buf):
    qs = S // 4
    nblk = qs // TWB
    lanes = [slice(l * LANE, (l + 1) * LANE) for l in range(TC // LANE)]

    @pl.when((pl.program_id(0) == 0) & (pl.program_id(1) == 0))
    def _():
        for fam in range(8):
            base = fam * (TW_PAD + TWB)
            _build_trig(cos_tab.at[fam], sin_tab.at[fam], tw_ref, base, nblk,
                        lambda i, base=base: base + TW_PAD)

    def phases(z_ref):
        for l, ls in enumerate(lanes):
            nat_buf[l] = z_ref[:, ls].astype(F32)
        return [jnp.concatenate([nat_buf[l, pl.ds(r, qs, stride=4), :] for l in range(len(lanes))],
                                axis=1) for r in range(4)]

    row = lax.broadcasted_iota(jnp.int32, (qs, TC), 0)

    def short_conv(z_ref, w_ref, b_ref, dst):
        z = phases(z_ref)
        before = jnp.where(row == 0, 0.0, pltpu.roll(z[3], 1, 0))
        after = jnp.where(row == qs - 1, 0.0, pltpu.roll(z[0], qs - 1, 0))
        w0, w1, w2, b = w_ref[0:1, :], w_ref[1:2, :], w_ref[2:3, :], b_ref[...]
        prev = [before, z[0], z[1], z[2]]
        nxt = [z[1], z[2], z[3], after]
        for r in range(4):
            dst[r] = b + prev[r] * w0 + z[r] * w1 + nxt[r] * w2

    def mm(tab, fam, rhs):
        return jnp.dot(tab[fam], rhs, preferred_element_type=F32)


    short_conv(v_ref, cwv_ref, cbv_ref, y_buf)
    gates = ((x1_ref, cw1_ref, cb1_ref), (x2_ref, cw2_ref, cb2_ref))
    for order, (z_ref, w_ref, b_ref) in enumerate(gates):
        u_buf[...] = y_buf[...].astype(BF16)
        for r in range(4):
            ab_buf[r] = mm(cos_tab, r, u_buf[r])
            ab_buf[4 + r] = mm(sin_tab, r, u_buf[r])
        for i in range(qs // SLAB):
            start = i * SLAB
            rs = slice(start, start + SLAB)
            a = [ab_buf[r, rs, :] for r in range(4)]
            b = [ab_buf[4 + r, rs, :] for r in range(4)]
            el_r, el_i = a[0] + a[2], -(b[0] + b[2])
            p, q = a[0] - a[2], b[0] - b[2]
            eh_r, eh_i = p + q, q - p
            ol_r, ol_i = a[1] + a[3], -(b[1] + b[3])
            p, q = a[1] - a[3], b[1] - b[3]
            oh_r, oh_i = q - p, -(p + q)
            xs = ((el_r + ol_r, el_i + ol_i), (ol_i - el_i, ol_r - el_r),
                  (eh_r + oh_r, eh_i + oh_i), (oh_i - eh_i, oh_r - eh_r))
            z = []
            for g, (xr, xi) in enumerate(xs):
                ks = slice(g * qs + start, g * qs + start + SLAB)
                kr, ki = kr_ref[order, ks, :], ki_ref[order, ks, :]
                z.append((xr * kr - xi * ki, xr * ki + xi * kr))
            el_r, ol_r = z[0][0] - z[1][1], z[0][0] + z[1][1]
            el_i, ol_i = z[0][1] - z[1][0], z[0][1] + z[1][0]
            eh_r, oh_r = z[2][0] - z[3][1], z[2][0] + z[3][1]
            eh_i, oh_i = z[2][1] - z[3][0], z[2][1] + z[3][0]
            p, q = eh_r - eh_i, eh_r + eh_i
            p2, q2 = -(oh_r + oh_i), oh_r - oh_i
            ah = (el_r + p, ol_r + p2, el_r - p, ol_r - p2)
            bh = (q - el_i, q2 - ol_i, -(el_i + q), -(ol_i + q2))
            for r in range(4):
                p_buf[r, rs, :] = ah[r].astype(BF16)
                p_buf[4 + r, rs, :] = bh[r].astype(BF16)
        short_conv(z_ref, w_ref, b_ref, g_buf)
        skip = skip_ref[order:order + 1, :]
        for r in range(4):
            conv = mm(cos_tab, 4 + r, p_buf[r]) + mm(sin_tab, 4 + r, p_buf[4 + r])
            y_buf[r] = g_buf[r] * (conv + y_buf[r] * skip)
    for l, ls in enumerate(lanes):
        for r in range(4):
            nat_buf[l, pl.ds(r, qs, stride=4), :] = y_buf[r, :, ls]
    for l, ls in enumerate(lanes):
        y_ref[:, ls] = nat_buf[l].astype(y_ref.dtype)


def _hyena_conv(z, conv_w, conv_b, kr, ki, skip, tw):
    nc = D // TC
    qs = S // 4
    zspec = lambda k: pl.BlockSpec((None, None, S, TC), lambda c, b, k=k: (b, k * nc + c, 0, 0))
    wspec = lambda k: pl.BlockSpec((3, TC), lambda c, b, k=k: (0, k * nc + c))
    bspec = lambda k: pl.BlockSpec((1, TC), lambda c, b, k=k: (0, k * nc + c))
    return pl.pallas_call(
        _hyena_conv_kernel,
        grid=(nc, B),
        in_specs=[
            zspec(0), zspec(1), zspec(2),
            wspec(0), wspec(1), wspec(2),
            bspec(0), bspec(1), bspec(2),
            pl.BlockSpec((2, S, TC), lambda c, b: (0, 0, c), pipeline_mode=pl.Buffered(1)),
            pl.BlockSpec((2, S, TC), lambda c, b: (0, 0, c), pipeline_mode=pl.Buffered(1)),
            pl.BlockSpec((2, TC), lambda c, b: (0, c)),
            pl.BlockSpec(tw.shape, lambda c, b: (0, 0, 0), pipeline_mode=pl.Buffered(1)),
        ],
        out_specs=pl.BlockSpec((None, None, S, TC), lambda c, b: (b, c, 0, 0)),
        out_shape=jax.ShapeDtypeStruct((B, nc, S, TC), BF16),
        scratch_shapes=[pltpu.VMEM((8, qs, qs), BF16), pltpu.VMEM((8, qs, qs), BF16),
                        pltpu.VMEM((4, qs, TC), F32), pltpu.VMEM((4, qs, TC), F32),
                        pltpu.VMEM((4, qs, TC), BF16), pltpu.VMEM((8, qs, TC), F32),
                        pltpu.VMEM((8, qs, TC), BF16), pltpu.VMEM((TC // LANE, S, LANE), F32)],
        compiler_params=_cparams(("arbitrary", "arbitrary")),
        name="hyena_conv",
    )(z, z, z, conv_w, conv_w, conv_w, conv_b, conv_b, conv_b, kr, ki, skip, tw)


def _pad_heads(w, heads, d, front=0):
    k = w.shape[0]
    w = w.reshape(k, heads, d)
    w = jnp.pad(w, ((0, 0), (0, 0), (front, LANE - d - front)))
    return w.reshape(k, heads * LANE)


@functools.lru_cache(maxsize=None)
def _rope_tables():
    pos = np.arange(S)
    rowf = (pos // GRID_W).astype(np.float64)
    colf = (pos % GRID_W).astype(np.float64)

    def pattern(base, half):
        n = half // 2
        inv = ROPE_THETA ** (-np.arange(n, dtype=np.float64) / n)
        cos_cols, lo_cols, hi_cols = [], [], []
        for p in (rowf, colf):
            ang = p[:, None] * inv[None]
            c, s = np.cos(ang), np.sin(ang)
            zero = np.zeros_like(s)
            cos_cols += [c, c]
            lo_cols += [-s, zero]
            hi_cols += [zero, s]
        width = 2 * half

        def place(cols, fill):
            body = np.concatenate(cols, axis=1)
            return np.concatenate([np.full((S, base), fill), body,
                                   np.full((S, LANE - base - width), fill)], axis=1)
        return place(cos_cols, 1.0), place(lo_cols, 0.0), place(hi_cols, 0.0)

    tabs = np.stack(pattern(HEAD, MLA_ROPE // 2) + pattern(0, HEAD // 2))
    ident = np.stack([np.ones((CTX, LANE)), np.zeros((CTX, LANE)), np.zeros((CTX, LANE))] * 2)
    return np.concatenate([ident, tabs], axis=1).astype(np.float32)


def _trig_factors(rows):
    ang = (rows % (4 * NFFT)).astype(np.float64) * (2.0 * math.pi / (4 * NFFT))
    return np.stack([np.cos(ang), np.sin(ang)]).astype(np.float32)


@functools.lru_cache(maxsize=None)
def _conv_trig_factors():
    qs = S // 4
    i = np.arange(TW_PAD)[:, None]
    j = np.arange(TWB)[:, None]
    c = np.arange(qs)[None, :]
    fams = []
    for r in range(4):
        odd = 2 * r + 1
        fams += [2 * TWB * i * (8 * c + odd), (2 * j + 1) * (8 * c + odd)]
    for r in range(4):
        odd = 2 * r + 1
        fams += [8 * TWB * i * (2 * c + 1), (8 * j + odd) * (2 * c + 1)]
    return _trig_factors(np.concatenate(fams, axis=0))


@functools.lru_cache(maxsize=None)
def _filter_trig_factors():
    nblk = S // TWB
    per_group = nblk // 4
    i = np.arange(nblk)[:, None]
    j = np.arange(TWB)[:, None]
    n = np.arange(S)[None, :]
    grp = i // per_group
    blk = 2 * TWB * (i % per_group) * 2 * n
    blk = np.where((grp == 0) | (grp == 3), blk, -blk)
    in_block = [(2 * j + 1) * 2 * n, (2 * S - 1 - 2 * j) * 2 * n,
                (S - 1 - 2 * j) * 2 * n, (S + 1 + 2 * j) * 2 * n]
    return _trig_factors(np.concatenate([blk] + in_block, axis=0))


@functools.lru_cache(maxsize=None)
def _hyena_features():
    t = np.arange(S, dtype=np.float64)
    t_norm = t / S
    w = 2.0 * math.pi * t / S
    bands = np.linspace(1e-4, HY_BANDS - 1, HY_BANDS)
    fw = w[:, None] * bands[None]
    feats = np.concatenate([t_norm[:, None], np.cos(fw), -np.sin(fw)], axis=-1)
    feats = np.pad(feats, ((0, 0), (0, HY_EMB_PAD - HY_EMB)))
    max_decay = math.log(1e-2) / 0.3
    min_decay = math.log(1e-2) / 1.5
    deltas = np.abs(np.linspace(min_decay, max_decay, D))
    return feats.astype(np.float32), deltas.reshape(1, D).astype(np.float32)


def kernel(x, c, ctx, c_ctx, w_mod, b_mod, norm1_g, norm2_g, mlp_w1, mlp_w2, a_w_in, a_q_a_g, a_w_q_b, a_kv_a_g, a_w_kv_b, a_q_norm_g, a_k_norm_g, a_w_out, h_w_in, h_conv_w, h_conv_b, h_f_w1, h_f_b1, h_f_w2, h_f_b2, h_f_w3, h_f_b3, h_f_freq, h_f_w4, h_skip, h_w_out, final_g):
    assert x.shape == (B, S, D) and ctx.shape == (B, CTX, D) and w_mod.shape[0] == 2

    c_rows = jnp.concatenate([c, c_ctx[None], jnp.zeros((MOD_ROWS - B - 1, D), F32)], axis=0)
    mods = _mods(c_rows, w_mod, b_mod)

    def lat_mod(i, k):
        return mods[i, :B, k * D:(k + 1) * D].reshape(B, 1, D)

    def ctx_mod(i, k):
        return mods[i, B:B + 1, k * D:(k + 1) * D]

    w_in = a_w_in[0]
    o_kr = Q_LORA + KV_LORA
    o_gq = o_kr + MLA_ROPE
    o_gk = o_gq + GQA_HEADS * HEAD
    o_gv = o_gk + GQA_KV * HEAD
    win = jnp.concatenate([
        w_in[:, :o_kr],
        _pad_heads(w_in[:, o_kr:o_gq], 1, MLA_ROPE, front=HEAD),
        _pad_heads(w_in[:, o_gq:o_gk], GQA_HEADS, HEAD),
        _pad_heads(w_in[:, o_gk:o_gv], GQA_KV, HEAD),
        _pad_heads(w_in[:, o_gv:], GQA_KV, HEAD),
    ], axis=1).astype(BF16)
    wqb = _pad_heads(a_w_q_b[0], MLA_HEADS, HEAD + MLA_ROPE).astype(BF16)
    wkv = a_w_kv_b[0].reshape(KV_LORA, MLA_HEADS, 2 * HEAD)
    wk = _pad_heads(wkv[:, :, :HEAD].reshape(KV_LORA, MLA_HEADS * HEAD), MLA_HEADS, HEAD).astype(BF16)
    wv = _pad_heads(wkv[:, :, HEAD:].reshape(KV_LORA, MLA_HEADS * HEAD), MLA_HEADS, HEAD).astype(BF16)
    qng = jnp.pad(a_q_norm_g[0], (0, LANE - HEAD)).reshape(1, LANE)
    kng = jnp.pad(a_k_norm_g[0], (0, LANE - HEAD)).reshape(1, LANE)

    q, k, vt = _attn_prep(x, ctx, norm1_g[0:1], lat_mod(0, 1), lat_mod(0, 0),
                          ctx_mod(0, 1), ctx_mod(0, 0), _rope_tables(), win, wqb, wk, wv,
                          a_q_a_g[0:1], a_kv_a_g[0:1], qng, kng)
    o = _attention(q, k, vt)
    x = _mlp(o, a_w_out[0].astype(BF16), lat_mod(0, 2), x,
             norm2_g[0:1], lat_mod(0, 4), lat_mod(0, 3), lat_mod(0, 5),
             mlp_w1[0].astype(BF16), mlp_w2[0].astype(BF16), final_g.reshape(1, D), final=False)

    feats, deltas = _hyena_features()
    kr, ki = _hyena_filters(
        feats, jnp.pad(h_f_w1[0], ((0, HY_EMB_PAD - HY_EMB), (0, 0))), h_f_b1[0:1],
        h_f_w2[0], h_f_b2[0:1], h_f_w3[0], h_f_b3[0:1], h_f_freq[0], h_f_w4[0], deltas,
        _filter_trig_factors())
    z = _modmm(x, norm1_g[1:2], lat_mod(1, 1), lat_mod(1, 0), h_w_in[0].astype(BF16))
    y = _hyena_conv(z, h_conv_w[0], h_conv_b[0:1], kr, ki, h_skip[0], _conv_trig_factors())
    x = _mlp(y, h_w_out[0].astype(BF16), lat_mod(1, 2), x,
             norm2_g[1:2], lat_mod(1, 4), lat_mod(1, 3), lat_mod(1, 5),
             mlp_w1[1].astype(BF16), mlp_w2[1].astype(BF16), final_g.reshape(1, D), final=True)
    return x
```

```python
import functools
import math

import jax
import jax.numpy as jnp
import numpy as np
from jax import lax
from jax.experimental import pallas as pl
from jax.experimental.pallas import tpu as pltpu

F32 = jnp.float32
BF16 = jnp.bfloat16

D = 1024
B = 16
S = 2048
CTX = 256
T = CTX + S
GRID_W = 64
D_FF = 4 * D
N_MOD = 6
HEAD = 64
MLA_HEADS = 8
MLA_ROPE = 32
Q_LORA = 384
KV_LORA = 256
GQA_HEADS = 8
GQA_KV = 2
N_HEADS = MLA_HEADS + GQA_HEADS
LANE = 128
LOG2E = 1.4426950408889634
ROPE_THETA = 10000.0
EPS = 1e-6
HY_BANDS = 8
HY_EMB = 1 + 2 * HY_BANDS
HY_EMB_PAD = 32
HY_HID = 64
NFFT = 2 * S

VMEM_LIMIT = 60 * 1024 * 1024

MOD_ROWS = 24
TM = 1024
TP = 256
TQ = 512
HEADS_PER_STEP = 8
TC = 256
TWB = 128
TW_PAD = 8
SLAB = 32
N_PHASE = 4


def _cparams(sem):
    return pltpu.CompilerParams(dimension_semantics=sem, vmem_limit_bytes=VMEM_LIMIT)


def _rms_mod(x, g, scale, shift):
    ms = jnp.mean(x * x, axis=-1, keepdims=True)
    return x * lax.rsqrt(ms + EPS) * (g * (1.0 + scale)) + shift


def _mods_kernel(c_ref, w_ref, b_ref, o_ref):
    c = c_ref[...]
    s = c * (1.0 / (1.0 + jnp.exp(-c)))
    o_ref[...] = jnp.dot(s.astype(BF16), w_ref[...].astype(BF16),
                         preferred_element_type=F32) + b_ref[...]


def _mods(c_rows, w_mod, b_mod):
    depth = w_mod.shape[0]
    tn = 1024
    return pl.pallas_call(
        _mods_kernel,
        grid=(depth, N_MOD * D // tn),
        in_specs=[
            pl.BlockSpec((MOD_ROWS, D), lambda i, j: (0, 0)),
            pl.BlockSpec((None, D, tn), lambda i, j: (i, 0, j)),
            pl.BlockSpec((None, 1, tn), lambda i, j: (i, 0, j)),
        ],
        out_specs=pl.BlockSpec((None, MOD_ROWS, tn), lambda i, j: (i, 0, j)),
        out_shape=jax.ShapeDtypeStruct((depth, MOD_ROWS, N_MOD * D), F32),
        compiler_params=_cparams(("arbitrary", "arbitrary")),
        name="mods",
    )(c_rows, w_mod, b_mod.reshape(depth, 1, N_MOD * D))


def _to_phases(x, buf):
    per = x.shape[0] // N_PHASE
    chunks = range(x.shape[1] // LANE)
    for c in chunks:
        buf[c] = x[:, c * LANE:(c + 1) * LANE]
    return jnp.concatenate(
        [jnp.concatenate([buf[c, pl.ds(r, per, stride=N_PHASE), :] for c in chunks], axis=1)
         for r in range(N_PHASE)], axis=0)


def _from_phases(y, buf, o_ref):
    per = y.shape[0] // N_PHASE
    chunks = range(y.shape[1] // LANE)
    for c in chunks:
        for r in range(N_PHASE):
            buf[c, pl.ds(r, per, stride=N_PHASE), :] = y[r * per:(r + 1) * per,
                                                         c * LANE:(c + 1) * LANE]
    for c in chunks:
        o_ref[:, c * LANE:(c + 1) * LANE] = buf[c]


def _modmm_kernel(x_ref, g_ref, sc_ref, sh_ref, w_ref, o_ref, buf, *, n_chunk):
    h = _rms_mod(x_ref[...], g_ref[...], sc_ref[...], sh_ref[...])
    h = _to_phases(h, buf).astype(BF16)
    n = w_ref.shape[1]
    per = n_chunk // TC
    rows = h.shape[0] // N_PHASE
    for j in range(n // n_chunk):
        sl = slice(j * n_chunk, (j + 1) * n_chunk)
        y = jnp.dot(h, w_ref[:, sl], preferred_element_type=F32).astype(o_ref.dtype)
        for c in range(per):
            for r in range(N_PHASE):
                o_ref[j * per + c, r] = y[r * rows:(r + 1) * rows, c * TC:(c + 1) * TC]


def _modmm(x, g, scale, shift, w):
    n = w.shape[1]
    return pl.pallas_call(
        functools.partial(_modmm_kernel, n_chunk=1024),
        grid=(B, S // TM),
        in_specs=[
            pl.BlockSpec((None, TM, D), lambda b, t: (b, t, 0)),
            pl.BlockSpec((1, D), lambda b, t: (0, 0)),
            pl.BlockSpec((None, 1, D), lambda b, t: (b, 0, 0)),
            pl.BlockSpec((None, 1, D), lambda b, t: (b, 0, 0)),
            pl.BlockSpec((D, n), lambda b, t: (0, 0)),
        ],
        out_specs=pl.BlockSpec((None, n // TC, N_PHASE, TM // N_PHASE, TC),
                               lambda b, t: (b, 0, 0, t, 0)),
        out_shape=jax.ShapeDtypeStruct((B, n // TC, N_PHASE, S // N_PHASE, TC), BF16),
        scratch_shapes=[pltpu.VMEM((D // LANE, TM, LANE), F32)],
        compiler_params=_cparams(("arbitrary", "arbitrary")),
        name="modmm",
    )(x, g, scale, shift, w)


def _rope_tiles(xs, cos, sin_lo, sin_hi, n):
    lo = [pltpu.roll(x, LANE - n, 1) for x in xs]
    hi = [pltpu.roll(x, n, 1) for x in xs]
    return [x * cos + a * sin_lo + b * sin_hi for x, a, b in zip(xs, lo, hi)]


def _attn_prep_kernel(*refs):
    *io, z_even, z_odd = refs
    t = pl.program_id(1)

    @pl.when((pl.program_id(0) == 0) & (t == 0))
    def _():
        z_odd[...] = jnp.zeros(z_odd.shape, F32)

    @pl.when(t % 2 == 0)
    def _():
        _attn_prep_step(*io, z_even, z_odd)

    @pl.when(t % 2 == 1)
    def _():
        _attn_prep_step(*io, z_odd, z_even)


def _attn_prep_step(x_ref, ctx_ref, g_ref, sc_ref, sh_ref, csc_ref, csh_ref, tab_ref,
                    win_ref, wqb_ref, wk_ref, wv_ref, qag_ref, kvag_ref, qng_ref, kng_ref,
                    q_ref, k_ref, vt_ref, z_next, z):
    t = pl.program_id(1)
    is_ctx = t == 0
    src = jnp.where(is_ctx, ctx_ref[...], x_ref[...])
    scale = jnp.where(is_ctx, csc_ref[...], sc_ref[...])
    shift = jnp.where(is_ctx, csh_ref[...], sh_ref[...])
    h = _rms_mod(src, g_ref[...], scale, shift).astype(BF16)
    z_next[...] = jnp.dot(h, win_ref[...], preferred_element_type=F32)

    cos_m, slo_m, shi_m = tab_ref[0], tab_ref[1], tab_ref[2]
    cos_g, slo_g, shi_g = tab_ref[3], tab_ref[4], tab_ref[5]
    ones_col = (lax.broadcasted_iota(jnp.int32, (1, LANE), 1) == HEAD).astype(F32)

    o_kr = Q_LORA + KV_LORA
    o_gq = o_kr + LANE
    o_gk = o_gq + GQA_HEADS * LANE
    o_gv = o_gk + GQA_KV * LANE

    def rms(v, g):
        return v * lax.rsqrt(jnp.mean(v * v, axis=-1, keepdims=True) + EPS) * g

    def tile(v, off, i):
        return v[:, off + i * LANE:off + (i + 1) * LANE]

    s_mla = LOG2E / math.sqrt(HEAD + MLA_ROPE)
    s_gqa = LOG2E / math.sqrt(HEAD)

    cq = rms(z[:, :Q_LORA], qag_ref[...] * s_mla).astype(BF16)
    ckv = rms(z[:, Q_LORA:o_kr], kvag_ref[...]).astype(BF16)
    q = jnp.dot(cq, wqb_ref[...], preferred_element_type=F32)
    kn = jnp.dot(ckv, wk_ref[...], preferred_element_type=F32)
    vm = jnp.dot(ckv, wv_ref[...], preferred_element_type=F32)

    gqk = [tile(z, o_gq, i) for i in range(GQA_HEADS)] + [tile(z, o_gk, j) for j in range(GQA_KV)]
    gains = [qng_ref[...] * s_gqa] * GQA_HEADS + [kng_ref[...]] * GQA_KV
    ssq = [jnp.sum(v * v, axis=-1, keepdims=True) for v in gqk]
    gqk = [v * lax.rsqrt(s * (1.0 / HEAD) + EPS) * g for v, s, g in zip(gqk, ssq, gains)]
    gqk = _rope_tiles(gqk, cos_g, slo_g, shi_g, HEAD // 4)

    mla = [tile(q, 0, i) for i in range(MLA_HEADS)] + [z[:, o_kr:o_gq]]
    mla = _rope_tiles(mla, cos_m, slo_m, shi_m, MLA_ROPE // 4)
    kr = mla[MLA_HEADS]

    def put(ref, head, val):
        grp, hh = divmod(head, HEADS_PER_STEP)
        ref[grp, :, hh * LANE:(hh + 1) * LANE] = val

    for hh in range(MLA_HEADS):
        sl = slice(hh * LANE, (hh + 1) * LANE)
        put(q_ref, hh, mla[hh].astype(BF16))
        put(k_ref, hh, (kn[:, sl] + kr).astype(BF16))
        vt_ref[sl, :] = (vm[:, sl] + ones_col).T.astype(BF16)
    for hh in range(GQA_HEADS):
        put(q_ref, MLA_HEADS + hh, gqk[hh].astype(BF16))
    rep = GQA_HEADS // GQA_KV
    for j in range(GQA_KV):
        gk = gqk[GQA_HEADS + j].astype(BF16)
        gvt = (tile(z, o_gv, j) + ones_col).T.astype(BF16)
        for r in range(rep):
            hh = MLA_HEADS + j * rep + r
            put(k_ref, hh, gk)
            vt_ref[hh * LANE:(hh + 1) * LANE, :] = gvt


def _attn_prep(x, ctx, g, sc, sh, csc, csh, tabs, win, wqb, wk, wv, qag, kvag, qng, kng):
    nw = win.shape[1]
    n_tiles = T // TP
    lat = lambda b, t: (b, jnp.clip(t - 1, 0, n_tiles - 2), 0)
    done = lambda t: jnp.maximum(t - 1, 0)
    groups = N_HEADS // HEADS_PER_STEP
    nl = HEADS_PER_STEP * LANE
    full2 = lambda b, t: (0, 0)
    return pl.pallas_call(
        _attn_prep_kernel,
        grid=(B, n_tiles + 1),
        in_specs=[
            pl.BlockSpec((None, TP, D), lat),
            pl.BlockSpec((None, CTX, D), lambda b, t: (b, 0, 0)),
            pl.BlockSpec((1, D), full2),
            pl.BlockSpec((None, 1, D), lambda b, t: (b, 0, 0)),
            pl.BlockSpec((None, 1, D), lambda b, t: (b, 0, 0)),
            pl.BlockSpec((1, D), full2),
            pl.BlockSpec((1, D), full2),
            pl.BlockSpec((6, TP, LANE), lambda b, t: (0, done(t), 0)),
            pl.BlockSpec((D, nw), full2),
            pl.BlockSpec((Q_LORA, MLA_HEADS * LANE), full2),
            pl.BlockSpec((KV_LORA, MLA_HEADS * LANE), full2),
            pl.BlockSpec((KV_LORA, MLA_HEADS * LANE), full2),
            pl.BlockSpec((1, Q_LORA), full2),
            pl.BlockSpec((1, KV_LORA), full2),
            pl.BlockSpec((1, LANE), full2),
            pl.BlockSpec((1, LANE), full2),
        ],
        out_specs=[
            pl.BlockSpec((None, groups, TP, nl), lambda b, t: (b, 0, jnp.maximum(done(t) - 1, 0), 0)),
            pl.BlockSpec((None, groups, TP, nl), lambda b, t: (b, 0, done(t), 0)),
            pl.BlockSpec((None, N_HEADS * LANE, TP), lambda b, t: (b, 0, done(t))),
        ],
        out_shape=[
            jax.ShapeDtypeStruct((B, groups, S, nl), BF16),
            jax.ShapeDtypeStruct((B, groups, T, nl), BF16),
            jax.ShapeDtypeStruct((B, N_HEADS * LANE, T), BF16),
        ],
        scratch_shapes=[pltpu.VMEM((TP, nw), F32), pltpu.VMEM((TP, nw), F32)],
        compiler_params=_cparams(("arbitrary", "arbitrary")),
        name="attn_prep",
    )(x, ctx, g, sc, sh, csc, csh, tabs, win, wqb, wk, wv, qag, kvag, qng, kng)


def _attn_kernel(q_ref, k_ref, vt_ref, o_ref, s0, s1, p0, p1, m0, m1):
    s_buf, p_buf, m_buf = (s0, s1), (p0, p1), (m0, m1)

    def scores(h):
        sl = slice(h * LANE, (h + 1) * LANE)
        st = lax.dot_general(k_ref[:, sl], q_ref[:, sl], (((1,), (1,)), ((), ())),
                             preferred_element_type=F32)
        s_buf[h % 2][...] = st
        m_buf[h % 2][...] = jnp.max(st, axis=0, keepdims=True)

    scores(0)
    for h in range(HEADS_PER_STEP):
        sl = slice(h * LANE, (h + 1) * LANE)
        cur = h % 2
        if h + 1 < HEADS_PER_STEP:
            scores(h + 1)
        p_buf[cur][...] = jnp.exp2(s_buf[cur][...] - m_buf[cur][...]).astype(BF16)
        ot = jnp.dot(vt_ref[sl, :], p_buf[cur][...], preferred_element_type=F32)
        ot = ot[:HEAD, :] / ot[HEAD:HEAD + 1, :]
        if h % 2 == 0:
            even_head = ot
        else:
            pair = jnp.concatenate([even_head, ot], axis=0)
            o_ref[:, (h // 2) * LANE:(h // 2 + 1) * LANE] = pair.T.astype(BF16)


def _attention(q, k, vt):
    nl = HEADS_PER_STEP * LANE
    groups = N_HEADS // HEADS_PER_STEP
    return pl.pallas_call(
        _attn_kernel,
        grid=(B, groups, S // TQ),
        in_specs=[
            pl.BlockSpec((None, None, TQ, nl), lambda b, g, i: (b, g, i, 0)),
            pl.BlockSpec((None, None, T, nl), lambda b, g, i: (b, g, 0, 0)),
            pl.BlockSpec((None, nl, T), lambda b, g, i: (b, g, 0)),
        ],
        out_specs=pl.BlockSpec((None, None, TQ, HEADS_PER_STEP * HEAD), lambda b, g, i: (b, g, i, 0)),
        out_shape=jax.ShapeDtypeStruct((B, groups, S, HEADS_PER_STEP * HEAD), BF16),
        scratch_shapes=[pltpu.VMEM((T, TQ), F32), pltpu.VMEM((T, TQ), F32),
                        pltpu.VMEM((T, TQ), BF16), pltpu.VMEM((T, TQ), BF16),
                        pltpu.VMEM((1, TQ), F32), pltpu.VMEM((1, TQ), F32)],
        compiler_params=_cparams(("arbitrary", "arbitrary", "arbitrary")),
        name="attention",
    )(q, k, vt)


def _mlp_kernel(a_ref, wo_ref, g1_ref, x_ref, g_ref, sc_ref, sh_ref, gate_ref, w1_ref, w2_ref,
                fg_ref, o_ref, *scratch, final, f_chunk, by_phase):
    groups = range(a_ref.shape[0])
    if by_phase:
        a = jnp.concatenate([jnp.concatenate([a_ref[j, r] for j in groups], axis=1)
                             for r in range(N_PHASE)], axis=0)
        x = _to_phases(x_ref, scratch[0])
    else:
        a = jnp.concatenate([a_ref[j] for j in groups], axis=1)
        x = x_ref[...]
    x = x + g1_ref[...] * jnp.dot(a, wo_ref[...], preferred_element_type=F32)
    h = _rms_mod(x, g_ref[...], sc_ref[...], sh_ref[...]).astype(BF16)
    acc = jnp.zeros(x.shape, F32)
    for j in range(D_FF // f_chunk):
        sl = slice(j * f_chunk, (j + 1) * f_chunk)
        a = jnp.maximum(jnp.dot(h, w1_ref[:, sl], preferred_element_type=F32), 0.0)
        acc = acc + jnp.dot((a * a).astype(BF16), w2_ref[sl, :], preferred_element_type=F32)
    y = x + gate_ref[...] * acc
    if final:
        y = y * lax.rsqrt(jnp.mean(y * y, axis=-1, keepdims=True) + EPS) * fg_ref[...]
    if by_phase:
        _from_phases(y, scratch[0], o_ref)
    else:
        o_ref[...] = y


def _mlp(a, wo, gate1, x, g, scale, shift, gate, w1, w2, final_g, final):
    const = lambda b, t: (0, 0)
    groups, width = a.shape[1], a.shape[-1]
    by_phase = a.ndim == 5
    if by_phase:
        a_spec = pl.BlockSpec((None, groups, N_PHASE, TM // N_PHASE, width),
                              lambda b, t: (b, 0, 0, t, 0))
        scratch = [pltpu.VMEM((D // LANE, TM, LANE), F32)]
    else:
        a_spec = pl.BlockSpec((None, groups, TM, width), lambda b, t: (b, 0, t, 0))
        scratch = []
    return pl.pallas_call(
        functools.partial(_mlp_kernel, final=final, f_chunk=1024, by_phase=by_phase),
        grid=(B, S // TM),
        in_specs=[
            a_spec,
            pl.BlockSpec((D, D), const, pipeline_mode=pl.Buffered(1)),
            pl.BlockSpec((None, 1, D), lambda b, t: (b, 0, 0)),
            pl.BlockSpec((None, TM, D), lambda b, t: (b, t, 0)),
            pl.BlockSpec((1, D), const),
            pl.BlockSpec((None, 1, D), lambda b, t: (b, 0, 0)),
            pl.BlockSpec((None, 1, D), lambda b, t: (b, 0, 0)),
            pl.BlockSpec((None, 1, D), lambda b, t: (b, 0, 0)),
            pl.BlockSpec((D, D_FF), const, pipeline_mode=pl.Buffered(1)),
            pl.BlockSpec((D_FF, D), const, pipeline_mode=pl.Buffered(1)),
            pl.BlockSpec((1, D), const),
        ],
        out_specs=pl.BlockSpec((None, TM, D), lambda b, t: (b, t, 0)),
        out_shape=jax.ShapeDtypeStruct((B, S, D), F32),
        scratch_shapes=scratch,
        compiler_params=_cparams(("arbitrary", "arbitrary")),
        name="mlp",
    )(a, wo, gate1, x, g, scale, shift, gate, w1, w2, final_g)


def _build_trig(cos_dst, sin_dst, tw_ref, a0, nblk, b0_of_block):
    for i in range(nblk):
        ca = tw_ref[0, a0 + i:a0 + i + 1, :]
        sa = tw_ref[1, a0 + i:a0 + i + 1, :]
        b0 = b0_of_block(i)
        cb = tw_ref[0, b0:b0 + TWB, :]
        sb = tw_ref[1, b0:b0 + TWB, :]
        rs = slice(i * TWB, (i + 1) * TWB)
        cos_dst[rs, :] = (ca * cb - sa * sb).astype(BF16)
        sin_dst[rs, :] = (sa * cb + ca * sb).astype(BF16)


def _hyena_filter_kernel(feat_ref, w1_ref, b1_ref, w2_ref, b2_ref, w3_ref, b3_ref, fr_ref,
                         w4f0_ref, w4f1_ref, w4b0_ref, w4b1_ref, dl_ref, tw_ref,
                         kr_ref, ki_ref, co_ref, so_ref, hid_hi, hid_lo):
    @pl.when(pl.program_id(0) == 0)
    def _():
        nblk = S // TWB
        _build_trig(co_ref, so_ref, tw_ref, 0, nblk,
                    lambda i: nblk + TWB * (i // (nblk // 4)))

        hp = lax.Precision.HIGHEST
        hid = jnp.sin(fr_ref[0:1, :] * (jnp.dot(feat_ref[...], w1_ref[...], precision=hp,
                                                preferred_element_type=F32) + b1_ref[...]))
        hid = jnp.sin(fr_ref[1:2, :] * (jnp.dot(hid, w2_ref[...], precision=hp,
                                                preferred_element_type=F32) + b2_ref[...]))
        hid = jnp.sin(fr_ref[2:3, :] * (jnp.dot(hid, w3_ref[...], precision=hp,
                                                preferred_element_type=F32) + b3_ref[...]))
        hi = hid.astype(BF16)
        hid_hi[...] = hi
        hid_lo[...] = (hid - hi.astype(F32)).astype(BF16)

    def dot3(w_ref):
        w = w_ref[...]
        w_hi = w.astype(BF16)
        w_lo = (w - w_hi.astype(F32)).astype(BF16)
        return (jnp.dot(hid_hi[...], w_hi, preferred_element_type=F32)
                + jnp.dot(hid_hi[...], w_lo, preferred_element_type=F32)
                + jnp.dot(hid_lo[...], w_hi, preferred_element_type=F32))

    row = lax.broadcasted_iota(jnp.int32, (S, TC), 0)
    t_norm = row.astype(F32) / S
    window = jnp.exp(-t_norm * dl_ref[...])
    for order, (wf_ref, wb_ref) in enumerate(((w4f0_ref, w4b0_ref), (w4f1_ref, w4b1_ref))):
        hf = dot3(wf_ref) * window
        hb = dot3(wb_ref) * window
        ss = jnp.sum(hf * hf + hb * hb, axis=0, keepdims=True)
        nrm = lax.rsqrt(ss + EPS)
        hf = hf * nrm
        hb = jnp.where(row == 0, 0.0, hb * nrm)
        hsum = (hf + hb).astype(BF16)
        hdif = (hf - hb).astype(BF16)
        for r in range(4):
            rs = slice(r * (S // 4), (r + 1) * (S // 4))
            scale = (2.0 if r < 2 else 1.0) / NFFT
            kr_ref[order, rs, :] = scale * jnp.dot(co_ref[rs, :], hsum, preferred_element_type=F32)
            ki_ref[order, rs, :] = -scale * jnp.dot(so_ref[rs, :], hdif, preferred_element_type=F32)


def _hyena_filters(feats, w1, b1, w2, b2, w3, b3, freq, w4, deltas, tw):
    nc = D // TC
    const = lambda c: (0, 0)
    w4spec = lambda k: pl.BlockSpec((HY_HID, TC), lambda c, k=k: (0, k * nc + c))
    return pl.pallas_call(
        _hyena_filter_kernel,
        grid=(nc,),
        in_specs=[
            pl.BlockSpec((S, HY_EMB_PAD), const),
            pl.BlockSpec((HY_EMB_PAD, HY_HID), const),
            pl.BlockSpec((1, HY_HID), const),
            pl.BlockSpec((HY_HID, HY_HID), const),
            pl.BlockSpec((1, HY_HID), const),
            pl.BlockSpec((HY_HID, HY_HID), const),
            pl.BlockSpec((1, HY_HID), const),
            pl.BlockSpec((3, HY_HID), const),
            w4spec(0), w4spec(1), w4spec(2), w4spec(3),
            pl.BlockSpec((1, TC), lambda c: (0, c)),
            pl.BlockSpec(tw.shape, lambda c: (0, 0, 0), pipeline_mode=pl.Buffered(1)),
        ],
        out_specs=[
            pl.BlockSpec((2, S, TC), lambda c: (0, 0, c)),
            pl.BlockSpec((2, S, TC), lambda c: (0, 0, c)),
        ],
        out_shape=[
            jax.ShapeDtypeStruct((2, S, D), F32),
            jax.ShapeDtypeStruct((2, S, D), F32),
        ],
        scratch_shapes=[pltpu.VMEM((S, S), BF16), pltpu.VMEM((S, S), BF16),
                        pltpu.VMEM((S, HY_HID), BF16), pltpu.VMEM((S, HY_HID), BF16)],
        compiler_params=_cparams(("arbitrary",)),
        name="hyena_filters",
    )(feats, w1, b1, w2, b2, w3, b3, freq, w4, w4, w4, w4, deltas, tw)


def _hyena_conv_kernel(x1_ref, x2_ref, v_ref, cw1_ref, cw2_ref, cwv_ref, cb1_ref, cb2_ref,
                       cbv_ref, kr_ref, ki_ref, skip_ref, tw_ref, y_ref,
                       cos_tab, sin_tab, y_buf, g_buf, u_buf, ab_buf, p_buf):
    qs = S // N_PHASE
    nblk = qs // TWB

    @pl.when((pl.program_id(0) == 0) & (pl.program_id(1) == 0))
    def _():
        for fam in range(8):
            base = fam * (TW_PAD + TWB)
            _build_trig(cos_tab.at[fam], sin_tab.at[fam], tw_ref, base, nblk,
                        lambda i, base=base: base + TW_PAD)

    row = lax.broadcasted_iota(jnp.int32, (qs, TC), 0)

    def short_conv(z_ref, w_ref, b_ref, dst):
        z = [z_ref[r].astype(F32) for r in range(N_PHASE)]
        before = jnp.where(row == 0, 0.0, pltpu.roll(z[3], 1, 0))
        after = jnp.where(row == qs - 1, 0.0, pltpu.roll(z[0], qs - 1, 0))
        w0, w1, w2, b = w_ref[0:1, :], w_ref[1:2, :], w_ref[2:3, :], b_ref[...]
        prev = [before, z[0], z[1], z[2]]
        nxt = [z[1], z[2], z[3], after]
        for r in range(4):
            dst[r] = b + prev[r] * w0 + z[r] * w1 + nxt[r] * w2

    def mm(tab, fam, rhs):
        return jnp.dot(tab[fam], rhs, preferred_element_type=F32)


    short_conv(v_ref, cwv_ref, cbv_ref, y_buf)
    gates = ((x1_ref, cw1_ref, cb1_ref), (x2_ref, cw2_ref, cb2_ref))
    for order, (z_ref, w_ref, b_ref) in enumerate(gates):
        u_buf[...] = y_buf[...].astype(BF16)
        for r in range(4):
            ab_buf[r] = mm(cos_tab, r, u_buf[r])
            ab_buf[4 + r] = mm(sin_tab, r, u_buf[r])
        for i in range(qs // SLAB):
            start = i * SLAB
            rs = slice(start, start + SLAB)
            a = [ab_buf[r, rs, :] for r in range(4)]
            b = [ab_buf[4 + r, rs, :] for r in range(4)]
            el_r, el_i = a[0] + a[2], -(b[0] + b[2])
            p, q = a[0] - a[2], b[0] - b[2]
            eh_r, eh_i = p + q, q - p
            ol_r, ol_i = a[1] + a[3], -(b[1] + b[3])
            p, q = a[1] - a[3], b[1] - b[3]
            oh_r, oh_i = q - p, -(p + q)
            xs = ((el_r + ol_r, el_i + ol_i), (ol_i - el_i, ol_r - el_r),
                  (eh_r + oh_r, eh_i + oh_i), (oh_i - eh_i, oh_r - eh_r))
            z = []
            for g, (xr, xi) in enumerate(xs):
                ks = slice(g * qs + start, g * qs + start + SLAB)
                kr, ki = kr_ref[order, ks, :], ki_ref[order, ks, :]
                z.append((xr * kr - xi * ki, xr * ki + xi * kr))
            el_r, ol_r = z[0][0] - z[1][1], z[0][0] + z[1][1]
            el_i, ol_i = z[0][1] - z[1][0], z[0][1] + z[1][0]
            eh_r, oh_r = z[2][0] - z[3][1], z[2][0] + z[3][1]
            eh_i, oh_i = z[2][1] - z[3][0], z[2][1] + z[3][0]
            p, q = eh_r - eh_i, eh_r + eh_i
            p2, q2 = -(oh_r + oh_i), oh_r - oh_i
            ah = (el_r + p, ol_r + p2, el_r - p, ol_r - p2)
            bh = (q - el_i, q2 - ol_i, -(el_i + q), -(ol_i + q2))
            for r in range(4):
                p_buf[r, rs, :] = ah[r].astype(BF16)
                p_buf[4 + r, rs, :] = bh[r].astype(BF16)
        short_conv(z_ref, w_ref, b_ref, g_buf)
        skip = skip_ref[order:order + 1, :]
        for r in range(4):
            conv = mm(cos_tab, 4 + r, p_buf[r]) + mm(sin_tab, 4 + r, p_buf[4 + r])
            y_buf[r] = g_buf[r] * (conv + y_buf[r] * skip)
    y_ref[...] = y_buf[...].astype(y_ref.dtype)


def _hyena_conv(z, conv_w, conv_b, kr, ki, skip, tw):
    nc = D // TC
    qs = S // 4
    zspec = lambda k: pl.BlockSpec((None, None, N_PHASE, qs, TC),
                                   lambda c, b, k=k: (b, k * nc + c, 0, 0, 0))
    wspec = lambda k: pl.BlockSpec((3, TC), lambda c, b, k=k: (0, k * nc + c))
    bspec = lambda k: pl.BlockSpec((1, TC), lambda c, b, k=k: (0, k * nc + c))
    return pl.pallas_call(
        _hyena_conv_kernel,
        grid=(nc, B),
        in_specs=[
            zspec(0), zspec(1), zspec(2),
            wspec(0), wspec(1), wspec(2),
            bspec(0), bspec(1), bspec(2),
            pl.BlockSpec((2, S, TC), lambda c, b: (0, 0, c), pipeline_mode=pl.Buffered(1)),
            pl.BlockSpec((2, S, TC), lambda c, b: (0, 0, c), pipeline_mode=pl.Buffered(1)),
            pl.BlockSpec((2, TC), lambda c, b: (0, c)),
            pl.BlockSpec(tw.shape, lambda c, b: (0, 0, 0), pipeline_mode=pl.Buffered(1)),
        ],
        out_specs=pl.BlockSpec((None, None, N_PHASE, qs, TC), lambda c, b: (b, c, 0, 0, 0)),
        out_shape=jax.ShapeDtypeStruct((B, nc, N_PHASE, qs, TC), BF16),
        scratch_shapes=[pltpu.VMEM((8, qs, qs), BF16), pltpu.VMEM((8, qs, qs), BF16),
                        pltpu.VMEM((4, qs, TC), F32), pltpu.VMEM((4, qs, TC), F32),
                        pltpu.VMEM((4, qs, TC), BF16), pltpu.VMEM((8, qs, TC), F32),
                        pltpu.VMEM((8, qs, TC), BF16)],
        compiler_params=_cparams(("arbitrary", "arbitrary")),
        name="hyena_conv",
    )(z, z, z, conv_w, conv_w, conv_w, conv_b, conv_b, conv_b, kr, ki, skip, tw)


def _pad_heads(w, heads, d, front=0):
    k = w.shape[0]
    w = w.reshape(k, heads, d)
    w = jnp.pad(w, ((0, 0), (0, 0), (front, LANE - d - front)))
    return w.reshape(k, heads * LANE)


@functools.lru_cache(maxsize=None)
def _rope_tables():
    pos = np.arange(S)
    rowf = (pos // GRID_W).astype(np.float64)
    colf = (pos % GRID_W).astype(np.float64)

    def pattern(base, half):
        n = half // 2
        inv = ROPE_THETA ** (-np.arange(n, dtype=np.float64) / n)
        cos_cols, lo_cols, hi_cols = [], [], []
        for p in (rowf, colf):
            ang = p[:, None] * inv[None]
            c, s = np.cos(ang), np.sin(ang)
            zero = np.zeros_like(s)
            cos_cols += [c, c]
            lo_cols += [-s, zero]
            hi_cols += [zero, s]
        width = 2 * half

        def place(cols, fill):
            body = np.concatenate(cols, axis=1)
            return np.concatenate([np.full((S, base), fill), body,
                                   np.full((S, LANE - base - width), fill)], axis=1)
        return place(cos_cols, 1.0), place(lo_cols, 0.0), place(hi_cols, 0.0)

    tabs = np.stack(pattern(HEAD, MLA_ROPE // 2) + pattern(0, HEAD // 2))
    ident = np.stack([np.ones((CTX, LANE)), np.zeros((CTX, LANE)), np.zeros((CTX, LANE))] * 2)
    return np.concatenate([ident, tabs], axis=1).astype(np.float32)


def _trig_factors(rows):
    ang = (rows % (4 * NFFT)).astype(np.float64) * (2.0 * math.pi / (4 * NFFT))
    return np.stack([np.cos(ang), np.sin(ang)]).astype(np.float32)


@functools.lru_cache(maxsize=None)
def _conv_trig_factors():
    qs = S // 4
    i = np.arange(TW_PAD)[:, None]
    j = np.arange(TWB)[:, None]
    c = np.arange(qs)[None, :]
    fams = []
    for r in range(4):
        odd = 2 * r + 1
        fams += [2 * TWB * i * (8 * c + odd), (2 * j + 1) * (8 * c + odd)]
    for r in range(4):
        odd = 2 * r + 1
        fams += [8 * TWB * i * (2 * c + 1), (8 * j + odd) * (2 * c + 1)]
    return _trig_factors(np.concatenate(fams, axis=0))


@functools.lru_cache(maxsize=None)
def _filter_trig_factors():
    nblk = S // TWB
    per_group = nblk // 4
    i = np.arange(nblk)[:, None]
    j = np.arange(TWB)[:, None]
    n = np.arange(S)[None, :]
    grp = i // per_group
    blk = 2 * TWB * (i % per_group) * 2 * n
    blk = np.where((grp == 0) | (grp == 3), blk, -blk)
    in_block = [(2 * j + 1) * 2 * n, (2 * S - 1 - 2 * j) * 2 * n,
                (S - 1 - 2 * j) * 2 * n, (S + 1 + 2 * j) * 2 * n]
    return _trig_factors(np.concatenate([blk] + in_block, axis=0))


@functools.lru_cache(maxsize=None)
def _hyena_features():
    t = np.arange(S, dtype=np.float64)
    t_norm = t / S
    w = 2.0 * math.pi * t / S
    bands = np.linspace(1e-4, HY_BANDS - 1, HY_BANDS)
    fw = w[:, None] * bands[None]
    feats = np.concatenate([t_norm[:, None], np.cos(fw), -np.sin(fw)], axis=-1)
    feats = np.pad(feats, ((0, 0), (0, HY_EMB_PAD - HY_EMB)))
    max_decay = math.log(1e-2) / 0.3
    min_decay = math.log(1e-2) / 1.5
    deltas = np.abs(np.linspace(min_decay, max_decay, D))
    return feats.astype(np.float32), deltas.reshape(1, D).astype(np.float32)


def kernel(x, c, ctx, c_ctx, w_mod, b_mod, norm1_g, norm2_g, mlp_w1, mlp_w2, a_w_in, a_q_a_g, a_w_q_b, a_kv_a_g, a_w_kv_b, a_q_norm_g, a_k_norm_g, a_w_out, h_w_in, h_conv_w, h_conv_b, h_f_w1, h_f_b1, h_f_w2, h_f_b2, h_f_w3, h_f_b3, h_f_freq, h_f_w4, h_skip, h_w_out, final_g):
    assert x.shape == (B, S, D) and ctx.shape == (B, CTX, D) and w_mod.shape[0] == 2

    c_rows = jnp.concatenate([c, c_ctx[None], jnp.zeros((MOD_ROWS - B - 1, D), F32)], axis=0)
    mods = _mods(c_rows, w_mod, b_mod)

    def lat_mod(i, k):
        return mods[i, :B, k * D:(k + 1) * D].reshape(B, 1, D)

    def ctx_mod(i, k):
        return mods[i, B:B + 1, k * D:(k + 1) * D]

    w_in = a_w_in[0]
    o_kr = Q_LORA + KV_LORA
    o_gq = o_kr + MLA_ROPE
    o_gk = o_gq + GQA_HEADS * HEAD
    o_gv = o_gk + GQA_KV * HEAD
    win = jnp.concatenate([
        w_in[:, :o_kr],
        _pad_heads(w_in[:, o_kr:o_gq], 1, MLA_ROPE, front=HEAD),
        _pad_heads(w_in[:, o_gq:o_gk], GQA_HEADS, HEAD),
        _pad_heads(w_in[:, o_gk:o_gv], GQA_KV, HEAD),
        _pad_heads(w_in[:, o_gv:], GQA_KV, HEAD),
    ], axis=1).astype(BF16)
    wqb = _pad_heads(a_w_q_b[0], MLA_HEADS, HEAD + MLA_ROPE).astype(BF16)
    wkv = a_w_kv_b[0].reshape(KV_LORA, MLA_HEADS, 2 * HEAD)
    wk = _pad_heads(wkv[:, :, :HEAD].reshape(KV_LORA, MLA_HEADS * HEAD), MLA_HEADS, HEAD).astype(BF16)
    wv = _pad_heads(wkv[:, :, HEAD:].reshape(KV_LORA, MLA_HEADS * HEAD), MLA_HEADS, HEAD).astype(BF16)
    qng = jnp.pad(a_q_norm_g[0], (0, LANE - HEAD)).reshape(1, LANE)
    kng = jnp.pad(a_k_norm_g[0], (0, LANE - HEAD)).reshape(1, LANE)

    q, k, vt = _attn_prep(x, ctx, norm1_g[0:1], lat_mod(0, 1), lat_mod(0, 0),
                          ctx_mod(0, 1), ctx_mod(0, 0), _rope_tables(), win, wqb, wk, wv,
                          a_q_a_g[0:1], a_kv_a_g[0:1], qng, kng)
    o = _attention(q, k, vt)
    x = _mlp(o, a_w_out[0].astype(BF16), lat_mod(0, 2), x,
             norm2_g[0:1], lat_mod(0, 4), lat_mod(0, 3), lat_mod(0, 5),
             mlp_w1[0].astype(BF16), mlp_w2[0].astype(BF16), final_g.reshape(1, D), final=False)

    feats, deltas = _hyena_features()
    kr, ki = _hyena_filters(
        feats, jnp.pad(h_f_w1[0], ((0, HY_EMB_PAD - HY_EMB), (0, 0))), h_f_b1[0:1],
        h_f_w2[0], h_f_b2[0:1], h_f_w3[0], h_f_b3[0:1], h_f_freq[0], h_f_w4[0], deltas,
        _filter_trig_factors())
    z = _modmm(x, norm1_g[1:2], lat_mod(1, 1), lat_mod(1, 0), h_w_in[0].astype(BF16))
    y = _hyena_conv(z, h_conv_w[0], h_conv_b[0:1], kr, ki, h_skip[0], _conv_trig_factors())
    x = _mlp(y, h_w_out[0].astype(BF16), lat_mod(1, 2), x,
             norm2_g[1:2], lat_mod(1, 4), lat_mod(1, 3), lat_mod(1, 5),
             mlp_w1[1].astype(BF16), mlp_w2[1].astype(BF16), final_g.reshape(1, D), final=True)
    return x
```

```python
import functools
import math

import jax
import jax.numpy as jnp
import numpy as np
from jax import lax
from jax.experimental import pallas as pl
from jax.experimental.pallas import tpu as pltpu

F32 = jnp.float32
BF16 = jnp.bfloat16

D = 1024
B = 16
S = 2048
CTX = 256
T = CTX + S
GRID_W = 64
D_FF = 4 * D
N_MOD = 6
HEAD = 64
MLA_HEADS = 8
MLA_ROPE = 32
Q_LORA = 384
KV_LORA = 256
GQA_HEADS = 8
GQA_KV = 2
N_HEADS = MLA_HEADS + GQA_HEADS
LANE = 128
LOG2E = 1.4426950408889634
ROPE_THETA = 10000.0
EPS = 1e-6
HY_BANDS = 8
HY_EMB = 1 + 2 * HY_BANDS
HY_EMB_PAD = 32
HY_HID = 64
NFFT = 2 * S

VMEM_LIMIT = 60 * 1024 * 1024

MOD_ROWS = 24
TM = 1024
TP = 256
TQ = 1024
HEADS_PER_STEP = 4
TC = 256
TWB = 128
TW_PAD = 8
SLAB = 32


def _cparams(sem):
    return pltpu.CompilerParams(dimension_semantics=sem, vmem_limit_bytes=VMEM_LIMIT)


def _rms_mod(x, g, scale, shift):
    ms = jnp.mean(x * x, axis=-1, keepdims=True)
    return x * lax.rsqrt(ms + EPS) * (g * (1.0 + scale)) + shift


def _mods_kernel(c_ref, w_ref, b_ref, o_ref):
    c = c_ref[...]
    s = c * (1.0 / (1.0 + jnp.exp(-c)))
    o_ref[...] = jnp.dot(s.astype(BF16), w_ref[...].astype(BF16),
                         preferred_element_type=F32) + b_ref[...]


def _mods(c_rows, w_mod, b_mod):
    depth = w_mod.shape[0]
    tn = 1024
    return pl.pallas_call(
        _mods_kernel,
        grid=(depth, N_MOD * D // tn),
        in_specs=[
            pl.BlockSpec((MOD_ROWS, D), lambda i, j: (0, 0)),
            pl.BlockSpec((None, D, tn), lambda i, j: (i, 0, j)),
            pl.BlockSpec((None, 1, tn), lambda i, j: (i, 0, j)),
        ],
        out_specs=pl.BlockSpec((None, MOD_ROWS, tn), lambda i, j: (i, 0, j)),
        out_shape=jax.ShapeDtypeStruct((depth, MOD_ROWS, N_MOD * D), F32),
        compiler_params=_cparams(("arbitrary", "arbitrary")),
        name="mods",
    )(c_rows, w_mod, b_mod.reshape(depth, 1, N_MOD * D))


def _modmm_kernel(x_ref, g_ref, sc_ref, sh_ref, w_ref, o_ref, *, n_chunk):
    h = _rms_mod(x_ref[...], g_ref[...], sc_ref[...], sh_ref[...]).astype(BF16)
    n = w_ref.shape[1]
    per = n_chunk // TC
    for j in range(n // n_chunk):
        sl = slice(j * n_chunk, (j + 1) * n_chunk)
        y = jnp.dot(h, w_ref[:, sl], preferred_element_type=F32).astype(o_ref.dtype)
        for c in range(per):
            o_ref[j * per + c] = y[:, c * TC:(c + 1) * TC]


def _modmm(x, g, scale, shift, w):
    n = w.shape[1]
    return pl.pallas_call(
        functools.partial(_modmm_kernel, n_chunk=1024),
        grid=(B, S // TM),
        in_specs=[
            pl.BlockSpec((None, TM, D), lambda b, t: (b, t, 0)),
            pl.BlockSpec((1, D), lambda b, t: (0, 0)),
            pl.BlockSpec((None, 1, D), lambda b, t: (b, 0, 0)),
            pl.BlockSpec((None, 1, D), lambda b, t: (b, 0, 0)),
            pl.BlockSpec((D, n), lambda b, t: (0, 0)),
        ],
        out_specs=pl.BlockSpec((None, n // TC, TM, TC), lambda b, t: (b, 0, t, 0)),
        out_shape=jax.ShapeDtypeStruct((B, n // TC, S, TC), BF16),
        compiler_params=_cparams(("arbitrary", "arbitrary")),
        name="modmm",
    )(x, g, scale, shift, w)


def _rope_tiles(xs, cos, sin_lo, sin_hi, n):
    lo = [pltpu.roll(x, LANE - n, 1) for x in xs]
    hi = [pltpu.roll(x, n, 1) for x in xs]
    return [x * cos + a * sin_lo + b * sin_hi for x, a, b in zip(xs, lo, hi)]


def _attn_prep_kernel(*refs):
    *io, z_even, z_odd = refs
    t = pl.program_id(1)

    @pl.when((pl.program_id(0) == 0) & (t == 0))
    def _():
        z_odd[...] = jnp.zeros(z_odd.shape, F32)

    @pl.when(t % 2 == 0)
    def _():
        _attn_prep_step(*io, z_even, z_odd)

    @pl.when(t % 2 == 1)
    def _():
        _attn_prep_step(*io, z_odd, z_even)


def _attn_prep_step(x_ref, ctx_ref, g_ref, sc_ref, sh_ref, csc_ref, csh_ref, tab_ref,
                    win_ref, wqb_ref, wk_ref, wv_ref, qag_ref, kvag_ref, qng_ref, kng_ref,
                    q_ref, k_ref, vt_ref, z_next, z):
    t = pl.program_id(1)
    is_ctx = t == 0
    src = jnp.where(is_ctx, ctx_ref[...], x_ref[...])
    scale = jnp.where(is_ctx, csc_ref[...], sc_ref[...])
    shift = jnp.where(is_ctx, csh_ref[...], sh_ref[...])
    h = _rms_mod(src, g_ref[...], scale, shift).astype(BF16)
    z_next[...] = jnp.dot(h, win_ref[...], preferred_element_type=F32)

    cos_m, slo_m, shi_m = tab_ref[0], tab_ref[1], tab_ref[2]
    cos_g, slo_g, shi_g = tab_ref[3], tab_ref[4], tab_ref[5]
    ones_col = (lax.broadcasted_iota(jnp.int32, (1, LANE), 1) == HEAD).astype(F32)

    o_kr = Q_LORA + KV_LORA
    o_gq = o_kr + LANE
    o_gk = o_gq + GQA_HEADS * LANE
    o_gv = o_gk + GQA_KV * LANE

    def rms(v, g):
        return v * lax.rsqrt(jnp.mean(v * v, axis=-1, keepdims=True) + EPS) * g

    def tile(v, off, i):
        return v[:, off + i * LANE:off + (i + 1) * LANE]

    s_mla = LOG2E / math.sqrt(HEAD + MLA_ROPE)
    s_gqa = LOG2E / math.sqrt(HEAD)

    cq = rms(z[:, :Q_LORA], qag_ref[...] * s_mla).astype(BF16)
    ckv = rms(z[:, Q_LORA:o_kr], kvag_ref[...]).astype(BF16)
    q = jnp.dot(cq, wqb_ref[...], preferred_element_type=F32)
    kn = jnp.dot(ckv, wk_ref[...], preferred_element_type=F32)
    vm = jnp.dot(ckv, wv_ref[...], preferred_element_type=F32)

    gqk = [tile(z, o_gq, i) for i in range(GQA_HEADS)] + [tile(z, o_gk, j) for j in range(GQA_KV)]
    gains = [qng_ref[...] * s_gqa] * GQA_HEADS + [kng_ref[...]] * GQA_KV
    ssq = [jnp.sum(v * v, axis=-1, keepdims=True) for v in gqk]
    gqk = [v * lax.rsqrt(s * (1.0 / HEAD) + EPS) * g for v, s, g in zip(gqk, ssq, gains)]
    gqk = _rope_tiles(gqk, cos_g, slo_g, shi_g, HEAD // 4)

    mla = [tile(q, 0, i) for i in range(MLA_HEADS)] + [z[:, o_kr:o_gq]]
    mla = _rope_tiles(mla, cos_m, slo_m, shi_m, MLA_ROPE // 4)
    kr = mla[MLA_HEADS]

    def put(ref, head, val):
        grp, hh = divmod(head, HEADS_PER_STEP)
        ref[grp, :, hh * LANE:(hh + 1) * LANE] = val

    for hh in range(MLA_HEADS):
        sl = slice(hh * LANE, (hh + 1) * LANE)
        put(q_ref, hh, mla[hh].astype(BF16))
        put(k_ref, hh, (kn[:, sl] + kr).astype(BF16))
        vt_ref[sl, :] = (vm[:, sl] + ones_col).T.astype(BF16)
    for hh in range(GQA_HEADS):
        put(q_ref, MLA_HEADS + hh, gqk[hh].astype(BF16))
    rep = GQA_HEADS // GQA_KV
    for j in range(GQA_KV):
        gk = gqk[GQA_HEADS + j].astype(BF16)
        gvt = (tile(z, o_gv, j) + ones_col).T.astype(BF16)
        for r in range(rep):
            hh = MLA_HEADS + j * rep + r
            put(k_ref, hh, gk)
            vt_ref[hh * LANE:(hh + 1) * LANE, :] = gvt


def _attn_prep(x, ctx, g, sc, sh, csc, csh, tabs, win, wqb, wk, wv, qag, kvag, qng, kng):
    nw = win.shape[1]
    n_tiles = T // TP
    lat = lambda b, t: (b, jnp.clip(t - 1, 0, n_tiles - 2), 0)
    done = lambda t: jnp.maximum(t - 1, 0)
    groups = N_HEADS // HEADS_PER_STEP
    nl = HEADS_PER_STEP * LANE
    full2 = lambda b, t: (0, 0)
    return pl.pallas_call(
        _attn_prep_kernel,
        grid=(B, n_tiles + 1),
        in_specs=[
            pl.BlockSpec((None, TP, D), lat),
            pl.BlockSpec((None, CTX, D), lambda b, t: (b, 0, 0)),
            pl.BlockSpec((1, D), full2),
            pl.BlockSpec((None, 1, D), lambda b, t: (b, 0, 0)),
            pl.BlockSpec((None, 1, D), lambda b, t: (b, 0, 0)),
            pl.BlockSpec((1, D), full2),
            pl.BlockSpec((1, D), full2),
            pl.BlockSpec((6, TP, LANE), lambda b, t: (0, done(t), 0)),
            pl.BlockSpec((D, nw), full2),
            pl.BlockSpec((Q_LORA, MLA_HEADS * LANE), full2),
            pl.BlockSpec((KV_LORA, MLA_HEADS * LANE), full2),
            pl.BlockSpec((KV_LORA, MLA_HEADS * LANE), full2),
            pl.BlockSpec((1, Q_LORA), full2),
            pl.BlockSpec((1, KV_LORA), full2),
            pl.BlockSpec((1, LANE), full2),
            pl.BlockSpec((1, LANE), full2),
        ],
        out_specs=[
            pl.BlockSpec((None, groups, TP, nl), lambda b, t: (b, 0, jnp.maximum(done(t) - 1, 0), 0)),
            pl.BlockSpec((None, groups, TP, nl), lambda b, t: (b, 0, done(t), 0)),
            pl.BlockSpec((None, N_HEADS * LANE, TP), lambda b, t: (b, 0, done(t))),
        ],
        out_shape=[
            jax.ShapeDtypeStruct((B, groups, S, nl), BF16),
            jax.ShapeDtypeStruct((B, groups, T, nl), BF16),
            jax.ShapeDtypeStruct((B, N_HEADS * LANE, T), BF16),
        ],
        scratch_shapes=[pltpu.VMEM((TP, nw), F32), pltpu.VMEM((TP, nw), F32)],
        compiler_params=_cparams(("arbitrary", "arbitrary")),
        name="attn_prep",
    )(x, ctx, g, sc, sh, csc, csh, tabs, win, wqb, wk, wv, qag, kvag, qng, kng)


def _attn_kernel(q_ref, k_ref, vt_ref, o_ref, s0, s1, p0, p1, m0, m1):
    s_buf, p_buf, m_buf = (s0, s1), (p0, p1), (m0, m1)

    def scores(h):
        sl = slice(h * LANE, (h + 1) * LANE)
        st = lax.dot_general(k_ref[:, sl], q_ref[:, sl], (((1,), (1,)), ((), ())),
                             preferred_element_type=F32)
        s_buf[h % 2][...] = st
        m_buf[h % 2][...] = jnp.max(st, axis=0, keepdims=True)

    scores(0)
    for h in range(HEADS_PER_STEP):
        sl = slice(h * LANE, (h + 1) * LANE)
        cur = h % 2
        if h + 1 < HEADS_PER_STEP:
            scores(h + 1)
        p_buf[cur][...] = jnp.exp2(s_buf[cur][...] - m_buf[cur][...]).astype(BF16)
        ot = jnp.dot(vt_ref[sl, :], p_buf[cur][...], preferred_element_type=F32)
        ot = ot[:HEAD, :] / ot[HEAD:HEAD + 1, :]
        if h % 2 == 0:
            even_head = ot
        else:
            pair = jnp.concatenate([even_head, ot], axis=0)
            o_ref[:, (h // 2) * LANE:(h // 2 + 1) * LANE] = pair.T.astype(BF16)


def _attention(q, k, vt):
    nl = HEADS_PER_STEP * LANE
    groups = N_HEADS // HEADS_PER_STEP
    return pl.pallas_call(
        _attn_kernel,
        grid=(B, groups, S // TQ),
        in_specs=[
            pl.BlockSpec((None, None, TQ, nl), lambda b, g, i: (b, g, i, 0)),
            pl.BlockSpec((None, None, T, nl), lambda b, g, i: (b, g, 0, 0)),
            pl.BlockSpec((None, nl, T), lambda b, g, i: (b, g, 0)),
        ],
        out_specs=pl.BlockSpec((None, None, TQ, HEADS_PER_STEP * HEAD), lambda b, g, i: (b, g, i, 0)),
        out_shape=jax.ShapeDtypeStruct((B, groups, S, HEADS_PER_STEP * HEAD), BF16),
        scratch_shapes=[pltpu.VMEM((T, TQ), F32), pltpu.VMEM((T, TQ), F32),
                        pltpu.VMEM((T, TQ), BF16), pltpu.VMEM((T, TQ), BF16),
                        pltpu.VMEM((1, TQ), F32), pltpu.VMEM((1, TQ), F32)],
        compiler_params=_cparams(("arbitrary", "arbitrary", "arbitrary")),
        name="attention",
    )(q, k, vt)


def _mlp_kernel(a_ref, wo_ref, g1_ref, x_ref, g_ref, sc_ref, sh_ref, gate_ref, w1_ref, w2_ref,
                fg_ref, o_ref, *, final, f_chunk):
    a = jnp.concatenate([a_ref[j] for j in range(a_ref.shape[0])], axis=1)
    x = x_ref[...] + g1_ref[...] * jnp.dot(a, wo_ref[...], preferred_element_type=F32)
    h = _rms_mod(x, g_ref[...], sc_ref[...], sh_ref[...]).astype(BF16)
    acc = jnp.zeros(x.shape, F32)
    for j in range(D_FF // f_chunk):
        sl = slice(j * f_chunk, (j + 1) * f_chunk)
        a = jnp.maximum(jnp.dot(h, w1_ref[:, sl], preferred_element_type=F32), 0.0)
        acc = acc + jnp.dot((a * a).astype(BF16), w2_ref[sl, :], preferred_element_type=F32)
    y = x + gate_ref[...] * acc
    if final:
        y = y * lax.rsqrt(jnp.mean(y * y, axis=-1, keepdims=True) + EPS) * fg_ref[...]
    o_ref[...] = y


def _mlp(a, wo, gate1, x, g, scale, shift, gate, w1, w2, final_g, final):
    const = lambda b, t: (0, 0)
    groups, width = a.shape[1], a.shape[3]
    return pl.pallas_call(
        functools.partial(_mlp_kernel, final=final, f_chunk=1024),
        grid=(B, S // TM),
        in_specs=[
            pl.BlockSpec((None, groups, TM, width), lambda b, t: (b, 0, t, 0)),
            pl.BlockSpec((D, D), const, pipeline_mode=pl.Buffered(1)),
            pl.BlockSpec((None, 1, D), lambda b, t: (b, 0, 0)),
            pl.BlockSpec((None, TM, D), lambda b, t: (b, t, 0)),
            pl.BlockSpec((1, D), const),
            pl.BlockSpec((None, 1, D), lambda b, t: (b, 0, 0)),
            pl.BlockSpec((None, 1, D), lambda b, t: (b, 0, 0)),
            pl.BlockSpec((None, 1, D), lambda b, t: (b, 0, 0)),
            pl.BlockSpec((D, D_FF), const, pipeline_mode=pl.Buffered(1)),
            pl.BlockSpec((D_FF, D), const, pipeline_mode=pl.Buffered(1)),
            pl.BlockSpec((1, D), const),
        ],
        out_specs=pl.BlockSpec((None, TM, D), lambda b, t: (b, t, 0)),
        out_shape=jax.ShapeDtypeStruct((B, S, D), F32),
        compiler_params=_cparams(("arbitrary", "arbitrary")),
        name="mlp",
    )(a, wo, gate1, x, g, scale, shift, gate, w1, w2, final_g)


def _build_trig(cos_dst, sin_dst, tw_ref, a0, nblk, b0_of_block):
    for i in range(nblk):
        ca = tw_ref[0, a0 + i:a0 + i + 1, :]
        sa = tw_ref[1, a0 + i:a0 + i + 1, :]
        b0 = b0_of_block(i)
        cb = tw_ref[0, b0:b0 + TWB, :]
        sb = tw_ref[1, b0:b0 + TWB, :]
        rs = slice(i * TWB, (i + 1) * TWB)
        cos_dst[rs, :] = (ca * cb - sa * sb).astype(BF16)
        sin_dst[rs, :] = (sa * cb + ca * sb).astype(BF16)


def _hyena_filter_kernel(feat_ref, w1_ref, b1_ref, w2_ref, b2_ref, w3_ref, b3_ref, fr_ref,
                         w4f0_ref, w4f1_ref, w4b0_ref, w4b1_ref, dl_ref, tw_ref,
                         kr_ref, ki_ref, co_ref, so_ref, hid_hi, hid_lo):
    @pl.when(pl.program_id(0) == 0)
    def _():
        nblk = S // TWB
        _build_trig(co_ref, so_ref, tw_ref, 0, nblk,
                    lambda i: nblk + TWB * (i // (nblk // 4)))

        hp = lax.Precision.HIGHEST
        hid = jnp.sin(fr_ref[0:1, :] * (jnp.dot(feat_ref[...], w1_ref[...], precision=hp,
                                                preferred_element_type=F32) + b1_ref[...]))
        hid = jnp.sin(fr_ref[1:2, :] * (jnp.dot(hid, w2_ref[...], precision=hp,
                                                preferred_element_type=F32) + b2_ref[...]))
        hid = jnp.sin(fr_ref[2:3, :] * (jnp.dot(hid, w3_ref[...], precision=hp,
                                                preferred_element_type=F32) + b3_ref[...]))
        hi = hid.astype(BF16)
        hid_hi[...] = hi
        hid_lo[...] = (hid - hi.astype(F32)).astype(BF16)

    def dot3(w_ref):
        w = w_ref[...]
        w_hi = w.astype(BF16)
        w_lo = (w - w_hi.astype(F32)).astype(BF16)
        return (jnp.dot(hid_hi[...], w_hi, preferred_element_type=F32)
                + jnp.dot(hid_hi[...], w_lo, preferred_element_type=F32)
                + jnp.dot(hid_lo[...], w_hi, preferred_element_type=F32))

    row = lax.broadcasted_iota(jnp.int32, (S, TC), 0)
    t_norm = row.astype(F32) / S
    window = jnp.exp(-t_norm * dl_ref[...])
    for order, (wf_ref, wb_ref) in enumerate(((w4f0_ref, w4b0_ref), (w4f1_ref, w4b1_ref))):
        hf = dot3(wf_ref) * window
        hb = dot3(wb_ref) * window
        ss = jnp.sum(hf * hf + hb * hb, axis=0, keepdims=True)
        nrm = lax.rsqrt(ss + EPS)
        hf = hf * nrm
        hb = jnp.where(row == 0, 0.0, hb * nrm)
        hsum = (hf + hb).astype(BF16)
        hdif = (hf - hb).astype(BF16)
        for r in range(4):
            rs = slice(r * (S // 4), (r + 1) * (S // 4))
            scale = (2.0 if r < 2 else 1.0) / NFFT
            kr_ref[order, rs, :] = scale * jnp.dot(co_ref[rs, :], hsum, preferred_element_type=F32)
            ki_ref[order, rs, :] = -scale * jnp.dot(so_ref[rs, :], hdif, preferred_element_type=F32)


def _hyena_filters(feats, w1, b1, w2, b2, w3, b3, freq, w4, deltas, tw):
    nc = D // TC
    const = lambda c: (0, 0)
    w4spec = lambda k: pl.BlockSpec((HY_HID, TC), lambda c, k=k: (0, k * nc + c))
    return pl.pallas_call(
        _hyena_filter_kernel,
        grid=(nc,),
        in_specs=[
            pl.BlockSpec((S, HY_EMB_PAD), const),
            pl.BlockSpec((HY_EMB_PAD, HY_HID), const),
            pl.BlockSpec((1, HY_HID), const),
            pl.BlockSpec((HY_HID, HY_HID), const),
            pl.BlockSpec((1, HY_HID), const),
            pl.BlockSpec((HY_HID, HY_HID), const),
            pl.BlockSpec((1, HY_HID), const),
            pl.BlockSpec((3, HY_HID), const),
            w4spec(0), w4spec(1), w4spec(2), w4spec(3),
            pl.BlockSpec((1, TC), lambda c: (0, c)),
            pl.BlockSpec(tw.shape, lambda c: (0, 0, 0), pipeline_mode=pl.Buffered(1)),
        ],
        out_specs=[
            pl.BlockSpec((2, S, TC), lambda c: (0, 0, c)),
            pl.BlockSpec((2, S, TC), lambda c: (0, 0, c)),
        ],
        out_shape=[
            jax.ShapeDtypeStruct((2, S, D), F32),
            jax.ShapeDtypeStruct((2, S, D), F32),
        ],
        scratch_shapes=[pltpu.VMEM((S, S), BF16), pltpu.VMEM((S, S), BF16),
                        pltpu.VMEM((S, HY_HID), BF16), pltpu.VMEM((S, HY_HID), BF16)],
        compiler_params=_cparams(("arbitrary",)),
        name="hyena_filters",
    )(feats, w1, b1, w2, b2, w3, b3, freq, w4, w4, w4, w4, deltas, tw)


def _hyena_conv_kernel(x1_ref, x2_ref, v_ref, cw1_ref, cw2_ref, cwv_ref, cb1_ref, cb2_ref,
                       cbv_ref, kr_ref, ki_ref, skip_ref, tw_ref, y_ref,
                       cos_tab, sin_tab, y_buf, g_buf, u_buf, ab_buf, p_buf, nat_buf):
    qs = S // 4
    nblk = qs // TWB
    lanes = [slice(l * LANE, (l + 1) * LANE) for l in range(TC // LANE)]

    @pl.when((pl.program_id(0) == 0) & (pl.program_id(1) == 0))
    def _():
        for fam in range(8):
            base = fam * (TW_PAD + TWB)
            _build_trig(cos_tab.at[fam], sin_tab.at[fam], tw_ref, base, nblk,
                        lambda i, base=base: base + TW_PAD)

    def phases(z_ref):
        for l, ls in enumerate(lanes):
            nat_buf[l] = z_ref[:, ls].astype(F32)
        return [jnp.concatenate([nat_buf[l, pl.ds(r, qs, stride=4), :] for l in range(len(lanes))],
                                axis=1) for r in range(4)]

    row = lax.broadcasted_iota(jnp.int32, (qs, TC), 0)

    def short_conv(z_ref, w_ref, b_ref, dst):
        z = phases(z_ref)
        before = jnp.where(row == 0, 0.0, pltpu.roll(z[3], 1, 0))
        after = jnp.where(row == qs - 1, 0.0, pltpu.roll(z[0], qs - 1, 0))
        w0, w1, w2, b = w_ref[0:1, :], w_ref[1:2, :], w_ref[2:3, :], b_ref[...]
        prev = [before, z[0], z[1], z[2]]
        nxt = [z[1], z[2], z[3], after]
        for r in range(4):
            dst[r] = b + prev[r] * w0 + z[r] * w1 + nxt[r] * w2

    def mm(tab, fam, rhs):
        return jnp.dot(tab[fam], rhs, preferred_element_type=F32)


    short_conv(v_ref, cwv_ref, cbv_ref, y_buf)
    gates = ((x1_ref, cw1_ref, cb1_ref), (x2_ref, cw2_ref, cb2_ref))
    for order, (z_ref, w_ref, b_ref) in enumerate(gates):
        u_buf[...] = y_buf[...].astype(BF16)
        for r in range(4):
            ab_buf[r] = mm(cos_tab, r, u_buf[r])
            ab_buf[4 + r] = mm(sin_tab, r, u_buf[r])
        for i in range(qs // SLAB):
            start = i * SLAB
            rs = slice(start, start + SLAB)
            a = [ab_buf[r, rs, :] for r in range(4)]
            b = [ab_buf[4 + r, rs, :] for r in range(4)]
            el_r, el_i = a[0] + a[2], -(b[0] + b[2])
            p, q = a[0] - a[2], b[0] - b[2]
            eh_r, eh_i = p + q, q - p
            ol_r, ol_i = a[1] + a[3], -(b[1] + b[3])
            p, q = a[1] - a[3], b[1] - b[3]
            oh_r, oh_i = q - p, -(p + q)
            xs = ((el_r + ol_r, el_i + ol_i), (ol_i - el_i, ol_r - el_r),
                  (eh_r + oh_r, eh_i + oh_i), (oh_i - eh_i, oh_r - eh_r))
            z = []
            for g, (xr, xi) in enumerate(xs):
                ks = slice(g * qs + start, g * qs + start + SLAB)
                kr, ki = kr_ref[order, ks, :], ki_ref[order, ks, :]
                z.append((xr * kr - xi * ki, xr * ki + xi * kr))
            el_r, ol_r = z[0][0] - z[1][1], z[0][0] + z[1][1]
            el_i, ol_i = z[0][1] - z[1][0], z[0][1] + z[1][0]
            eh_r, oh_r = z[2][0] - z[3][1], z[2][0] + z[3][1]
            eh_i, oh_i = z[2][1] - z[3][0], z[2][1] + z[3][0]
            p, q = eh_r - eh_i, eh_r + eh_i
            p2, q2 = -(oh_r + oh_i), oh_r - oh_i
            ah = (el_r + p, ol_r + p2, el_r - p, ol_r - p2)
            bh = (q - el_i, q2 - ol_i, -(el_i + q), -(ol_i + q2))
            for r in range(4):
                p_buf[r, rs, :] = ah[r].astype(BF16)
                p_buf[4 + r, rs, :] = bh[r].astype(BF16)
        short_conv(z_ref, w_ref, b_ref, g_buf)
        skip = skip_ref[order:order + 1, :]
        for r in range(4):
            conv = mm(cos_tab, 4 + r, p_buf[r]) + mm(sin_tab, 4 + r, p_buf[4 + r])
            y_buf[r] = g_buf[r] * (conv + y_buf[r] * skip)
    for l, ls in enumerate(lanes):
        for r in range(4):
            nat_buf[l, pl.ds(r, qs, stride=4), :] = y_buf[r, :, ls]
    for l, ls in enumerate(lanes):
        y_ref[:, ls] = nat_buf[l].astype(y_ref.dtype)


def _hyena_conv(z, conv_w, conv_b, kr, ki, skip, tw):
    nc = D // TC
    qs = S // 4
    zspec = lambda k: pl.BlockSpec((None, None, S, TC), lambda c, b, k=k: (b, k * nc + c, 0, 0))
    wspec = lambda k: pl.BlockSpec((3, TC), lambda c, b, k=k: (0, k * nc + c))
    bspec = lambda k: pl.BlockSpec((1, TC), lambda c, b, k=k: (0, k * nc + c))
    return pl.pallas_call(
        _hyena_conv_kernel,
        grid=(nc, B),
        in_specs=[
            zspec(0), zspec(1), zspec(2),
            wspec(0), wspec(1), wspec(2),
            bspec(0), bspec(1), bspec(2),
            pl.BlockSpec((2, S, TC), lambda c, b: (0, 0, c), pipeline_mode=pl.Buffered(1)),
            pl.BlockSpec((2, S, TC), lambda c, b: (0, 0, c), pipeline_mode=pl.Buffered(1)),
            pl.BlockSpec((2, TC), lambda c, b: (0, c)),
            pl.BlockSpec(tw.shape, lambda c, b: (0, 0, 0), pipeline_mode=pl.Buffered(1)),
        ],
        out_specs=pl.BlockSpec((None, None, S, TC), lambda c, b: (b, c, 0, 0)),
        out_shape=jax.ShapeDtypeStruct((B, nc, S, TC), BF16),
        scratch_shapes=[pltpu.VMEM((8, qs, qs), BF16), pltpu.VMEM((8, qs, qs), BF16),
                        pltpu.VMEM((4, qs, TC), F32), pltpu.VMEM((4, qs, TC), F32),
                        pltpu.VMEM((4, qs, TC), BF16), pltpu.VMEM((8, qs, TC), F32),
                        pltpu.VMEM((8, qs, TC), BF16), pltpu.VMEM((TC // LANE, S, LANE), F32)],
        compiler_params=_cparams(("arbitrary", "arbitrary")),
        name="hyena_conv",
    )(z, z, z, conv_w, conv_w, conv_w, conv_b, conv_b, conv_b, kr, ki, skip, tw)


def _pad_heads(w, heads, d, front=0):
    k = w.shape[0]
    w = w.reshape(k, heads, d)
    w = jnp.pad(w, ((0, 0), (0, 0), (front, LANE - d - front)))
    return w.reshape(k, heads * LANE)


@functools.lru_cache(maxsize=None)
def _rope_tables():
    pos = np.arange(S)
    rowf = (pos // GRID_W).astype(np.float64)
    colf = (pos % GRID_W).astype(np.float64)

    def pattern(base, half):
        n = half // 2
        inv = ROPE_THETA ** (-np.arange(n, dtype=np.float64) / n)
        cos_cols, lo_cols, hi_cols = [], [], []
        for p in (rowf, colf):
            ang = p[:, None] * inv[None]
            c, s = np.cos(ang), np.sin(ang)
            zero = np.zeros_like(s)
            cos_cols += [c, c]
            lo_cols += [-s, zero]
            hi_cols += [zero, s]
        width = 2 * half

        def place(cols, fill):
            body = np.concatenate(cols, axis=1)
            return np.concatenate([np.full((S, base), fill), body,
                                   np.full((S, LANE - base - width), fill)], axis=1)
        return place(cos_cols, 1.0), place(lo_cols, 0.0), place(hi_cols, 0.0)

    tabs = np.stack(pattern(HEAD, MLA_ROPE // 2) + pattern(0, HEAD // 2))
    ident = np.stack([np.ones((CTX, LANE)), np.zeros((CTX, LANE)), np.zeros((CTX, LANE))] * 2)
    return np.concatenate([ident, tabs], axis=1).astype(np.float32)


def _trig_factors(rows):
    ang = (rows % (4 * NFFT)).astype(np.float64) * (2.0 * math.pi / (4 * NFFT))
    return np.stack([np.cos(ang), np.sin(ang)]).astype(np.float32)


@functools.lru_cache(maxsize=None)
def _conv_trig_factors():
    qs = S // 4
    i = np.arange(TW_PAD)[:, None]
    j = np.arange(TWB)[:, None]
    c = np.arange(qs)[None, :]
    fams = []
    for r in range(4):
        odd = 2 * r + 1
        fams += [2 * TWB * i * (8 * c + odd), (2 * j + 1) * (8 * c + odd)]
    for r in range(4):
        odd = 2 * r + 1
        fams += [8 * TWB * i * (2 * c + 1), (8 * j + odd) * (2 * c + 1)]
    return _trig_factors(np.concatenate(fams, axis=0))


@functools.lru_cache(maxsize=None)
def _filter_trig_factors():
    nblk = S // TWB
    per_group = nblk // 4
    i = np.arange(nblk)[:, None]
    j = np.arange(TWB)[:, None]
    n = np.arange(S)[None, :]
    grp = i // per_group
    blk = 2 * TWB * (i % per_group) * 2 * n
    blk = np.where((grp == 0) | (grp == 3), blk, -blk)
    in_block = [(2 * j + 1) * 2 * n, (2 * S - 1 - 2 * j) * 2 * n,
                (S - 1 - 2 * j) * 2 * n, (S + 1 + 2 * j) * 2 * n]
    return _trig_factors(np.concatenate([blk] + in_block, axis=0))


@functools.lru_cache(maxsize=None)
def _hyena_features():
    t = np.arange(S, dtype=np.float64)
    t_norm = t / S
    w = 2.0 * math.pi * t / S
    bands = np.linspace(1e-4, HY_BANDS - 1, HY_BANDS)
    fw = w[:, None] * bands[None]
    feats = np.concatenate([t_norm[:, None], np.cos(fw), -np.sin(fw)], axis=-1)
    feats = np.pad(feats, ((0, 0), (0, HY_EMB_PAD - HY_EMB)))
    max_decay = math.log(1e-2) / 0.3
    min_decay = math.log(1e-2) / 1.5
    deltas = np.abs(np.linspace(min_decay, max_decay, D))
    return feats.astype(np.float32), deltas.reshape(1, D).astype(np.float32)


def kernel(x, c, ctx, c_ctx, w_mod, b_mod, norm1_g, norm2_g, mlp_w1, mlp_w2, a_w_in, a_q_a_g, a_w_q_b, a_kv_a_g, a_w_kv_b, a_q_norm_g, a_k_norm_g, a_w_out, h_w_in, h_conv_w, h_conv_b, h_f_w1, h_f_b1, h_f_w2, h_f_b2, h_f_w3, h_f_b3, h_f_freq, h_f_w4, h_skip, h_w_out, final_g):
    assert x.shape == (B, S, D) and ctx.shape == (B, CTX, D) and w_mod.shape[0] == 2

    c_rows = jnp.concatenate([c, c_ctx[None], jnp.zeros((MOD_ROWS - B - 1, D), F32)], axis=0)
    mods = _mods(c_rows, w_mod, b_mod)

    def lat_mod(i, k):
        return mods[i, :B, k * D:(k + 1) * D].reshape(B, 1, D)

    def ctx_mod(i, k):
        return mods[i, B:B + 1, k * D:(k + 1) * D]

    w_in = a_w_in[0]
    o_kr = Q_LORA + KV_LORA
    o_gq = o_kr + MLA_ROPE
    o_gk = o_gq + GQA_HEADS * HEAD
    o_gv = o_gk + GQA_KV * HEAD
    win = jnp.concatenate([
        w_in[:, :o_kr],
        _pad_heads(w_in[:, o_kr:o_gq], 1, MLA_ROPE, front=HEAD),
        _pad_heads(w_in[:, o_gq:o_gk], GQA_HEADS, HEAD),
        _pad_heads(w_in[:, o_gk:o_gv], GQA_KV, HEAD),
        _pad_heads(w_in[:, o_gv:], GQA_KV, HEAD),
    ], axis=1).astype(BF16)
    wqb = _pad_heads(a_w_q_b[0], MLA_HEADS, HEAD + MLA_ROPE).astype(BF16)
    wkv = a_w_kv_b[0].reshape(KV_LORA, MLA_HEADS, 2 * HEAD)
    wk = _pad_heads(wkv[:, :, :HEAD].reshape(KV_LORA, MLA_HEADS * HEAD), MLA_HEADS, HEAD).astype(BF16)
    wv = _pad_heads(wkv[:, :, HEAD:].reshape(KV_LORA, MLA_HEADS * HEAD), MLA_HEADS, HEAD).astype(BF16)
    qng = jnp.pad(a_q_norm_g[0], (0, LANE - HEAD)).reshape(1, LANE)
    kng = jnp.pad(a_k_norm_g[0], (0, LANE - HEAD)).reshape(1, LANE)

    q, k, vt = _attn_prep(x, ctx, norm1_g[0:1], lat_mod(0, 1), lat_mod(0, 0),
                          ctx_mod(0, 1), ctx_mod(0, 0), _rope_tables(), win, wqb, wk, wv,
                          a_q_a_g[0:1], a_kv_a_g[0:1], qng, kng)
    o = _attention(q, k, vt)
    x = _mlp(o, a_w_out[0].astype(BF16), lat_mod(0, 2), x,
             norm2_g[0:1], lat_mod(0, 4), lat_mod(0, 3), lat_mod(0, 5),
             mlp_w1[0].astype(BF16), mlp_w2[0].astype(BF16), final_g.reshape(1, D), final=False)

    feats, deltas = _hyena_features()
    kr, ki = _hyena_filters(
        feats, jnp.pad(h_f_w1[0], ((0, HY_EMB_PAD - HY_EMB), (0, 0))), h_f_b1[0:1],
        h_f_w2[0], h_f_b2[0:1], h_f_w3[0], h_f_b3[0:1], h_f_freq[0], h_f_w4[0], deltas,
        _filter_trig_factors())
    z = _modmm(x, norm1_g[1:2], lat_mod(1, 1), lat_mod(1, 0), h_w_in[0].astype(BF16))
    y = _hyena_conv(z, h_conv_w[0], h_conv_b[0:1], kr, ki, h_skip[0], _conv_trig_factors())
    x = _mlp(y, h_w_out[0].astype(BF16), lat_mod(1, 2), x,
             norm2_g[1:2], lat_mod(1, 4), lat_mod(1, 3), lat_mod(1, 5),
             mlp_w1[1].astype(BF16), mlp_w2[1].astype(BF16), final_g.reshape(1, D), final=True)
    return x
```

```python
import functools
import math

import jax
import jax.numpy as jnp
import numpy as np
from jax import lax
from jax.experimental import pallas as pl
from jax.experimental.pallas import tpu as pltpu

F32 = jnp.float32
BF16 = jnp.bfloat16

D = 1024
B = 16
S = 2048
CTX = 256
T = CTX + S
GRID_W = 64
D_FF = 4 * D
N_MOD = 6
HEAD = 64
MLA_HEADS = 8
MLA_ROPE = 32
Q_LORA = 384
KV_LORA = 256
GQA_HEADS = 8
GQA_KV = 2
N_HEADS = MLA_HEADS + GQA_HEADS
LANE = 128
LOG2E = 1.4426950408889634
ROPE_THETA = 10000.0
EPS = 1e-6
HY_BANDS = 8
HY_EMB = 1 + 2 * HY_BANDS
HY_EMB_PAD = 32
HY_HID = 64
NFFT = 2 * S

VMEM_LIMIT = 60 * 1024 * 1024

MOD_ROWS = 24
TM = 1024
TP = 256
TQ = 1024
HEADS_PER_STEP = 8
TC = 256
TWB = 128
TW_PAD = 8
SLAB = 32


def _cparams(sem):
    return pltpu.CompilerParams(dimension_semantics=sem, vmem_limit_bytes=VMEM_LIMIT)


def _rms_mod(x, g, scale, shift):
    ms = jnp.mean(x * x, axis=-1, keepdims=True)
    return x * lax.rsqrt(ms + EPS) * (g * (1.0 + scale)) + shift


def _mods_kernel(c_ref, w_ref, b_ref, o_ref):
    c = c_ref[...]
    s = c * (1.0 / (1.0 + jnp.exp(-c)))
    o_ref[...] = jnp.dot(s.astype(BF16), w_ref[...].astype(BF16),
                         preferred_element_type=F32) + b_ref[...]


def _mods(c_rows, w_mod, b_mod):
    depth = w_mod.shape[0]
    tn = 1024
    return pl.pallas_call(
        _mods_kernel,
        grid=(depth, N_MOD * D // tn),
        in_specs=[
            pl.BlockSpec((MOD_ROWS, D), lambda i, j: (0, 0)),
            pl.BlockSpec((None, D, tn), lambda i, j: (i, 0, j)),
            pl.BlockSpec((None, 1, tn), lambda i, j: (i, 0, j)),
        ],
        out_specs=pl.BlockSpec((None, MOD_ROWS, tn), lambda i, j: (i, 0, j)),
        out_shape=jax.ShapeDtypeStruct((depth, MOD_ROWS, N_MOD * D), F32),
        compiler_params=_cparams(("arbitrary", "arbitrary")),
        name="mods",
    )(c_rows, w_mod, b_mod.reshape(depth, 1, N_MOD * D))


def _modmm_kernel(x_ref, g_ref, sc_ref, sh_ref, w_ref, o_ref, *, n_chunk):
    h = _rms_mod(x_ref[...], g_ref[...], sc_ref[...], sh_ref[...]).astype(BF16)
    n = w_ref.shape[1]
    per = n_chunk // TC
    for j in range(n // n_chunk):
        sl = slice(j * n_chunk, (j + 1) * n_chunk)
        y = jnp.dot(h, w_ref[:, sl], preferred_element_type=F32).astype(o_ref.dtype)
        for c in range(per):
            o_ref[j * per + c] = y[:, c * TC:(c + 1) * TC]


def _modmm(x, g, scale, shift, w):
    n = w.shape[1]
    return pl.pallas_call(
        functools.partial(_modmm_kernel, n_chunk=1024),
        grid=(B, S // TM),
        in_specs=[
            pl.BlockSpec((None, TM, D), lambda b, t: (b, t, 0)),
            pl.BlockSpec((1, D), lambda b, t: (0, 0)),
            pl.BlockSpec((None, 1, D), lambda b, t: (b, 0, 0)),
            pl.BlockSpec((None, 1, D), lambda b, t: (b, 0, 0)),
            pl.BlockSpec((D, n), lambda b, t: (0, 0)),
        ],
        out_specs=pl.BlockSpec((None, n // TC, TM, TC), lambda b, t: (b, 0, t, 0)),
        out_shape=jax.ShapeDtypeStruct((B, n // TC, S, TC), BF16),
        compiler_params=_cparams(("arbitrary", "arbitrary")),
        name="modmm",
    )(x, g, scale, shift, w)


def _rope_tiles(xs, cos, sin_lo, sin_hi, n):
    lo = [pltpu.roll(x, LANE - n, 1) for x in xs]
    hi = [pltpu.roll(x, n, 1) for x in xs]
    return [x * cos + a * sin_lo + b * sin_hi for x, a, b in zip(xs, lo, hi)]


def _attn_prep_kernel(*refs):
    *io, z_even, z_odd = refs
    t = pl.program_id(1)

    @pl.when((pl.program_id(0) == 0) & (t == 0))
    def _():
        z_odd[...] = jnp.zeros(z_odd.shape, F32)

    @pl.when(t % 2 == 0)
    def _():
        _attn_prep_step(*io, z_even, z_odd)

    @pl.when(t % 2 == 1)
    def _():
        _attn_prep_step(*io, z_odd, z_even)


def _attn_prep_step(x_ref, ctx_ref, g_ref, sc_ref, sh_ref, csc_ref, csh_ref, tab_ref,
                    win_ref, wqb_ref, wk_ref, wv_ref, qag_ref, kvag_ref, qng_ref, kng_ref,
                    q_ref, k_ref, vt_ref, z_next, z):
    t = pl.program_id(1)
    is_ctx = t == 0
    src = jnp.where(is_ctx, ctx_ref[...], x_ref[...])
    scale = jnp.where(is_ctx, csc_ref[...], sc_ref[...])
    shift = jnp.where(is_ctx, csh_ref[...], sh_ref[...])
    h = _rms_mod(src, g_ref[...], scale, shift).astype(BF16)
    z_next[...] = jnp.dot(h, win_ref[...], preferred_element_type=F32)

    cos_m, slo_m, shi_m = tab_ref[0], tab_ref[1], tab_ref[2]
    cos_g, slo_g, shi_g = tab_ref[3], tab_ref[4], tab_ref[5]
    ones_col = (lax.broadcasted_iota(jnp.int32, (1, LANE), 1) == HEAD).astype(F32)

    o_kr = Q_LORA + KV_LORA
    o_gq = o_kr + LANE
    o_gk = o_gq + GQA_HEADS * LANE
    o_gv = o_gk + GQA_KV * LANE

    def rms(v, g):
        return v * lax.rsqrt(jnp.mean(v * v, axis=-1, keepdims=True) + EPS) * g

    def tile(v, off, i):
        return v[:, off + i * LANE:off + (i + 1) * LANE]

    s_mla = LOG2E / math.sqrt(HEAD + MLA_ROPE)
    s_gqa = LOG2E / math.sqrt(HEAD)

    cq = rms(z[:, :Q_LORA], qag_ref[...] * s_mla).astype(BF16)
    ckv = rms(z[:, Q_LORA:o_kr], kvag_ref[...]).astype(BF16)
    q = jnp.dot(cq, wqb_ref[...], preferred_element_type=F32)
    kn = jnp.dot(ckv, wk_ref[...], preferred_element_type=F32)
    vm = jnp.dot(ckv, wv_ref[...], preferred_element_type=F32)

    gqk = [tile(z, o_gq, i) for i in range(GQA_HEADS)] + [tile(z, o_gk, j) for j in range(GQA_KV)]
    gains = [qng_ref[...] * s_gqa] * GQA_HEADS + [kng_ref[...]] * GQA_KV
    ssq = [jnp.sum(v * v, axis=-1, keepdims=True) for v in gqk]
    gqk = [v * lax.rsqrt(s * (1.0 / HEAD) + EPS) * g for v, s, g in zip(gqk, ssq, gains)]
    gqk = _rope_tiles(gqk, cos_g, slo_g, shi_g, HEAD // 4)

    mla = [tile(q, 0, i) for i in range(MLA_HEADS)] + [z[:, o_kr:o_gq]]
    mla = _rope_tiles(mla, cos_m, slo_m, shi_m, MLA_ROPE // 4)
    kr = mla[MLA_HEADS]

    def put(ref, head, val):
        grp, hh = divmod(head, HEADS_PER_STEP)
        ref[grp, :, hh * LANE:(hh + 1) * LANE] = val

    for hh in range(MLA_HEADS):
        sl = slice(hh * LANE, (hh + 1) * LANE)
        put(q_ref, hh, mla[hh].astype(BF16))
        put(k_ref, hh, (kn[:, sl] + kr).astype(BF16))
        vt_ref[sl, :] = (vm[:, sl] + ones_col).T.astype(BF16)
    for hh in range(GQA_HEADS):
        put(q_ref, MLA_HEADS + hh, gqk[hh].astype(BF16))
    rep = GQA_HEADS // GQA_KV
    for j in range(GQA_KV):
        gk = gqk[GQA_HEADS + j].astype(BF16)
        gvt = (tile(z, o_gv, j) + ones_col).T.astype(BF16)
        for r in range(rep):
            hh = MLA_HEADS + j * rep + r
            put(k_ref, hh, gk)
            vt_ref[hh * LANE:(hh + 1) * LANE, :] = gvt


def _attn_prep(x, ctx, g, sc, sh, csc, csh, tabs, win, wqb, wk, wv, qag, kvag, qng, kng):
    nw = win.shape[1]
    n_tiles = T // TP
    lat = lambda b, t: (b, jnp.clip(t - 1, 0, n_tiles - 2), 0)
    done = lambda t: jnp.maximum(t - 1, 0)
    groups = N_HEADS // HEADS_PER_STEP
    nl = HEADS_PER_STEP * LANE
    full2 = lambda b, t: (0, 0)
    return pl.pallas_call(
        _attn_prep_kernel,
        grid=(B, n_tiles + 1),
        in_specs=[
            pl.BlockSpec((None, TP, D), lat),
            pl.BlockSpec((None, CTX, D), lambda b, t: (b, 0, 0)),
            pl.BlockSpec((1, D), full2),
            pl.BlockSpec((None, 1, D), lambda b, t: (b, 0, 0)),
            pl.BlockSpec((None, 1, D), lambda b, t: (b, 0, 0)),
            pl.BlockSpec((1, D), full2),
            pl.BlockSpec((1, D), full2),
            pl.BlockSpec((6, TP, LANE), lambda b, t: (0, done(t), 0)),
            pl.BlockSpec((D, nw), full2),
            pl.BlockSpec((Q_LORA, MLA_HEADS * LANE), full2),
            pl.BlockSpec((KV_LORA, MLA_HEADS * LANE), full2),
            pl.BlockSpec((KV_LORA, MLA_HEADS * LANE), full2),
            pl.BlockSpec((1, Q_LORA), full2),
            pl.BlockSpec((1, KV_LORA), full2),
            pl.BlockSpec((1, LANE), full2),
            pl.BlockSpec((1, LANE), full2),
        ],
        out_specs=[
            pl.BlockSpec((None, groups, TP, nl), lambda b, t: (b, 0, jnp.maximum(done(t) - 1, 0), 0)),
            pl.BlockSpec((None, groups, TP, nl), lambda b, t: (b, 0, done(t), 0)),
            pl.BlockSpec((None, N_HEADS * LANE, TP), lambda b, t: (b, 0, done(t))),
        ],
        out_shape=[
            jax.ShapeDtypeStruct((B, groups, S, nl), BF16),
            jax.ShapeDtypeStruct((B, groups, T, nl), BF16),
            jax.ShapeDtypeStruct((B, N_HEADS * LANE, T), BF16),
        ],
        scratch_shapes=[pltpu.VMEM((TP, nw), F32), pltpu.VMEM((TP, nw), F32)],
        compiler_params=_cparams(("arbitrary", "arbitrary")),
        name="attn_prep",
    )(x, ctx, g, sc, sh, csc, csh, tabs, win, wqb, wk, wv, qag, kvag, qng, kng)


def _attn_kernel(q_ref, k_ref, vt_ref, o_ref, s0, s1, p0, p1, m0, m1):
    s_buf, p_buf, m_buf = (s0, s1), (p0, p1), (m0, m1)

    def scores(h):
        sl = slice(h * LANE, (h + 1) * LANE)
        st = lax.dot_general(k_ref[:, sl], q_ref[:, sl], (((1,), (1,)), ((), ())),
                             preferred_element_type=F32)
        s_buf[h % 2][...] = st
        m_buf[h % 2][...] = jnp.max(st, axis=0, keepdims=True)

    scores(0)
    for h in range(HEADS_PER_STEP):
        sl = slice(h * LANE, (h + 1) * LANE)
        cur = h % 2
        if h + 1 < HEADS_PER_STEP:
            scores(h + 1)
        p_buf[cur][...] = jnp.exp2(s_buf[cur][...] - m_buf[cur][...]).astype(BF16)
        ot = jnp.dot(vt_ref[sl, :], p_buf[cur][...], preferred_element_type=F32)
        ot = ot[:HEAD, :] / ot[HEAD:HEAD + 1, :]
        if h % 2 == 0:
            even_head = ot
        else:
            pair = jnp.concatenate([even_head, ot], axis=0)
            o_ref[:, (h // 2) * LANE:(h // 2 + 1) * LANE] = pair.T.astype(BF16)


def _attention(q, k, vt):
    nl = HEADS_PER_STEP * LANE
    groups = N_HEADS // HEADS_PER_STEP
    return pl.pallas_call(
        _attn_kernel,
        grid=(B, groups, S // TQ),
        in_specs=[
            pl.BlockSpec((None, None, TQ, nl), lambda b, g, i: (b, g, i, 0)),
            pl.BlockSpec((None, None, T, nl), lambda b, g, i: (b, g, 0, 0)),
            pl.BlockSpec((None, nl, T), lambda b, g, i: (b, g, 0)),
        ],
        out_specs=pl.BlockSpec((None, None, TQ, HEADS_PER_STEP * HEAD), lambda b, g, i: (b, g, i, 0)),
        out_shape=jax.ShapeDtypeStruct((B, groups, S, HEADS_PER_STEP * HEAD), BF16),
        scratch_shapes=[pltpu.VMEM((T, TQ), F32), pltpu.VMEM((T, TQ), F32),
                        pltpu.VMEM((T, TQ), BF16), pltpu.VMEM((T, TQ), BF16),
                        pltpu.VMEM((1, TQ), F32), pltpu.VMEM((1, TQ), F32)],
        compiler_params=_cparams(("arbitrary", "arbitrary", "arbitrary")),
        name="attention",
    )(q, k, vt)


def _mlp_kernel(a_ref, wo_ref, g1_ref, x_ref, g_ref, sc_ref, sh_ref, gate_ref, w1_ref, w2_ref,
                fg_ref, o_ref, *, final, f_chunk):
    a = jnp.concatenate([a_ref[j] for j in range(a_ref.shape[0])], axis=1)
    x = x_ref[...] + g1_ref[...] * jnp.dot(a, wo_ref[...], preferred_element_type=F32)
    h = _rms_mod(x, g_ref[...], sc_ref[...], sh_ref[...]).astype(BF16)
    acc = jnp.zeros(x.shape, F32)
    for j in range(D_FF // f_chunk):
        sl = slice(j * f_chunk, (j + 1) * f_chunk)
        a = jnp.maximum(jnp.dot(h, w1_ref[:, sl], preferred_element_type=F32), 0.0)
        acc = acc + jnp.dot((a * a).astype(BF16), w2_ref[sl, :], preferred_element_type=F32)
    y = x + gate_ref[...] * acc
    if final:
        y = y * lax.rsqrt(jnp.mean(y * y, axis=-1, keepdims=True) + EPS) * fg_ref[...]
    o_ref[...] = y


def _mlp(a, wo, gate1, x, g, scale, shift, gate, w1, w2, final_g, final):
    const = lambda b, t: (0, 0)
    groups, width = a.shape[1], a.shape[3]
    return pl.pallas_call(
        functools.partial(_mlp_kernel, final=final, f_chunk=1024),
        grid=(B, S // TM),
        in_specs=[
            pl.BlockSpec((None, groups, TM, width), lambda b, t: (b, 0, t, 0)),
            pl.BlockSpec((D, D), const, pipeline_mode=pl.Buffered(1)),
            pl.BlockSpec((None, 1, D), lambda b, t: (b, 0, 0)),
            pl.BlockSpec((None, TM, D), lambda b, t: (b, t, 0)),
            pl.BlockSpec((1, D), const),
            pl.BlockSpec((None, 1, D), lambda b, t: (b, 0, 0)),
            pl.BlockSpec((None, 1, D), lambda b, t: (b, 0, 0)),
            pl.BlockSpec((None, 1, D), lambda b, t: (b, 0, 0)),
            pl.BlockSpec((D, D_FF), const, pipeline_mode=pl.Buffered(1)),
            pl.BlockSpec((D_FF, D), const, pipeline_mode=pl.Buffered(1)),
            pl.BlockSpec((1, D), const),
        ],
        out_specs=pl.BlockSpec((None, TM, D), lambda b, t: (b, t, 0)),
        out_shape=jax.ShapeDtypeStruct((B, S, D), F32),
        compiler_params=_cparams(("arbitrary", "arbitrary")),
        name="mlp",
    )(a, wo, gate1, x, g, scale, shift, gate, w1, w2, final_g)


def _build_trig(cos_dst, sin_dst, tw_ref, a0, nblk, b0_of_block):
    for i in range(nblk):
        ca = tw_ref[0, a0 + i:a0 + i + 1, :]
        sa = tw_ref[1, a0 + i:a0 + i + 1, :]
        b0 = b0_of_block(i)
        cb = tw_ref[0, b0:b0 + TWB, :]
        sb = tw_ref[1, b0:b0 + TWB, :]
        rs = slice(i * TWB, (i + 1) * TWB)
        cos_dst[rs, :] = (ca * cb - sa * sb).astype(BF16)
        sin_dst[rs, :] = (sa * cb + ca * sb).astype(BF16)


def _hyena_filter_kernel(feat_ref, w1_ref, b1_ref, w2_ref, b2_ref, w3_ref, b3_ref, fr_ref,
                         w4f0_ref, w4f1_ref, w4b0_ref, w4b1_ref, dl_ref, tw_ref,
                         kr_ref, ki_ref, co_ref, so_ref, hid_hi, hid_lo):
    @pl.when(pl.program_id(0) == 0)
    def _():
        nblk = S // TWB
        _build_trig(co_ref, so_ref, tw_ref, 0, nblk,
                    lambda i: nblk + TWB * (i // (nblk // 4)))

        hp = lax.Precision.HIGHEST
        hid = jnp.sin(fr_ref[0:1, :] * (jnp.dot(feat_ref[...], w1_ref[...], precision=hp,
                                                preferred_element_type=F32) + b1_ref[...]))
        hid = jnp.sin(fr_ref[1:2, :] * (jnp.dot(hid, w2_ref[...], precision=hp,
                                                preferred_element_type=F32) + b2_ref[...]))
        hid = jnp.sin(fr_ref[2:3, :] * (jnp.dot(hid, w3_ref[...], precision=hp,
                                                preferred_element_type=F32) + b3_ref[...]))
        hi = hid.astype(BF16)
        hid_hi[...] = hi
        hid_lo[...] = (hid - hi.astype(F32)).astype(BF16)

    def dot3(w_ref):
        w = w_ref[...]
        w_hi = w.astype(BF16)
        w_lo = (w - w_hi.astype(F32)).astype(BF16)
        return (jnp.dot(hid_hi[...], w_hi, preferred_element_type=F32)
                + jnp.dot(hid_hi[...], w_lo, preferred_element_type=F32)
                + jnp.dot(hid_lo[...], w_hi, preferred_element_type=F32))

    row = lax.broadcasted_iota(jnp.int32, (S, TC), 0)
    t_norm = row.astype(F32) / S
    window = jnp.exp(-t_norm * dl_ref[...])
    for order, (wf_ref, wb_ref) in enumerate(((w4f0_ref, w4b0_ref), (w4f1_ref, w4b1_ref))):
        hf = dot3(wf_ref) * window
        hb = dot3(wb_ref) * window
        ss = jnp.sum(hf * hf + hb * hb, axis=0, keepdims=True)
        nrm = lax.rsqrt(ss + EPS)
        hf = hf * nrm
        hb = jnp.where(row == 0, 0.0, hb * nrm)
        hsum = (hf + hb).astype(BF16)
        hdif = (hf - hb).astype(BF16)
        for r in range(4):
            rs = slice(r * (S // 4), (r + 1) * (S // 4))
            scale = (2.0 if r < 2 else 1.0) / NFFT
            kr_ref[order, rs, :] = scale * jnp.dot(co_ref[rs, :], hsum, preferred_element_type=F32)
            ki_ref[order, rs, :] = -scale * jnp.dot(so_ref[rs, :], hdif, preferred_element_type=F32)


def _hyena_filters(feats, w1, b1, w2, b2, w3, b3, freq, w4, deltas, tw):
    nc = D // TC
    const = lambda c: (0, 0)
    w4spec = lambda k: pl.BlockSpec((HY_HID, TC), lambda c, k=k: (0, k * nc + c))
    return pl.pallas_call(
        _hyena_filter_kernel,
        grid=(nc,),
        in_specs=[
            pl.BlockSpec((S, HY_EMB_PAD), const),
            pl.BlockSpec((HY_EMB_PAD, HY_HID), const),
            pl.BlockSpec((1, HY_HID), const),
            pl.BlockSpec((HY_HID, HY_HID), const),
            pl.BlockSpec((1, HY_HID), const),
            pl.BlockSpec((HY_HID, HY_HID), const),
            pl.BlockSpec((1, HY_HID), const),
            pl.BlockSpec((3, HY_HID), const),
            w4spec(0), w4spec(1), w4spec(2), w4spec(3),
            pl.BlockSpec((1, TC), lambda c: (0, c)),
            pl.BlockSpec(tw.shape, lambda c: (0, 0, 0), pipeline_mode=pl.Buffered(1)),
        ],
        out_specs=[
            pl.BlockSpec((2, S, TC), lambda c: (0, 0, c)),
            pl.BlockSpec((2, S, TC), lambda c: (0, 0, c)),
        ],
        out_shape=[
            jax.ShapeDtypeStruct((2, S, D), F32),
            jax.ShapeDtypeStruct((2, S, D), F32),
        ],
        scratch_shapes=[pltpu.VMEM((S, S), BF16), pltpu.VMEM((S, S), BF16),
                        pltpu.VMEM((S, HY_HID), BF16), pltpu.VMEM((S, HY_HID), BF16)],
        compiler_params=_cparams(("arbitrary",)),
        name="hyena_filters",
    )(feats, w1, b1, w2, b2, w3, b3, freq, w4, w4, w4, w4, deltas, tw)


def _hyena_conv_kernel(x1_ref, x2_ref, v_ref, cw1_ref, cw2_ref, cwv_ref, cb1_ref, cb2_ref,
                       cbv_ref, kr_ref, ki_ref, skip_ref, tw_ref, y_ref,
                       cos_tab, sin_tab, y_buf, g_buf, u_buf, ab_buf, p_buf, nat_buf):
    qs = S // 4
    nblk = qs // TWB
    lanes = [slice(l * LANE, (l + 1) * LANE) for l in range(TC // LANE)]

    @pl.when((pl.program_id(0) == 0) & (pl.program_id(1) == 0))
    def _():
        for fam in range(8):
            base = fam * (TW_PAD + TWB)
            _build_trig(cos_tab.at[fam], sin_tab.at[fam], tw_ref, base, nblk,
                        lambda i, base=base: base + TW_PAD)

    def phases(z_ref):
        for l, ls in enumerate(lanes):
            nat_buf[l] = z_ref[:, ls].astype(F32)
        return [jnp.concatenate([nat_buf[l, pl.ds(r, qs, stride=4), :] for l in range(len(lanes))],
                                axis=1) for r in range(4)]

    row = lax.broadcasted_iota(jnp.int32, (qs, TC), 0)

    def short_conv(z_ref, w_ref, b_ref, dst):
        z = phases(z_ref)
        before = jnp.where(row == 0, 0.0, pltpu.roll(z[3], 1, 0))
        after = jnp.where(row == qs - 1, 0.0, pltpu.roll(z[0], qs - 1, 0))
        w0, w1, w2, b = w_ref[0:1, :], w_ref[1:2, :], w_ref[2:3, :], b_ref[...]
        prev = [before, z[0], z[1], z[2]]
        nxt = [z[1], z[2], z[3], after]
        for r in range(4):
            dst[r] = b + prev[r] * w0 + z[r] * w1 + nxt[r] * w2

    def mm(tab, fam, rhs):
        return jnp.dot(tab[fam], rhs, preferred_element_type=F32)


    short_conv(v_ref, cwv_ref, cbv_ref, y_buf)
    gates = ((x1_ref, cw1_ref, cb1_ref), (x2_ref, cw2_ref, cb2_ref))
    for order, (z_ref, w_ref, b_ref) in enumerate(gates):
        u_buf[...] = y_buf[...].astype(BF16)
        for r in range(4):
            ab_buf[r] = mm(cos_tab, r, u_buf[r])
            ab_buf[4 + r] = mm(sin_tab, r, u_buf[r])
        for i in range(qs // SLAB):
            start = i * SLAB
            rs = slice(start, start + SLAB)
            a = [ab_buf[r, rs, :] for r in range(4)]
            b = [ab_buf[4 + r, rs, :] for r in range(4)]
            el_r, el_i = a[0] + a[2], -(b[0] + b[2])
            p, q = a[0] - a[2], b[0] - b[2]
            eh_r, eh_i = p + q, q - p
            ol_r, ol_i = a[1] + a[3], -(b[1] + b[3])
            p, q = a[1] - a[3], b[1] - b[3]
            oh_r, oh_i = q - p, -(p + q)
            xs = ((el_r + ol_r, el_i + ol_i), (ol_i - el_i, ol_r - el_r),
                  (eh_r + oh_r, eh_i + oh_i), (oh_i - eh_i, oh_r - eh_r))
            z = []
            for g, (xr, xi) in enumerate(xs):
                ks = slice(g * qs + start, g * qs + start + SLAB)
                kr, ki = kr_ref[order, ks, :], ki_ref[order, ks, :]
                z.append((xr * kr - xi * ki, xr * ki + xi * kr))
            el_r, ol_r = z[0][0] - z[1][1], z[0][0] + z[1][1]
            el_i, ol_i = z[0][1] - z[1][0], z[0][1] + z[1][0]
            eh_r, oh_r = z[2][0] - z[3][1], z[2][0] + z[3][1]
            eh_i, oh_i = z[2][1] - z[3][0], z[2][1] + z[3][0]
            p, q = eh_r - eh_i, eh_r + eh_i
            p2, q2 = -(oh_r + oh_i), oh_r - oh_i
            ah = (el_r + p, ol_r + p2, el_r - p, ol_r - p2)
            bh = (q - el_i, q2 - ol_i, -(el_i + q), -(ol_i + q2))
            for r in range(4):
                p_buf[r, rs, :] = ah[r].astype(BF16)
                p_buf[4 + r, rs, :] = bh[r].astype(BF16)
        short_conv(z_ref, w_ref, b_ref, g_buf)
        skip = skip_ref[order:order + 1, :]
        for r in range(4):
            conv = mm(cos_tab, 4 + r, p_buf[r]) + mm(sin_tab, 4 + r, p_buf[4 + r])
            y_buf[r] = g_buf[r] * (conv + y_buf[r] * skip)
    for l, ls in enumerate(lanes):
        for r in range(4):
            nat_buf[l, pl.ds(r, qs, stride=4), :] = y_buf[r, :, ls]
    for l, ls in enumerate(lanes):
        y_ref[:, ls] = nat_buf[l].astype(y_ref.dtype)


def _hyena_conv(z, conv_w, conv_b, kr, ki, skip, tw):
    nc = D // TC
    qs = S // 4
    zspec = lambda k: pl.BlockSpec((None, None, S, TC), lambda c, b, k=k: (b, k * nc + c, 0, 0))
    wspec = lambda k: pl.BlockSpec((3, TC), lambda c, b, k=k: (0, k * nc + c))
    bspec = lambda k: pl.BlockSpec((1, TC), lambda c, b, k=k: (0, k * nc + c))
    return pl.pallas_call(
        _hyena_conv_kernel,
        grid=(nc, B),
        in_specs=[
            zspec(0), zspec(1), zspec(2),
            wspec(0), wspec(1), wspec(2),
            bspec(0), bspec(1), bspec(2),
            pl.BlockSpec((2, S, TC), lambda c, b: (0, 0, c), pipeline_mode=pl.Buffered(1)),
            pl.BlockSpec((2, S, TC), lambda c, b: (0, 0, c), pipeline_mode=pl.Buffered(1)),
            pl.BlockSpec((2, TC), lambda c, b: (0, c)),
            pl.BlockSpec(tw.shape, lambda c, b: (0, 0, 0), pipeline_mode=pl.Buffered(1)),
        ],
        out_specs=pl.BlockSpec((None, None, S, TC), lambda c, b: (b, c, 0, 0)),
        out_shape=jax.ShapeDtypeStruct((B, nc, S, TC), BF16),
        scratch_shapes=[pltpu.VMEM((8, qs, qs), BF16), pltpu.VMEM((8, qs, qs), BF16),
                        pltpu.VMEM((4, qs, TC), F32), pltpu.VMEM((4, qs, TC), F32),
                        pltpu.VMEM((4, qs, TC), BF16), pltpu.VMEM((8, qs, TC), F32),
                        pltpu.VMEM((8, qs, TC), BF16), pltpu.VMEM((TC // LANE, S, LANE), F32)],
        compiler_params=_cparams(("arbitrary", "arbitrary")),
        name="hyena_conv",
    )(z, z, z, conv_w, conv_w, conv_w, conv_b, conv_b, conv_b, kr, ki, skip, tw)


def _pad_heads(w, heads, d, front=0):
    k = w.shape[0]
    w = w.reshape(k, heads, d)
    w = jnp.pad(w, ((0, 0), (0, 0), (front, LANE - d - front)))
    return w.reshape(k, heads * LANE)


@functools.lru_cache(maxsize=None)
def _rope_tables():
    pos = np.arange(S)
    rowf = (pos // GRID_W).astype(np.float64)
    colf = (pos % GRID_W).astype(np.float64)

    def pattern(base, half):
        n = half // 2
        inv = ROPE_THETA ** (-np.arange(n, dtype=np.float64) / n)
        cos_cols, lo_cols, hi_cols = [], [], []
        for p in (rowf, colf):
            ang = p[:, None] * inv[None]
            c, s = np.cos(ang), np.sin(ang)
            zero = np.zeros_like(s)
            cos_cols += [c, c]
            lo_cols += [-s, zero]
            hi_cols += [zero, s]
        width = 2 * half

        def place(cols, fill):
            body = np.concatenate(cols, axis=1)
            return np.concatenate([np.full((S, base), fill), body,
                                   np.full((S, LANE - base - width), fill)], axis=1)
        return place(cos_cols, 1.0), place(lo_cols, 0.0), place(hi_cols, 0.0)

    tabs = np.stack(pattern(HEAD, MLA_ROPE // 2) + pattern(0, HEAD // 2))
    ident = np.stack([np.ones((CTX, LANE)), np.zeros((CTX, LANE)), np.zeros((CTX, LANE))] * 2)
    return np.concatenate([ident, tabs], axis=1).astype(np.float32)


def _trig_factors(rows):
    ang = (rows % (4 * NFFT)).astype(np.float64) * (2.0 * math.pi / (4 * NFFT))
    return np.stack([np.cos(ang), np.sin(ang)]).astype(np.float32)


@functools.lru_cache(maxsize=None)
def _conv_trig_factors():
    qs = S // 4
    i = np.arange(TW_PAD)[:, None]
    j = np.arange(TWB)[:, None]
    c = np.arange(qs)[None, :]
    fams = []
    for r in range(4):
        odd = 2 * r + 1
        fams += [2 * TWB * i * (8 * c + odd), (2 * j + 1) * (8 * c + odd)]
    for r in range(4):
        odd = 2 * r + 1
        fams += [8 * TWB * i * (2 * c + 1), (8 * j + odd) * (2 * c + 1)]
    return _trig_factors(np.concatenate(fams, axis=0))


@functools.lru_cache(maxsize=None)
def _filter_trig_factors():
    nblk = S // TWB
    per_group = nblk // 4
    i = np.arange(nblk)[:, None]
    j = np.arange(TWB)[:, None]
    n = np.arange(S)[None, :]
    grp = i // per_group
    blk = 2 * TWB * (i % per_group) * 2 * n
    blk = np.where((grp == 0) | (grp == 3), blk, -blk)
    in_block = [(2 * j + 1) * 2 * n, (2 * S - 1 - 2 * j) * 2 * n,
                (S - 1 - 2 * j) * 2 * n, (S + 1 + 2 * j) * 2 * n]
    return _trig_factors(np.concatenate([blk] + in_block, axis=0))


@functools.lru_cache(maxsize=None)
def _hyena_features():
    t = np.arange(S, dtype=np.float64)
    t_norm = t / S
    w = 2.0 * math.pi * t / S
    bands = np.linspace(1e-4, HY_BANDS - 1, HY_BANDS)
    fw = w[:, None] * bands[None]
    feats = np.concatenate([t_norm[:, None], np.cos(fw), -np.sin(fw)], axis=-1)
    feats = np.pad(feats, ((0, 0), (0, HY_EMB_PAD - HY_EMB)))
    max_decay = math.log(1e-2) / 0.3
    min_decay = math.log(1e-2) / 1.5
    deltas = np.abs(np.linspace(min_decay, max_decay, D))
    return feats.astype(np.float32), deltas.reshape(1, D).astype(np.float32)


def kernel(x, c, ctx, c_ctx, w_mod, b_mod, norm1_g, norm2_g, mlp_w1, mlp_w2, a_w_in, a_q_a_g, a_w_q_b, a_kv_a_g, a_w_kv_b, a_q_norm_g, a_k_norm_g, a_w_out, h_w_in, h_conv_w, h_conv_b, h_f_w1, h_f_b1, h_f_w2, h_f_b2, h_f_w3, h_f_b3, h_f_freq, h_f_w4, h_skip, h_w_out, final_g):
    assert x.shape == (B, S, D) and ctx.shape == (B, CTX, D) and w_mod.shape[0] == 2

    c_rows = jnp.concatenate([c, c_ctx[None], jnp.zeros((MOD_ROWS - B - 1, D), F32)], axis=0)
    mods = _mods(c_rows, w_mod, b_mod)

    def lat_mod(i, k):
        return mods[i, :B, k * D:(k + 1) * D].reshape(B, 1, D)

    def ctx_mod(i, k):
        return mods[i, B:B + 1, k * D:(k + 1) * D]

    w_in = a_w_in[0]
    o_kr = Q_LORA + KV_LORA
    o_gq = o_kr + MLA_ROPE
    o_gk = o_gq + GQA_HEADS * HEAD
    o_gv = o_gk + GQA_KV * HEAD
    win = jnp.concatenate([
        w_in[:, :o_kr],
        _pad_heads(w_in[:, o_kr:o_gq], 1, MLA_ROPE, front=HEAD),
        _pad_heads(w_in[:, o_gq:o_gk], GQA_HEADS, HEAD),
        _pad_heads(w_in[:, o_gk:o_gv], GQA_KV, HEAD),
        _pad_heads(w_in[:, o_gv:], GQA_KV, HEAD),
    ], axis=1).astype(BF16)
    wqb = _pad_heads(a_w_q_b[0], MLA_HEADS, HEAD + MLA_ROPE).astype(BF16)
    wkv = a_w_kv_b[0].reshape(KV_LORA, MLA_HEADS, 2 * HEAD)
    wk = _pad_heads(wkv[:, :, :HEAD].reshape(KV_LORA, MLA_HEADS * HEAD), MLA_HEADS, HEAD).astype(BF16)
    wv = _pad_heads(wkv[:, :, HEAD:].reshape(KV_LORA, MLA_HEADS * HEAD), MLA_HEADS, HEAD).astype(BF16)
    qng = jnp.pad(a_q_norm_g[0], (0, LANE - HEAD)).reshape(1, LANE)
    kng = jnp.pad(a_k_norm_g[0], (0, LANE - HEAD)).reshape(1, LANE)

    q, k, vt = _attn_prep(x, ctx, norm1_g[0:1], lat_mod(0, 1), lat_mod(0, 0),
                          ctx_mod(0, 1), ctx_mod(0, 0), _rope_tables(), win, wqb, wk, wv,
                          a_q_a_g[0:1], a_kv_a_g[0:1], qng, kng)
    o = _attention(q, k, vt)
    x = _mlp(o, a_w_out[0].astype(BF16), lat_mod(0, 2), x,
             norm2_g[0:1], lat_mod(0, 4), lat_mod(0, 3), lat_mod(0, 5),
             mlp_w1[0].astype(BF16), mlp_w2[0].astype(BF16), final_g.reshape(1, D), final=False)

    feats, deltas = _hyena_features()
    kr, ki = _hyena_filters(
        feats, jnp.pad(h_f_w1[0], ((0, HY_EMB_PAD - HY_EMB), (0, 0))), h_f_b1[0:1],
        h_f_w2[0], h_f_b2[0:1], h_f_w3[0], h_f_b3[0:1], h_f_freq[0], h_f_w4[0], deltas,
        _filter_trig_factors())
    z = _modmm(x, norm1_g[1:2], lat_mod(1, 1), lat_mod(1, 0), h_w_in[0].astype(BF16))
    y = _hyena_conv(z, h_conv_w[0], h_conv_b[0:1], kr, ki, h_skip[0], _conv_trig_factors())
    x = _mlp(y, h_w_out[0].astype(BF16), lat_mod(1, 2), x,
             norm2_g[1:2], lat_mod(1, 4), lat_mod(1, 3), lat_mod(1, 5),
             mlp_w1[1].astype(BF16), mlp_w2[1].astype(BF16), final_g.reshape(1, D), final=True)
    return x
```
